```python
import jax
import jax.numpy as jnp
from jax import lax
import numpy as np

D_MODEL = 1024
BATCH = 2
SEQ = 16384
DEPTH = 2

GRID_W = 64
CTX_LEN = 256
N_BRANCH = 4
BR = 512
HD = 64
EPS = 1e-6
ROPE_BASE = 10000.0
Q_BLOCK = 128

NA_HEADS = BR // HD
NA_WIN_R = 8
NA_WIN_C = 16

RW_HS = 64
RW_HEADS = BR // RW_HS
RW_DECAY_LORA = 64
RW_AAA_LORA = 64
RW_GN_EPS = 64e-5

MLA_HEADS = 8
MLA_Q_RANK = 256
MLA_KV_RANK = 256
MLA_NOPE = 64
MLA_ROPE = 32
MLA_V = BR // MLA_HEADS

GQA_HEADS = BR // HD
GQA_KV_HEADS = 2

A_IN = 4 * BR
B_SHIFT = 3 * BR + 2 * (RW_DECAY_LORA + RW_AAA_LORA)
B_IN = B_SHIFT + BR
C_IN = MLA_Q_RANK + MLA_KV_RANK + MLA_ROPE + BR
D_IN = (GQA_HEADS + 2 * GQA_KV_HEADS) * HD + BR
N_IN = A_IN + B_IN + C_IN + D_IN

kernel_name = "hybrid_gated_natten_rwkv7_mla_gqa_block"


def split_cols(u, sizes):
    out, off = [], 0
    for s in sizes:
        out.append(u[..., off:off + s])
        off += s
    return out


def heads(t, n):
    return t.reshape(t.shape[:-1] + (n, t.shape[-1] // n))


def rmsnorm(t, g):
    t32 = t.astype(jnp.float32)
    return (t32 * lax.rsqrt(jnp.mean(t32 * t32, -1, keepdims=True) + EPS)).astype(t.dtype) * g


def axial_rope_tables(n_tok, d_rot):
    t = jnp.arange(n_tok)
    row = (t // GRID_W).astype(jnp.float32)
    col = (t % GRID_W).astype(jnp.float32)
    n_freq = d_rot // 4
    inv = ROPE_BASE ** (-jnp.arange(n_freq, dtype=jnp.float32) / n_freq)
    ang = jnp.concatenate([row[:, None] * inv, col[:, None] * inv], -1)
    return jnp.cos(ang), jnp.sin(ang)


def apply_rope(t, cos, sin):
    half = t.shape[-1] // 2
    t1, t2 = t[..., :half], t[..., half:]
    c, s = cos[:, None, :], sin[:, None, :]
    return jnp.concatenate([t1 * c - t2 * s, t1 * s + t2 * c], -1).astype(t.dtype)


def rope_tail(t, cos, sin, n):
    return jnp.concatenate([t[..., :-n], apply_rope(t[..., -n:], cos, sin)], -1)


def attend(q, k, v, scale):
    s = jnp.einsum('btgrd,bngd->bgrtn', q, k).astype(jnp.float32) * scale
    p = jax.nn.softmax(s, axis=-1).astype(v.dtype)
    o = jnp.einsum('bgrtn,bngd->btgrd', p, v)
    return o.reshape(o.shape[:2] + (-1,))


def attend_blocked(q, k, v, scale):
    b, t = q.shape[:2]
    qb = jnp.moveaxis(q.reshape((b, t // Q_BLOCK, Q_BLOCK) + q.shape[2:]), 1, 0)
    o = lax.map(lambda qi: attend(qi, k, v, scale), qb)
    return jnp.moveaxis(o, 0, 1).reshape(b, t, -1)


def neighbourhood_attention(q, k, v, k_ctx, v_ctx, rpb):
    b, n_tok, nh, d = q.shape
    rows = n_tok // GRID_W
    kr = min(NA_WIN_R, rows)
    kc = NA_WIN_C
    n_win = kr * kc
    qg = q.reshape(b, rows, GRID_W, nh, d)
    kg = k.reshape(b, rows, GRID_W, nh, d)
    vg = v.reshape(b, rows, GRID_W, nh, d)
    cols = np.arange(GRID_W)
    col_idx = np.clip(cols - kc // 2, 0, GRID_W - kc)[:, None] + np.arange(kc)[None, :]
    dc_idx = col_idx - cols[:, None] + (NA_WIN_C - 1)
    scale = d ** -0.5

    def one_row(r):
        rs = jnp.clip(r - kr // 2, 0, rows - kr)
        k_win = lax.dynamic_slice_in_dim(kg, rs, kr, axis=1)[:, :, col_idx]
        v_win = lax.dynamic_slice_in_dim(vg, rs, kr, axis=1)[:, :, col_idx]
        q_row = lax.dynamic_index_in_dim(qg, r, axis=1, keepdims=False)
        dr_idx = rs + jnp.arange(kr) - r + (NA_WIN_R - 1)
        bias = rpb[:, dr_idx[None, :, None], dc_idx[:, None, :]]
        s_win = jnp.einsum('bqhd,brqchd->bhqrc', q_row, k_win).astype(jnp.float32) * scale + bias
        s_ctx = jnp.einsum('bqhd,bnhd->bhqn', q_row, k_ctx).astype(jnp.float32) * scale
        s = jnp.concatenate([s_win.reshape(b, nh, GRID_W, n_win), s_ctx], -1)
        p = jax.nn.softmax(s, axis=-1).astype(v.dtype)
        p_win = p[..., :n_win].reshape(b, nh, GRID_W, kr, kc)
        return (jnp.einsum('bhqrc,brqchd->bqhd', p_win, v_win)
                + jnp.einsum('bhqn,bnhd->bqhd', p[..., n_win:], v_ctx))

    o = lax.map(one_row, jnp.arange(rows))
    return jnp.moveaxis(o, 0, 1).reshape(b, n_tok, nh * d)


def mixer_neigh(p, pc, q_g, k_g, rpb, need_ctx):
    q, k, v, g = split_cols(p, (BR,) * 4)
    qc, kc, vc, gc = split_cols(pc, (BR,) * 4)
    k_ctx = rmsnorm(heads(kc, NA_HEADS), k_g)
    v_ctx = heads(vc, NA_HEADS)
    y = neighbourhood_attention(rmsnorm(heads(q, NA_HEADS), q_g), rmsnorm(heads(k, NA_HEADS), k_g),
                                heads(v, NA_HEADS), k_ctx, v_ctx, rpb) * jax.nn.silu(g)
    y_c = None
    if need_ctx:
        qh = rmsnorm(heads(qc, NA_HEADS), q_g)[:, :, :, None, :]
        y_c = attend(qh, k_ctx, v_ctx, HD ** -0.5) * jax.nn.silu(gc)
    return y, y_c


def centred_shift(u):
    pad = jnp.pad(u, ((0, 0), (1, 1), (0, 0)))
    return 0.5 * (pad[:, :-2] + pad[:, 2:])


def rwkv_prep(p, mu, w0, w2, a0, a2, k_k, k_a):
    u, g = p[..., :B_SHIFT], p[..., B_SHIFT:]
    u = (u + mu * (centred_shift(u) - u)).astype(jnp.float32)
    r, k, v, wl, al = split_cols(u, (BR, BR, BR, 2 * RW_DECAY_LORA, 2 * RW_AAA_LORA))
    kk = heads(k * k_k, RW_HEADS)
    kk = (kk * lax.rsqrt(jnp.sum(kk * kk, -1, keepdims=True) + 1e-12)).reshape(k.shape)
    dirs = []
    for d in range(2):
        wl_d = wl[..., d * RW_DECAY_LORA:(d + 1) * RW_DECAY_LORA]
        al_d = al[..., d * RW_AAA_LORA:(d + 1) * RW_AAA_LORA]
        w_log = -jax.nn.softplus(-(w0[d] + jnp.tanh(wl_d) @ w2[d])) - 0.5
        a = jax.nn.sigmoid(a0[d] + al_d @ a2[d])
        dirs.append((jnp.exp(-jnp.exp(w_log)), k * (1.0 + (a - 1.0) * k_a), kk * a))
    return r, v, kk, dirs, g


def rwkv7_scan(r, w, k, v, a, b, s0, readout):
    def step(state, inp):
        w_t, k_t, v_t, a_t, b_t = inp[:5]
        sa = jnp.einsum('bhij,bhj->bhi', state, a_t)
        state = (state * w_t[:, :, None, :] + sa[..., None] * b_t[:, :, None, :]
                 + v_t[..., None] * k_t[:, :, None, :])
        y = jnp.einsum('bhij,bhj->bhi', state, inp[5]) if readout else None
        return state, y
    seqs = (w, k, v, a, b) + ((r,) if readout else ())
    s_fin, ys = lax.scan(step, s0, tuple(jnp.moveaxis(t, 1, 0) for t in seqs))
    return (jnp.moveaxis(ys, 0, 1) if readout else None), s_fin


def head_groupnorm(y, w, b):
    mean = jnp.mean(y, -1, keepdims=True)
    var = jnp.mean((y - mean) ** 2, -1, keepdims=True)
    yn = (y - mean) * lax.rsqrt(var + RW_GN_EPS)
    return yn.reshape(y.shape[:-2] + (-1,)) * w + b


def mixer_rwkv(p, pc, mu, w0, w2, a0, a2, k_k, k_a, r_k, gn_w, gn_b, need_ctx):
    lat = rwkv_prep(p, mu, w0, w2, a0, a2, k_k, k_a)
    cxt = rwkv_prep(pc, mu, w0, w2, a0, a2, k_k, k_a)

    def run(parts, d, s0, readout):
        r, v, kk, dirs, _ = parts
        decay, k_d, b_d = dirs[d]
        ts = [heads(t, RW_HEADS) for t in (r, decay, k_d, v, -kk, b_d)]
        if d == 1:
            ts = [jnp.flip(t, 1) for t in ts]
        y, s_fin = rwkv7_scan(*ts, s0, readout)
        if readout and d == 1:
            y = jnp.flip(y, 1)
        return y, s_fin

    def finish(parts, y_sum):
        r, v, kk, dirs, g = parts
        rh = heads(r, RW_HEADS)
        k_sum = heads(dirs[0][1] + dirs[1][1], RW_HEADS)
        bonus = jnp.sum(rh * k_sum * r_k, -1, keepdims=True) * heads(v, RW_HEADS)
        y = head_groupnorm(y_sum, gn_w, gn_b) + bonus.reshape(r.shape)
        return y.astype(g.dtype) * jax.nn.silu(g)

    s0 = jnp.zeros((p.shape[0], RW_HEADS, RW_HS, RW_HS), jnp.float32)
    y_lat, y_ctx = 0.0, 0.0
    for d in range(2):
        yc, s_c = run(cxt, d, s0, need_ctx)
        yl, _ = run(lat, d, s_c, True)
        y_lat = y_lat + yl
        if need_ctx:
            y_ctx = y_ctx + yc
    return finish(lat, y_lat), (finish(cxt, y_ctx) if need_ctx else None)


def mla_project(p, qa_g, kva_g, wuq, wukv, q_g, k_g, rope, want_q):
    cq, ckv, kr, g = split_cols(p, (MLA_Q_RANK, MLA_KV_RANK, MLA_ROPE, BR))
    kv = heads(rmsnorm(ckv, kva_g) @ wukv, MLA_HEADS)
    k_nope, v = kv[..., :MLA_NOPE], kv[..., MLA_NOPE:]
    k_pe = jnp.broadcast_to(kr[..., None, :], k_nope.shape[:-1] + (MLA_ROPE,))
    k = rmsnorm(jnp.concatenate([k_nope, k_pe], -1), k_g)
    q = rmsnorm(heads(rmsnorm(cq, qa_g) @ wuq, MLA_HEADS), q_g) if want_q else None
    if rope is not None:
        k = rope_tail(k, rope[0], rope[1], MLA_ROPE)
        q = rope_tail(q, rope[0], rope[1], MLA_ROPE) if want_q else None
    return q, k, v, g


def mixer_mla(p, pc, qa_g, kva_g, wuq, wukv, q_g, k_g, rope, need_ctx):
    q, k, v, g = mla_project(p, qa_g, kva_g, wuq, wukv, q_g, k_g, rope, True)
    qc, kc, vc, gc = mla_project(pc, qa_g, kva_g, wuq, wukv, q_g, k_g, None, need_ctx)
    scale = (MLA_NOPE + MLA_ROPE) ** -0.5
    k_all = jnp.concatenate([kc, k], 1)
    v_all = jnp.concatenate([vc, v], 1)
    y = attend_blocked(q[:, :, :, None], k_all, v_all, scale) * jax.nn.silu(g)
    y_c = attend(qc[:, :, :, None], kc, vc, scale) * jax.nn.silu(gc) if need_ctx else None
    return y, y_c


def gqa_project(p, q_g, k_g, rope, want_q):
    q, k, v, g = split_cols(p, (GQA_HEADS * HD, GQA_KV_HEADS * HD, GQA_KV_HEADS * HD, BR))
    k = rmsnorm(heads(k, GQA_KV_HEADS), k_g)
    v = heads(v, GQA_KV_HEADS)
    q = rmsnorm(heads(q, GQA_HEADS), q_g) if want_q else None
    if rope is not None:
        k = apply_rope(k, rope[0], rope[1])
        q = apply_rope(q, rope[0], rope[1]) if want_q else None
    if want_q:
        q = q.reshape(q.shape[:2] + (GQA_KV_HEADS, GQA_HEADS // GQA_KV_HEADS, HD))
    return q, k, v, g


def mixer_gqa(p, pc, q_g, k_g, rope, need_ctx):
    q, k, v, g = gqa_project(p, q_g, k_g, rope, True)
    qc, kc, vc, gc = gqa_project(pc, q_g, k_g, None, need_ctx)
    scale = HD ** -0.5
    y = attend_blocked(q, jnp.concatenate([kc, k], 1), jnp.concatenate([vc, v], 1), scale) * jax.nn.silu(g)
    y_c = attend(qc, kc, vc, scale) * jax.nn.silu(gc) if need_ctx else None
    return y, y_c


def merge_branches(h, ys, mg_w, mg_b, w_br, w_out):
    acc = None
    for i in range(N_BRANCH):
        term = jax.nn.sigmoid(h @ mg_w[i] + mg_b[i]) * (ys[i] @ w_br[i])
        acc = term if acc is None else acc + term
    return acc @ w_out


def setup_inputs(seed: int = 0) -> dict:
    key = jax.random.key(seed)
    ks = iter(jax.random.split(key, 48))
    L, D = DEPTH, D_MODEL

    def nrm(shape, s):
        return jax.random.normal(next(ks), shape, jnp.float32) * s

    def gain(shape):
        return 1.0 + nrm(shape, 0.02)

    return {
        "x": nrm((BATCH, SEQ, D), 1.0),
        "c": nrm((BATCH, D), 1.0),
        "ctx": nrm((BATCH, CTX_LEN, D), 1.0),
        "c_ctx": nrm((D,), 1.0),
        "norm_g": gain((L, D)),
        "mod_w": nrm((L, D, 3 * D), 0.5 * D ** -0.5),
        "mod_b": nrm((L, 3 * D), 0.02),
        "w_in": nrm((L, D, N_IN), D ** -0.5),
        "na_qg": gain((L, HD)),
        "na_kg": gain((L, HD)),
        "na_rpb": nrm((L, NA_HEADS, 2 * NA_WIN_R - 1, 2 * NA_WIN_C - 1), 0.1),
        "rw_mu": jax.random.uniform(next(ks), (L, B_SHIFT), jnp.float32, 0.0, 1.0),
        "rw_w0": jax.random.uniform(next(ks), (L, 2, BR), jnp.float32, -6.0, 1.0),
        "rw_w2": nrm((L, 2, RW_DECAY_LORA, BR), 0.1 * RW_DECAY_LORA ** -0.5),
        "rw_a0": nrm((L, 2, BR), 0.5),
        "rw_a2": nrm((L, 2, RW_AAA_LORA, BR), 0.1 * RW_AAA_LORA ** -0.5),
        "rw_kk": 0.85 + nrm((L, BR), 0.02),
        "rw_ka": gain((L, BR)),
        "rw_rk": nrm((L, RW_HEADS, RW_HS), 0.1),
        "rw_gn_w": gain((L, BR)),
        "rw_gn_b": nrm((L, BR), 0.02),
        "mla_qa_g": gain((L, MLA_Q_RANK)),
        "mla_kva_g": gain((L, MLA_KV_RANK)),
        "mla_wuq": nrm((L, MLA_Q_RANK, MLA_HEADS * (MLA_NOPE + MLA_ROPE)), MLA_Q_RANK ** -0.5),
        "mla_wukv": nrm((L, MLA_KV_RANK, MLA_HEADS * (MLA_NOPE + MLA_V)), MLA_KV_RANK ** -0.5),
        "mla_qg": gain((L, MLA_NOPE + MLA_ROPE)),
        "mla_kg": gain((L, MLA_NOPE + MLA_ROPE)),
        "gqa_qg": gain((L, HD)),
        "gqa_kg": gain((L, HD)),
        "mg_w": nrm((L, N_BRANCH, D, D), D ** -0.5),
        "mg_b": nrm((L, N_BRANCH, D), 0.02),
        "w_br": nrm((L, N_BRANCH, BR, D), BR ** -0.5),
        "w_out": nrm((L, D, D), D ** -0.5),
    }


def reference(x, c, ctx, c_ctx, norm_g, mod_w, mod_b, w_in,
              na_qg, na_kg, na_rpb,
              rw_mu, rw_w0, rw_w2, rw_a0, rw_a2, rw_kk, rw_ka, rw_rk, rw_gn_w, rw_gn_b,
              mla_qa_g, mla_kva_g, mla_wuq, mla_wukv, mla_qg, mla_kg,
              gqa_qg, gqa_kg,
              mg_w, mg_b, w_br, w_out):
    n_tok = x.shape[1]
    rope_hd = axial_rope_tables(n_tok, HD)
    rope_mla = axial_rope_tables(n_tok, MLA_ROPE)
    cx = ctx
    for l in range(DEPTH):
        need_ctx = l < DEPTH - 1
        sh, sc, gt = jnp.split(jax.nn.silu(c) @ mod_w[l] + mod_b[l], 3, axis=-1)
        sh_c, sc_c, gt_c = jnp.split(jax.nn.silu(c_ctx) @ mod_w[l] + mod_b[l], 3, axis=-1)
        h = rmsnorm(x, norm_g[l]) * (1.0 + sc[:, None]) + sh[:, None]
        hc = rmsnorm(cx, norm_g[l]) * (1.0 + sc_c) + sh_c
        pa, pb, pm, pd = split_cols(h @ w_in[l], (A_IN, B_IN, C_IN, D_IN))
        pa_c, pb_c, pm_c, pd_c = split_cols(hc @ w_in[l], (A_IN, B_IN, C_IN, D_IN))

        ya, ya_c = mixer_neigh(pa, pa_c, na_qg[l], na_kg[l], na_rpb[l], need_ctx)
        yb, yb_c = mixer_rwkv(pb, pb_c, rw_mu[l], rw_w0[l], rw_w2[l], rw_a0[l], rw_a2[l],
                              rw_kk[l], rw_ka[l], rw_rk[l], rw_gn_w[l], rw_gn_b[l], need_ctx)
        yc, yc_c = mixer_mla(pm, pm_c, mla_qa_g[l], mla_kva_g[l], mla_wuq[l], mla_wukv[l],
                             mla_qg[l], mla_kg[l], rope_mla, need_ctx)
        yd, yd_c = mixer_gqa(pd, pd_c, gqa_qg[l], gqa_kg[l], rope_hd, need_ctx)

        x = x + gt[:, None] * merge_branches(h, (ya, yb, yc, yd), mg_w[l], mg_b[l], w_br[l], w_out[l])
        if need_ctx:
            cx = cx + gt_c * merge_branches(hc, (ya_c, yb_c, yc_c, yd_c), mg_w[l], mg_b[l], w_br[l], w_out[l])
    return x
```

```python
import functools

import numpy as np
import jax
import jax.numpy as jnp
from jax import lax
from jax.experimental import pallas as pl
from jax.experimental.pallas import tpu as pltpu

F32 = jnp.float32
BF16 = jnp.bfloat16
HIGHEST = lax.Precision.HIGHEST

D_MODEL = 1024
GRID_W = 64
BR = 512
HD = 64
N_HEADS = BR // HD
EPS = 1e-6
ROPE_BASE = 10000.0
NA_WIN_R = 8
NA_WIN_C = 16
RW_LORA = 64
RW_GN_EPS = 64e-5
MLA_RANK = 256
MLA_NOPE = 64
MLA_ROPE = 32
GQA_KV_HEADS = 2
U_COLS = 3 * BR + 4 * RW_LORA

LANES = 128
VMEM_LIMIT = 56 * 1024 * 1024

TM = 256
CHUNK = 64
NA_ROWS = 32


def _params(*sem):
    return pltpu.CompilerParams(dimension_semantics=sem, vmem_limit_bytes=VMEM_LIMIT)


def _dot(a, b):
    return jnp.dot(a.astype(BF16), b.astype(BF16), preferred_element_type=F32)


def _dot_nt(a, b):
    return lax.dot_general(a.astype(BF16), b.astype(BF16), (((1,), (1,)), ((), ())),
                           preferred_element_type=F32)


def _hdot(a, b):
    return jnp.dot(a, b, precision=HIGHEST, preferred_element_type=F32)


def _hdot_nt(a, b):
    return lax.dot_general(a, b, (((1,), (1,)), ((), ())), precision=HIGHEST,
                           preferred_element_type=F32)


def _hdot_tn(a, b):
    return lax.dot_general(a, b, (((0,), (0,)), ((), ())), precision=HIGHEST,
                           preferred_element_type=F32)


def _silu(t):
    return t / (1.0 + jnp.exp(-t))


def _sigmoid(t):
    return 1.0 / (1.0 + jnp.exp(-t))


def _mod_kernel(c_ref, w_ref, b_ref, o_ref):
    o_ref[...] = _dot(_silu(c_ref[...]), w_ref[...]) + b_ref[...]


def _modulation(cc, w, b):
    n = cc.shape[0]
    return pl.pallas_call(
        _mod_kernel,
        out_shape=jax.ShapeDtypeStruct((n, w.shape[1]), F32),
        compiler_params=_params(),
        name="adaln_mod",
    )(cc, w, b.reshape(1, -1))


def _norm_proj_kernel(x_ref, g_ref, sc_ref, sh_ref, w_ref, o_ref, *h_ref):
    x = x_ref[0]
    xn = x * lax.rsqrt(jnp.mean(x * x, -1, keepdims=True) + EPS) * g_ref[...]
    h = (xn * (1.0 + sc_ref[0, 0]) + sh_ref[0, 0]).astype(BF16)
    o_ref[0] = jnp.dot(h, w_ref[...], preferred_element_type=F32)
    if h_ref:
        h_ref[0][0] = h


def _norm_proj(x_all, g, sc, sh, w, n_ctx_tiles, want_h):
    b, n, d = x_all.shape
    ncol = w.shape[1]
    kind = lambda i: jnp.where(i < n_ctx_tiles, 0, 1)
    out_shape = [jax.ShapeDtypeStruct((b, n, ncol), F32)]
    out_specs = [pl.BlockSpec((1, TM, ncol), lambda bi, i: (bi, i, 0))]
    if want_h:
        out_shape.append(jax.ShapeDtypeStruct((b, n, d), BF16))
        out_specs.append(pl.BlockSpec((1, TM, d), lambda bi, i: (bi, i, 0)))
    return pl.pallas_call(
        _norm_proj_kernel,
        grid=(b, n // TM),
        in_specs=[
            pl.BlockSpec((1, TM, d), lambda bi, i: (bi, i, 0)),
            pl.BlockSpec((1, d), lambda bi, i: (0, 0)),
            pl.BlockSpec((1, 1, 1, d), lambda bi, i: (bi, kind(i), 0, 0)),
            pl.BlockSpec((1, 1, 1, d), lambda bi, i: (bi, kind(i), 0, 0)),
            pl.BlockSpec((d, ncol), lambda bi, i: (0, 0)),
        ],
        out_specs=out_specs,
        out_shape=out_shape,
        compiler_params=_params("parallel", "parallel"),
        name="norm_proj",
    )(x_all, g.reshape(1, d), sc, sh, w)


def _head_norm(xh, gain, inv_d):
    ms = jnp.sum(xh * xh, -1, keepdims=True) * inv_d
    return xh * lax.rsqrt(ms + EPS) * gain


def _rope(xh, cos, sin, lo, half):
    lane = lax.broadcasted_iota(jnp.int32, xh.shape, 1)
    swapped = jnp.where(lane < lo + half, pltpu.roll(xh, LANES - half, 1), pltpu.roll(xh, half, 1))
    return xh * cos + swapped * sin


def _na_prep_kernel(p_ref, qg_ref, kg_ref, q_ref, k_ref, v_ref):
    qoff, koff, voff = BR, BR + N_HEADS * LANES, BR + 2 * N_HEADS * LANES
    for h in range(N_HEADS):
        qh = p_ref[0, :, qoff + h * LANES:qoff + (h + 1) * LANES]
        kh = p_ref[0, :, koff + h * LANES:koff + (h + 1) * LANES]
        q_ref[0, h] = (_head_norm(qh, qg_ref[...], 1.0 / HD) * HD ** -0.5).astype(BF16)
        k_ref[0, h] = _head_norm(kh, kg_ref[...], 1.0 / HD).astype(BF16)
        v_ref[0, h] = p_ref[0, :, voff + h * HD:voff + (h + 1) * HD].astype(BF16)


def _na_prep(pa, qg, kg):
    b, n, ncol = pa.shape
    hm = lambda bi, i: (bi, 0, i, 0)
    return pl.pallas_call(
        _na_prep_kernel,
        grid=(b, n // TM),
        in_specs=[pl.BlockSpec((1, TM, ncol), lambda bi, i: (bi, i, 0)),
                  pl.BlockSpec((1, LANES), lambda bi, i: (0, 0)),
                  pl.BlockSpec((1, LANES), lambda bi, i: (0, 0))],
        out_specs=[pl.BlockSpec((1, N_HEADS, TM, LANES), hm),
                   pl.BlockSpec((1, N_HEADS, TM, LANES), hm),
                   pl.BlockSpec((1, N_HEADS, TM, HD), hm)],
        out_shape=[jax.ShapeDtypeStruct((b, N_HEADS, n, LANES), BF16),
                   jax.ShapeDtypeStruct((b, N_HEADS, n, LANES), BF16),
                   jax.ShapeDtypeStruct((b, N_HEADS, n, HD), BF16)],
        compiler_params=_params("parallel", "parallel"),
        name="na_prep",
    )(pa, qg, kg)


def _gqa_prep_kernel(p_ref, qg_ref, kg_ref, cos_ref, sin_ref, q_ref, k_ref, v_ref):
    qoff, koff = BR, BR + N_HEADS * LANES
    voff = koff + GQA_KV_HEADS * LANES
    cos, sin = cos_ref[...], sin_ref[...]
    for h in range(N_HEADS):
        qh = _head_norm(p_ref[0, :, qoff + h * LANES:qoff + (h + 1) * LANES], qg_ref[...], 1.0 / HD)
        q_ref[0, h] = (_rope(qh, cos, sin, 0, HD // 2) * HD ** -0.5).astype(BF16)
    for h in range(GQA_KV_HEADS):
        kh = _head_norm(p_ref[0, :, koff + h * LANES:koff + (h + 1) * LANES], kg_ref[...], 1.0 / HD)
        k_ref[0, h] = _rope(kh, cos, sin, 0, HD // 2).astype(BF16)
        v_ref[0, h] = p_ref[0, :, voff + h * HD:voff + (h + 1) * HD].astype(BF16)


def _gqa_prep(pd, qg, kg, cos, sin):
    b, n, ncol = pd.shape
    hm = lambda bi, i: (bi, 0, i, 0)
    vec = pl.BlockSpec((1, LANES), lambda bi, i: (0, 0))
    tab = pl.BlockSpec((TM, LANES), lambda bi, i: (i, 0))
    return pl.pallas_call(
        _gqa_prep_kernel,
        grid=(b, n // TM),
        in_specs=[pl.BlockSpec((1, TM, ncol), lambda bi, i: (bi, i, 0)), vec, vec, tab, tab],
        out_specs=[pl.BlockSpec((1, N_HEADS, TM, LANES), hm),
                   pl.BlockSpec((1, GQA_KV_HEADS, TM, LANES), hm),
                   pl.BlockSpec((1, GQA_KV_HEADS, TM, HD), hm)],
        out_shape=[jax.ShapeDtypeStruct((b, N_HEADS, n, LANES), BF16),
                   jax.ShapeDtypeStruct((b, GQA_KV_HEADS, n, LANES), BF16),
                   jax.ShapeDtypeStruct((b, GQA_KV_HEADS, n, HD), BF16)],
        compiler_params=_params("parallel", "parallel"),
        name="gqa_prep",
    )(pd, qg, kg, cos, sin)


def _mla_prep_kernel(p_ref, qa_ref, kva_ref, wuq_ref, wuk_ref, wuv_ref, qg_ref, kg_ref,
                     cos_ref, sin_ref, q_ref, k_ref, v_ref):
    d_qk = MLA_NOPE + MLA_ROPE
    cq = p_ref[0, :, BR:BR + MLA_RANK]
    ckv = p_ref[0, :, BR + MLA_RANK:BR + 2 * MLA_RANK]
    kr = p_ref[0, :, BR + 2 * MLA_RANK:BR + 2 * MLA_RANK + N_HEADS * LANES]
    cqn = cq * lax.rsqrt(jnp.mean(cq * cq, -1, keepdims=True) + EPS) * qa_ref[...]
    ckvn = ckv * lax.rsqrt(jnp.mean(ckv * ckv, -1, keepdims=True) + EPS) * kva_ref[...]
    qf = _dot(cqn, wuq_ref[...])
    kf = _dot(ckvn, wuk_ref[...]) + kr
    vf = _dot(ckvn, wuv_ref[...])
    cos, sin = cos_ref[...], sin_ref[...]
    for h in range(N_HEADS):
        qh = _head_norm(qf[:, h * LANES:(h + 1) * LANES], qg_ref[...], 1.0 / d_qk)
        kh = _head_norm(kf[:, h * LANES:(h + 1) * LANES], kg_ref[...], 1.0 / d_qk)
        q_ref[0, h] = (_rope(qh, cos, sin, MLA_NOPE, MLA_ROPE // 2) * d_qk ** -0.5).astype(BF16)
        k_ref[0, h] = _rope(kh, cos, sin, MLA_NOPE, MLA_ROPE // 2).astype(BF16)
        v_ref[0, h] = vf[:, h * HD:(h + 1) * HD].astype(BF16)


def _mla_prep(pc, qa_g, kva_g, wuq, wuk, wuv, qg, kg, cos, sin):
    b, n, ncol = pc.shape
    hm = lambda bi, i: (bi, 0, i, 0)
    full = lambda a: pl.BlockSpec(a.shape, lambda bi, i: (0,) * a.ndim)
    tab = pl.BlockSpec((TM, LANES), lambda bi, i: (i, 0))
    return pl.pallas_call(
        _mla_prep_kernel,
        grid=(b, n // TM),
        in_specs=[pl.BlockSpec((1, TM, ncol), lambda bi, i: (bi, i, 0)),
                  full(qa_g), full(kva_g), full(wuq), full(wuk), full(wuv), full(qg), full(kg), tab, tab],
        out_specs=[pl.BlockSpec((1, N_HEADS, TM, LANES), hm),
                   pl.BlockSpec((1, N_HEADS, TM, LANES), hm),
                   pl.BlockSpec((1, N_HEADS, TM, HD), hm)],
        out_shape=[jax.ShapeDtypeStruct((b, N_HEADS, n, LANES), BF16),
                   jax.ShapeDtypeStruct((b, N_HEADS, n, LANES), BF16),
                   jax.ShapeDtypeStruct((b, N_HEADS, n, HD), BF16)],
        compiler_params=_params("parallel", "parallel"),
        name="mla_prep",
    )(pc, qa_g, kva_g, wuq, wuk, wuv, qg, kg, cos, sin)


def _flash_kernel(q_ref, k_ref, v_ref, o_ref, m_sc, l_sc, acc_sc, *, tk, nk):
    q = q_ref[0, 0]
    m_sc[...] = jnp.full(m_sc.shape, -jnp.inf, F32)
    l_sc[...] = jnp.zeros(l_sc.shape, F32)
    acc_sc[...] = jnp.zeros(acc_sc.shape, F32)

    def body(j, carry):
        off = pl.multiple_of(j * tk, tk)
        k = k_ref[0, 0, pl.ds(off, tk), :]
        v = v_ref[0, 0, pl.ds(off, tk), :]
        s = lax.dot_general(q, k, (((1,), (1,)), ((), ())), preferred_element_type=F32)
        m_prev = m_sc[...]
        m_new = jnp.maximum(m_prev, jnp.max(s, -1, keepdims=True))
        alpha = jnp.exp(m_prev - m_new)
        p = jnp.exp(s - m_new)
        l_sc[...] = alpha * l_sc[...] + jnp.sum(p, -1, keepdims=True)
        acc_sc[...] = alpha * acc_sc[...] + jnp.dot(p.astype(BF16), v, preferred_element_type=F32)
        m_sc[...] = m_new
        return carry

    lax.fori_loop(0, nk, body, 0)
    o_ref[0, 0] = acc_sc[...] / l_sc[...]


def _flash(q, k, v, n_keys, tq, tk):
    b, hq, t, _ = q.shape
    hk = k.shape[1]
    rep = hq // hk
    assert t % tq == 0 and n_keys % tk == 0
    kv_map = lambda bi, h, i: (bi, h // rep, 0, 0)
    return pl.pallas_call(
        functools.partial(_flash_kernel, tk=tk, nk=n_keys // tk),
        grid=(b, hq, t // tq),
        in_specs=[pl.BlockSpec((1, 1, tq, LANES), lambda bi, h, i: (bi, h, i, 0)),
                  pl.BlockSpec((1, 1, n_keys, LANES), kv_map),
                  pl.BlockSpec((1, 1, n_keys, HD), kv_map)],
        out_specs=pl.BlockSpec((1, 1, tq, HD), lambda bi, h, i: (bi, h, i, 0)),
        out_shape=jax.ShapeDtypeStruct((b, hq, t, HD), F32),
        scratch_shapes=[pltpu.VMEM((tq, 1), F32), pltpu.VMEM((tq, 1), F32), pltpu.VMEM((tq, HD), F32)],
        compiler_params=_params("parallel", "parallel", "arbitrary"),
        name="flash_attention",
    )(q, k, v)


def _na_kernel(q_ref, k_ref, v_ref, kc_ref, vc_ref, bias_ref, o_ref, *, rows):
    kr = NA_WIN_R
    kc = kc_ref[0, 0]
    vc = vc_ref[0, 0]
    r0 = pl.program_id(2) * NA_ROWS

    def body(i, carry):
        r = r0 + i
        rs = jnp.clip(r - kr // 2, 0, rows - kr)
        q = q_ref[0, 0, pl.ds(pl.multiple_of(i * GRID_W, GRID_W), GRID_W), :]
        koff = pl.multiple_of(rs * GRID_W, GRID_W)
        kw = k_ref[0, 0, pl.ds(koff, kr * GRID_W), :]
        vw = v_ref[0, 0, pl.ds(koff, kr * GRID_W), :]
        s_w = lax.dot_general(q, kw, (((1,), (1,)), ((), ())), preferred_element_type=F32)
        s_w = s_w + bias_ref[r - rs, 0]
        s_c = lax.dot_general(q, kc, (((1,), (1,)), ((), ())), preferred_element_type=F32)
        m = jnp.maximum(jnp.max(s_w, -1, keepdims=True), jnp.max(s_c, -1, keepdims=True))
        p_w = jnp.exp(s_w - m)
        p_c = jnp.exp(s_c - m)
        den = jnp.sum(p_w, -1, keepdims=True) + jnp.sum(p_c, -1, keepdims=True)
        o = (jnp.dot(p_w.astype(BF16), vw, preferred_element_type=F32)
             + jnp.dot(p_c.astype(BF16), vc, preferred_element_type=F32))
        o_ref[0, 0, pl.ds(pl.multiple_of(i * GRID_W, GRID_W), GRID_W), :] = o / den
        return carry

    lax.fori_loop(0, NA_ROWS, body, 0)


def _na_bias_table(rpb):
    cols = np.arange(GRID_W)
    cs = np.clip(cols - NA_WIN_C // 2, 0, GRID_W - NA_WIN_C)
    kcol = np.arange(GRID_W)
    inwin = (kcol[None, :] >= cs[:, None]) & (kcol[None, :] < cs[:, None] + NA_WIN_C)
    dc = np.clip(kcol[None, :] - cols[:, None] + NA_WIN_C - 1, 0, 2 * NA_WIN_C - 2)
    off = np.arange(NA_WIN_R)
    j = np.arange(NA_WIN_R)
    dr = j[None, :] - off[:, None] + NA_WIN_R - 1
    tab = rpb[:, dr[:, None, :, None], dc[None, :, None, :]]
    tab = jnp.where(inwin[None, None, :, None, :], tab, -1e30)
    return jnp.transpose(tab, (1, 0, 2, 3, 4)).reshape(NA_WIN_R, rpb.shape[0], GRID_W, NA_WIN_R * GRID_W)


def _neighbourhood(q, k, v, n_ctx, bias):
    b, h, n, _ = q.shape
    s = n - n_ctx
    rows = s // GRID_W
    assert rows >= NA_WIN_R and rows % NA_ROWS == 0
    tile = NA_ROWS * GRID_W
    qoff = n_ctx // tile
    assert n_ctx % tile == 0 or True
    q_lat, k_lat, v_lat = q[:, :, n_ctx:], k[:, :, n_ctx:], v[:, :, n_ctx:]
    whole = lambda bi, hi, i: (bi, hi, 0, 0)
    return pl.pallas_call(
        functools.partial(_na_kernel, rows=rows),
        grid=(b, h, rows // NA_ROWS),
        in_specs=[pl.BlockSpec((1, 1, tile, LANES), lambda bi, hi, i: (bi, hi, i, 0)),
                  pl.BlockSpec((1, 1, s, LANES), whole),
                  pl.BlockSpec((1, 1, s, HD), whole),
                  pl.BlockSpec((1, 1, n_ctx, LANES), whole),
                  pl.BlockSpec((1, 1, n_ctx, HD), whole),
                  pl.BlockSpec((NA_WIN_R, 1, GRID_W, NA_WIN_R * GRID_W), lambda bi, hi, i: (0, hi, 0, 0))],
        out_specs=pl.BlockSpec((1, 1, tile, HD), lambda bi, hi, i: (bi, hi, i, 0)),
        out_shape=jax.ShapeDtypeStruct((b, h, s, HD), F32),
        compiler_params=_params("parallel", "parallel", "arbitrary"),
        name="neighbourhood_attention",
    )(q_lat, k_lat, v_lat, k, v, bias)


def _seg_sum(t, ones_blk):
    return _hdot(t, ones_blk)


def _rw_prep_kernel(p_ref, prev_ref, next_ref, mu_ref, kk_ref, ka_ref, rk_ref, w0_ref, w2_ref, a0_ref,
                    a2_ref, ones_ref, r_o, v_o, nkk_o, bonus_o, lw_o, kd_o, bd_o, *, n_ctx_tiles, n_tiles):
    i = pl.program_id(1)
    u = p_ref[0, :, BR:]
    first = jnp.logical_or(i == 0, i == n_ctx_tiles)
    last = jnp.logical_or(i == n_ctx_tiles - 1, i == n_tiles - 1)
    prev_row = jnp.where(first, 0.0, prev_ref[0, 7:8, BR:])
    next_row = jnp.where(last, 0.0, next_ref[0, 0:1, BR:])
    row = lax.broadcasted_iota(jnp.int32, u.shape, 0)
    up = jnp.where(row == 0, prev_row, pltpu.roll(u, 1, 0))
    dn = jnp.where(row == TM - 1, next_row, pltpu.roll(u, TM - 1, 0))
    u = u + mu_ref[...] * (0.5 * (up + dn) - u)
    r, k, v = u[:, :BR], u[:, BR:2 * BR], u[:, 2 * BR:3 * BR]
    wl, al = u[:, 3 * BR:3 * BR + 2 * RW_LORA], u[:, 3 * BR + 2 * RW_LORA:]
    ones_blk = ones_ref[...]
    kk = k * kk_ref[...]
    kk = kk * lax.rsqrt(_seg_sum(kk * kk, ones_blk) + 1e-12)
    r_o[0] = r
    v_o[0] = v
    nkk_o[0] = -kk
    k_sum = jnp.zeros_like(k)
    for d in range(2):
        wl_d = wl[:, d * RW_LORA:(d + 1) * RW_LORA]
        al_d = al[:, d * RW_LORA:(d + 1) * RW_LORA]
        z = -(w0_ref[d] + _dot(jnp.tanh(wl_d), w2_ref[d]))
        softplus = jnp.maximum(z, 0.0) + jnp.log(1.0 + jnp.exp(-jnp.abs(z)))
        lw_o[d, 0] = -jnp.exp(-softplus - 0.5)
        a = _sigmoid(a0_ref[d] + _dot(al_d, a2_ref[d]))
        k_d = k * (1.0 + (a - 1.0) * ka_ref[...])
        kd_o[d, 0] = k_d
        bd_o[d, 0] = kk * a
        k_sum = k_sum + k_d
    bonus_o[0] = _seg_sum(r * k_sum * rk_ref[...], ones_blk) * v


def _rw_prep(pb, mu, kk, ka, rk, w0, w2, a0, a2, ones_blk, n_ctx_tiles):
    b, n, ncol = pb.shape
    nt = n // TM
    r8 = TM // 8
    full = lambda a: pl.BlockSpec(a.shape, lambda bi, i: (0,) * a.ndim)
    tok = pl.BlockSpec((1, TM, BR), lambda bi, i: (bi, i, 0))
    tok2 = pl.BlockSpec((2, 1, TM, BR), lambda bi, i: (0, bi, i, 0))
    one = jax.ShapeDtypeStruct((b, n, BR), F32)
    two = jax.ShapeDtypeStruct((2, b, n, BR), F32)
    return pl.pallas_call(
        functools.partial(_rw_prep_kernel, n_ctx_tiles=n_ctx_tiles, n_tiles=nt),
        grid=(b, nt),
        in_specs=[pl.BlockSpec((1, TM, ncol), lambda bi, i: (bi, i, 0)),
                  pl.BlockSpec((1, 8, ncol), lambda bi, i: (bi, jnp.maximum(i * r8 - 1, 0), 0)),
                  pl.BlockSpec((1, 8, ncol), lambda bi, i: (bi, jnp.minimum((i + 1) * r8, n // 8 - 1), 0)),
                  full(mu), full(kk), full(ka), full(rk), full(w0), full(w2), full(a0), full(a2),
                  full(ones_blk)],
        out_specs=[tok, tok, tok, tok, tok2, tok2, tok2],
        out_shape=[one, one, one, one, two, two, two],
        compiler_params=_params("parallel", "parallel"),
        name="rwkv_prep",
    )(pb, pb, pb, mu, kk, ka, rk, w0, w2, a0, a2, ones_blk)


def _rw_chunk(r, lw, k, v, a, b, state, sign):
    c = CHUNK
    tt = lax.broadcasted_iota(jnp.int32, (c, c), 0)
    ss = lax.broadcasted_iota(jnp.int32, (c, c), 1)
    delta = (tt - ss) * sign
    strict = delta > 0
    incl = delta >= 0
    cum = _hdot(incl.astype(F32), lw)
    tot = jnp.sum(lw, 0, keepdims=True)
    p_in = jnp.exp(cum)
    p_ex = jnp.exp(cum - lw)
    p_inv = jnp.exp(-cum)
    p_end = jnp.exp(tot - cum)
    a_t, r_t = a * p_ex, r * p_in
    b_t, k_t = b * p_inv, k * p_inv
    b_h, k_h = b * p_end, k * p_end
    zero = jnp.zeros((c, c), F32)
    lab = jnp.where(strict, _hdot_nt(a_t, b_t), zero)
    lak = jnp.where(strict, _hdot_nt(a_t, k_t), zero)
    qrb = jnp.where(incl, _hdot_nt(r_t, b_t), zero)
    qrk = jnp.where(incl, _hdot_nt(r_t, k_t), zero)
    eye = (tt == ss).astype(F32)
    tinv = eye + lab
    lp = lab
    for _ in range(5):
        lp = _hdot(lp, lp)
        tinv = tinv + _hdot(tinv, lp)
    a_hat = _hdot(tinv, a_t)
    u_hat = _hdot(tinv, _hdot(lak, v))
    u = _hdot_nt(a_hat, state) + u_hat
    y = _hdot_nt(r_t, state) + _hdot(qrb, u) + _hdot(qrk, v)
    new_state = state * jnp.exp(tot) + _hdot_tn(u, b_h) + _hdot_tn(v, k_h)
    return y, new_state


def _rw_scan_kernel(r_ref, v_ref, a_ref, lw_ref, k_ref, b_ref, y_ref, st_ref, *, heads):
    d = pl.program_id(0)
    sign = 1 - 2 * d

    @pl.when(pl.program_id(3) == 0)
    def _():
        st_ref[...] = jnp.zeros(st_ref.shape, F32)

    n_chunks = TM // CHUNK

    def body(ci, carry):
        c = jnp.where(d == 0, ci, n_chunks - 1 - ci)
        rows = pl.ds(pl.multiple_of(c * CHUNK, CHUNK), CHUNK)
        for h in range(heads):
            cols = slice(h * HD, (h + 1) * HD)
            y, st = _rw_chunk(r_ref[0, rows, cols], lw_ref[0, 0, rows, cols], k_ref[0, 0, rows, cols],
                              v_ref[0, rows, cols], a_ref[0, rows, cols], b_ref[0, 0, rows, cols],
                              st_ref[h], sign)
            y_ref[0, 0, rows, cols] = y
            st_ref[h] = st
        return carry

    lax.fori_loop(0, n_chunks, body, 0)


def _rw_scan(r, v, nkk, lw, kd, bd):
    b, n, _ = r.shape
    nt = n // TM
    heads = LANES // HD
    tile = lambda d, j: jnp.where(jnp.logical_or(d == 0, j == 0), j, nt - j)
    one = pl.BlockSpec((1, TM, LANES), lambda d, bi, g, j: (bi, tile(d, j), g))
    two = pl.BlockSpec((1, 1, TM, LANES), lambda d, bi, g, j: (d, bi, tile(d, j), g))
    return pl.pallas_call(
        functools.partial(_rw_scan_kernel, heads=heads),
        grid=(2, b, BR // LANES, nt),
        in_specs=[one, one, one, two, two, two],
        out_specs=two,
        out_shape=jax.ShapeDtypeStruct((2, b, n, BR), F32),
        scratch_shapes=[pltpu.VMEM((heads, HD, HD), F32)],
        compiler_params=_params("parallel", "parallel", "parallel", "arbitrary"),
        name="rwkv_scan",
    )(r, v, nkk, lw, kd, bd)


def _merge_kernel(x_ref, h_ref, ya_ref, yb_ref, yc_ref, yd_ref, ga_ref, gb_ref, gc_ref, gd_ref,
                  bonus_ref, gnw_ref, gnb_ref, ones_ref, mgw_ref, mgb_ref, wbr_ref, wout_ref, gt_ref, o_ref):
    h = h_ref[0]
    ones_blk = ones_ref[...]
    yb = yb_ref[0, 0] + yb_ref[1, 0]
    mean = _seg_sum(yb, ones_blk) * (1.0 / HD)
    cen = yb - mean
    var = _seg_sum(cen * cen, ones_blk) * (1.0 / HD)
    yb = cen * lax.rsqrt(var + RW_GN_EPS) * gnw_ref[...] + gnb_ref[...] + bonus_ref[0]
    ys = (ya_ref[0], yb, yc_ref[0], yd_ref[0])
    gs = (ga_ref[0], gb_ref[0], gc_ref[0], gd_ref[0])
    acc = None
    for i in range(4):
        gate = _sigmoid(jnp.dot(h, mgw_ref[i], preferred_element_type=F32) + mgb_ref[i])
        term = gate * _dot(ys[i] * _silu(gs[i]), wbr_ref[i])
        acc = term if acc is None else acc + term
    o_ref[0] = x_ref[0] + gt_ref[0, 0] * _dot(acc, wout_ref[...])


def _merge(x_all, h, ya, yb2, yc, yd, pa, pb, pc, pd, bonus, gnw, gnb, ones_blk, mgw, mgb, wbr, wout, gt,
           n_ctx_tiles):
    b, n, d = x_all.shape
    kind = lambda i: jnp.where(i < n_ctx_tiles, 0, 1)
    tok = lambda w: pl.BlockSpec((1, TM, w), lambda bi, i: (bi, i, 0))
    full = lambda a: pl.BlockSpec(a.shape, lambda bi, i: (0,) * a.ndim)
    return pl.pallas_call(
        _merge_kernel,
        grid=(b, n // TM),
        in_specs=[tok(d), tok(d), tok(BR),
                  pl.BlockSpec((2, 1, TM, BR), lambda bi, i: (0, bi, i, 0)),
                  tok(BR), tok(BR), tok(BR), tok(BR), tok(BR), tok(BR), tok(BR),
                  full(gnw), full(gnb), full(ones_blk), full(mgw), full(mgb), full(wbr), full(wout),
                  pl.BlockSpec((1, 1, 1, d), lambda bi, i: (bi, kind(i), 0, 0))],
        out_specs=tok(d),
        out_shape=jax.ShapeDtypeStruct((b, n, d), F32),
        compiler_params=_params("parallel", "parallel"),
        name="merge_out",
    )(x_all, h, ya, yb2, yc, yd, pa, pb, pc, pd, bonus, gnw, gnb, ones_blk, mgw, mgb, wbr, wout, gt)


def _pad_heads(w, n_heads, d):
    lead = w.shape[:-1]
    w = w.reshape(lead + (n_heads, d))
    w = jnp.pad(w, [(0, 0)] * len(lead) + [(0, 0), (0, LANES - d)])
    return w.reshape(lead + (n_heads * LANES,))


def _pad_vec(g):
    return jnp.pad(g, (0, LANES - g.shape[0])).reshape(1, LANES)


def _rope_tables(n_lat, n_ctx, d_rot, lo):
    t = jnp.arange(n_lat)
    row = (t // GRID_W).astype(F32)
    col = (t % GRID_W).astype(F32)
    n_freq = d_rot // 4
    inv = ROPE_BASE ** (-jnp.arange(n_freq, dtype=F32) / n_freq)
    ang = jnp.concatenate([row[:, None] * inv, col[:, None] * inv], -1)
    cos, sin = jnp.cos(ang), jnp.sin(ang)
    half = d_rot // 2
    cos_t = jnp.ones((n_lat, LANES), F32).at[:, lo:lo + half].set(cos).at[:, lo + half:lo + d_rot].set(cos)
    sin_t = jnp.zeros((n_lat, LANES), F32).at[:, lo:lo + half].set(-sin).at[:, lo + half:lo + d_rot].set(sin)
    cos_t = jnp.concatenate([jnp.ones((n_ctx, LANES), F32), cos_t], 0)
    sin_t = jnp.concatenate([jnp.zeros((n_ctx, LANES), F32), sin_t], 0)
    return cos_t, sin_t


def _heads_to_cols(o):
    b, h, t, d = o.shape
    return jnp.transpose(o, (0, 2, 1, 3)).reshape(b, t, h * d)


def _pick_tile(n, candidates):
    for c in candidates:
        if n % c == 0:
            return c
    raise ValueError(f"no tile for {n}")


def kernel(x, c, ctx, c_ctx, norm_g, mod_w, mod_b, w_in, na_qg, na_kg, na_rpb, rw_mu, rw_w0, rw_w2, rw_a0, rw_a2, rw_kk, rw_ka, rw_rk, rw_gn_w, rw_gn_b, mla_qa_g, mla_kva_g, mla_wuq, mla_wukv, mla_qg, mla_kg, gqa_qg, gqa_kg, mg_w, mg_b, w_br, w_out):
    bsz, seq, d = x.shape
    n_ctx = ctx.shape[1]
    depth = w_in.shape[0]
    assert d == D_MODEL and n_ctx == TM and seq % (NA_ROWS * GRID_W) == 0
    n = n_ctx + seq
    n_ctx_tiles = n_ctx // TM

    gqa_cos, gqa_sin = _rope_tables(seq, n_ctx, HD, 0)
    mla_cos, mla_sin = _rope_tables(seq, n_ctx, MLA_ROPE, MLA_NOPE)
    ones_blk = jnp.asarray(np.kron(np.eye(N_HEADS), np.ones((HD, HD))), F32)
    tq = _pick_tile(seq, (512, 256))
    tk = _pick_tile(n, (1280, 640, 256))

    x_all = jnp.concatenate([ctx, x], axis=1)
    cc = jnp.concatenate([c_ctx[None], c], axis=0)
    for l in range(depth):
        w = w_in[l]
        o = 0
        aq, ak, av, ag = (w[:, o + i * BR:o + (i + 1) * BR] for i in range(4))
        o += 4 * BR
        bu, bg = w[:, o:o + U_COLS], w[:, o + U_COLS:o + U_COLS + BR]
        o += U_COLS + BR
        ccq, cckv = w[:, o:o + MLA_RANK], w[:, o + MLA_RANK:o + 2 * MLA_RANK]
        ckr = w[:, o + 2 * MLA_RANK:o + 2 * MLA_RANK + MLA_ROPE]
        cg = w[:, o + 2 * MLA_RANK + MLA_ROPE:o + 2 * MLA_RANK + MLA_ROPE + BR]
        o += 2 * MLA_RANK + MLA_ROPE + BR
        dq = w[:, o:o + BR]
        dk = w[:, o + BR:o + BR + GQA_KV_HEADS * HD]
        dv = w[:, o + BR + GQA_KV_HEADS * HD:o + BR + 2 * GQA_KV_HEADS * HD]
        dg = w[:, o + BR + 2 * GQA_KV_HEADS * HD:]
        ckr_t = jnp.tile(jnp.pad(ckr, ((0, 0), (MLA_NOPE, LANES - MLA_NOPE - MLA_ROPE))), (1, N_HEADS))
        w_a = jnp.concatenate([ag, _pad_heads(aq, N_HEADS, HD), _pad_heads(ak, N_HEADS, HD), av], 1).astype(BF16)
        w_b = jnp.concatenate([bg, bu], 1).astype(BF16)
        w_c = jnp.concatenate([cg, ccq, cckv, ckr_t], 1).astype(BF16)
        w_d = jnp.concatenate([dg, _pad_heads(dq, N_HEADS, HD), _pad_heads(dk, GQA_KV_HEADS, HD), dv], 1).astype(BF16)
        wuq = _pad_heads(mla_wuq[l], N_HEADS, MLA_NOPE + MLA_ROPE).astype(BF16)
        wukv = mla_wukv[l].reshape(MLA_RANK, N_HEADS, MLA_NOPE + HD)
        wuk = _pad_heads(wukv[:, :, :MLA_NOPE].reshape(MLA_RANK, -1), N_HEADS, MLA_NOPE).astype(BF16)
        wuv = wukv[:, :, MLA_NOPE:].reshape(MLA_RANK, -1).astype(BF16)

        mod = _modulation(cc, mod_w[l], mod_b[l])
        sh, sc, gt = jnp.split(mod, 3, axis=-1)
        pair = lambda m: jnp.stack([jnp.broadcast_to(m[0], (bsz, d)), m[1:]], axis=1)[:, :, None, :]
        sh2, sc2, gt2 = pair(sh), pair(sc), pair(gt)

        pa, h = _norm_proj(x_all, norm_g[l], sc2, sh2, w_a, n_ctx_tiles, True)
        pb, = _norm_proj(x_all, norm_g[l], sc2, sh2, w_b, n_ctx_tiles, False)
        pc, = _norm_proj(x_all, norm_g[l], sc2, sh2, w_c, n_ctx_tiles, False)
        pd, = _norm_proj(x_all, norm_g[l], sc2, sh2, w_d, n_ctx_tiles, False)

        qa, ka, va = _na_prep(pa, _pad_vec(na_qg[l]), _pad_vec(na_kg[l]))
        ya_lat = _neighbourhood(qa, ka, va, n_ctx, _na_bias_table(na_rpb[l]))
        ya_ctx = _flash(qa[:, :, :n_ctx], ka, va, n_ctx, n_ctx, n_ctx)
        ya = _heads_to_cols(jnp.concatenate([ya_ctx, ya_lat], 2))

        row = lambda t: t.reshape(1, -1)
        r, v, nkk, bonus, lw, kd, bd = _rw_prep(
            pb, row(rw_mu[l]), row(rw_kk[l]), row(rw_ka[l]), row(rw_rk[l]),
            rw_w0[l][:, None, :], rw_w2[l].astype(BF16), rw_a0[l][:, None, :], rw_a2[l].astype(BF16),
            ones_blk, n_ctx_tiles)
        yb2 = _rw_scan(r, v, nkk, lw, kd, bd)

        qc, kc, vc = _mla_prep(pc, row(mla_qa_g[l]), row(mla_kva_g[l]), wuq, wuk, wuv,
                               _pad_vec(mla_qg[l]), _pad_vec(mla_kg[l]), mla_cos, mla_sin)
        yc_lat = _flash(qc[:, :, n_ctx:], kc, vc, n, tq, tk)
        yc_ctx = _flash(qc[:, :, :n_ctx], kc, vc, n_ctx, n_ctx, n_ctx)
        yc = _heads_to_cols(jnp.concatenate([yc_ctx, yc_lat], 2))

        qd, kd_, vd = _gqa_prep(pd, _pad_vec(gqa_qg[l]), _pad_vec(gqa_kg[l]), gqa_cos, gqa_sin)
        yd_lat = _flash(qd[:, :, n_ctx:], kd_, vd, n, tq, tk)
        yd_ctx = _flash(qd[:, :, :n_ctx], kd_, vd, n_ctx, n_ctx, n_ctx)
        yd = _heads_to_cols(jnp.concatenate([yd_ctx, yd_lat], 2))

        x_all = _merge(x_all, h, ya, yb2, yc, yd, pa, pb, pc, pd, bonus,
                       row(rw_gn_w[l]), row(rw_gn_b[l]), ones_blk,
                       mg_w[l].astype(BF16), mg_b[l][:, None, :], w_br[l].astype(BF16), w_out[l].astype(BF16),
                       gt2, n_ctx_tiles)
    return x_all[:, n_ctx:]
```

```python
import functools

import numpy as np
import jax
import jax.numpy as jnp
from jax import lax
from jax.experimental import pallas as pl
from jax.experimental.pallas import tpu as pltpu

F32 = jnp.float32
BF16 = jnp.bfloat16
HIGHEST = lax.Precision.HIGHEST

D_MODEL = 1024
GRID_W = 64
BR = 512
HD = 64
N_HEADS = BR // HD
EPS = 1e-6
ROPE_BASE = 10000.0
NA_WIN_R = 8
NA_WIN_C = 16
RW_LORA = 64
RW_GN_EPS = 64e-5
MLA_RANK = 256
MLA_NOPE = 64
MLA_ROPE = 32
GQA_KV_HEADS = 2
U_COLS = 3 * BR + 4 * RW_LORA

LANES = 128
BF16_SUBLANES = 16
VMEM_LIMIT = 56 * 1024 * 1024

TM = 256
CHUNK = 64
RW_TT = 128
RW_PACK = 4
FLASH_QB = 256
FLASH_KB = 256
LOG2E = 1.4426950408889634
NA_ROWS = 32


def _params(*sem):
    return pltpu.CompilerParams(dimension_semantics=sem, vmem_limit_bytes=VMEM_LIMIT)


def _dot(a, b):
    return jnp.dot(a.astype(BF16), b.astype(BF16), preferred_element_type=F32)


def _dot_nt(a, b):
    return lax.dot_general(a.astype(BF16), b.astype(BF16), (((1,), (1,)), ((), ())),
                           preferred_element_type=F32)


def _hdot(a, b):
    return jnp.dot(a, b, precision=HIGHEST, preferred_element_type=F32)


def _hdot_nt(a, b):
    return lax.dot_general(a, b, (((1,), (1,)), ((), ())), precision=HIGHEST,
                           preferred_element_type=F32)


def _hdot_tn(a, b):
    return lax.dot_general(a, b, (((0,), (0,)), ((), ())), precision=HIGHEST,
                           preferred_element_type=F32)


def _silu(t):
    return t / (1.0 + jnp.exp(-t))


def _sigmoid(t):
    return 1.0 / (1.0 + jnp.exp(-t))


def _mod_kernel(c_ref, w_ref, b_ref, o_ref):
    o_ref[...] = _dot(_silu(c_ref[...]), w_ref[...]) + b_ref[...]


def _modulation(cc, w, b):
    n = cc.shape[0]
    return pl.pallas_call(
        _mod_kernel,
        out_shape=jax.ShapeDtypeStruct((n, w.shape[1]), F32),
        compiler_params=_params(),
        name="adaln_mod",
    )(cc, w, b.reshape(1, -1))


def _norm_proj_kernel(x_ref, g_ref, sc_ref, sh_ref, w_ref, o_ref, *h_ref):
    x = x_ref[0]
    xn = x * lax.rsqrt(jnp.mean(x * x, -1, keepdims=True) + EPS) * g_ref[...]
    h = (xn * (1.0 + sc_ref[0, 0]) + sh_ref[0, 0]).astype(BF16)
    o_ref[0] = jnp.dot(h, w_ref[...], preferred_element_type=F32)
    if h_ref:
        h_ref[0][0] = h


def _norm_proj(x_all, g, sc, sh, w, n_ctx_tiles, want_h):
    b, n, d = x_all.shape
    ncol = w.shape[1]
    kind = lambda i: jnp.where(i < n_ctx_tiles, 0, 1)
    out_shape = [jax.ShapeDtypeStruct((b, n, ncol), F32)]
    out_specs = [pl.BlockSpec((1, TM, ncol), lambda bi, i: (bi, i, 0))]
    if want_h:
        out_shape.append(jax.ShapeDtypeStruct((b, n, d), BF16))
        out_specs.append(pl.BlockSpec((1, TM, d), lambda bi, i: (bi, i, 0)))
    return pl.pallas_call(
        _norm_proj_kernel,
        grid=(b, n // TM),
        in_specs=[
            pl.BlockSpec((1, TM, d), lambda bi, i: (bi, i, 0)),
            pl.BlockSpec((1, d), lambda bi, i: (0, 0)),
            pl.BlockSpec((1, 1, 1, d), lambda bi, i: (bi, kind(i), 0, 0)),
            pl.BlockSpec((1, 1, 1, d), lambda bi, i: (bi, kind(i), 0, 0)),
            pl.BlockSpec((d, ncol), lambda bi, i: (0, 0)),
        ],
        out_specs=out_specs,
        out_shape=out_shape,
        compiler_params=_params("parallel", "parallel"),
        name="norm_proj",
    )(x_all, g.reshape(1, d), sc, sh, w)


def _head_norm(xh, gain, inv_d):
    ms = jnp.sum(xh * xh, -1, keepdims=True) * inv_d
    return xh * lax.rsqrt(ms + EPS) * gain


def _rope(xh, cos, sin, lo, half):
    lane = lax.broadcasted_iota(jnp.int32, xh.shape, 1)
    swapped = jnp.where(lane < lo + half, pltpu.roll(xh, LANES - half, 1), pltpu.roll(xh, half, 1))
    return xh * cos + swapped * sin


def _na_prep_kernel(p_ref, qg_ref, kg_ref, q_ref, k_ref, v_ref):
    qoff, koff, voff = BR, BR + N_HEADS * LANES, BR + 2 * N_HEADS * LANES
    for h in range(N_HEADS):
        qh = p_ref[0, :, qoff + h * LANES:qoff + (h + 1) * LANES]
        kh = p_ref[0, :, koff + h * LANES:koff + (h + 1) * LANES]
        q_ref[0, h] = (_head_norm(qh, qg_ref[...], 1.0 / HD) * (HD ** -0.5 * LOG2E)).astype(BF16)
        k_ref[0, h] = _head_norm(kh, kg_ref[...], 1.0 / HD).astype(BF16)
        v_ref[0, h] = p_ref[0, :, voff + h * HD:voff + (h + 1) * HD].astype(BF16)


def _na_prep(pa, qg, kg):
    b, n, ncol = pa.shape
    hm = lambda bi, i: (bi, 0, i, 0)
    return pl.pallas_call(
        _na_prep_kernel,
        grid=(b, n // TM),
        in_specs=[pl.BlockSpec((1, TM, ncol), lambda bi, i: (bi, i, 0)),
                  pl.BlockSpec((1, LANES), lambda bi, i: (0, 0)),
                  pl.BlockSpec((1, LANES), lambda bi, i: (0, 0))],
        out_specs=[pl.BlockSpec((1, N_HEADS, TM, LANES), hm),
                   pl.BlockSpec((1, N_HEADS, TM, LANES), hm),
                   pl.BlockSpec((1, N_HEADS, TM, HD), hm)],
        out_shape=[jax.ShapeDtypeStruct((b, N_HEADS, n, LANES), BF16),
                   jax.ShapeDtypeStruct((b, N_HEADS, n, LANES), BF16),
                   jax.ShapeDtypeStruct((b, N_HEADS, n, HD), BF16)],
        compiler_params=_params("parallel", "parallel"),
        name="na_prep",
    )(pa, qg, kg)


def _gqa_prep_kernel(p_ref, qg_ref, kg_ref, cos_ref, sin_ref, q_ref, k_ref, v_ref):
    qoff, koff = BR, BR + N_HEADS * LANES
    voff = koff + GQA_KV_HEADS * LANES
    cos, sin = cos_ref[...], sin_ref[...]
    for h in range(N_HEADS):
        qh = _head_norm(p_ref[0, :, qoff + h * LANES:qoff + (h + 1) * LANES], qg_ref[...], 1.0 / HD)
        q_ref[0, h] = (_rope(qh, cos, sin, 0, HD // 2) * (HD ** -0.5 * LOG2E)).astype(BF16)
    for h in range(GQA_KV_HEADS):
        kh = _head_norm(p_ref[0, :, koff + h * LANES:koff + (h + 1) * LANES], kg_ref[...], 1.0 / HD)
        k_ref[0, h] = _rope(kh, cos, sin, 0, HD // 2).astype(BF16)
        v_ref[0, h] = p_ref[0, :, voff + h * HD:voff + (h + 1) * HD].astype(BF16)


def _gqa_prep(pd, qg, kg, cos, sin):
    b, n, ncol = pd.shape
    hm = lambda bi, i: (bi, 0, i, 0)
    vec = pl.BlockSpec((1, LANES), lambda bi, i: (0, 0))
    tab = pl.BlockSpec((TM, LANES), lambda bi, i: (i, 0))
    return pl.pallas_call(
        _gqa_prep_kernel,
        grid=(b, n // TM),
        in_specs=[pl.BlockSpec((1, TM, ncol), lambda bi, i: (bi, i, 0)), vec, vec, tab, tab],
        out_specs=[pl.BlockSpec((1, N_HEADS, TM, LANES), hm),
                   pl.BlockSpec((1, GQA_KV_HEADS, TM, LANES), hm),
                   pl.BlockSpec((1, GQA_KV_HEADS, TM, HD), hm)],
        out_shape=[jax.ShapeDtypeStruct((b, N_HEADS, n, LANES), BF16),
                   jax.ShapeDtypeStruct((b, GQA_KV_HEADS, n, LANES), BF16),
                   jax.ShapeDtypeStruct((b, GQA_KV_HEADS, n, HD), BF16)],
        compiler_params=_params("parallel", "parallel"),
        name="gqa_prep",
    )(pd, qg, kg, cos, sin)


def _mla_prep_kernel(p_ref, qa_ref, kva_ref, wuq_ref, wuk_ref, wuv_ref, qg_ref, kg_ref,
                     cos_ref, sin_ref, q_ref, k_ref, v_ref):
    d_qk = MLA_NOPE + MLA_ROPE
    cq = p_ref[0, :, BR:BR + MLA_RANK]
    ckv = p_ref[0, :, BR + MLA_RANK:BR + 2 * MLA_RANK]
    kr = p_ref[0, :, BR + 2 * MLA_RANK:BR + 2 * MLA_RANK + N_HEADS * LANES]
    cqn = cq * lax.rsqrt(jnp.mean(cq * cq, -1, keepdims=True) + EPS) * qa_ref[...]
    ckvn = ckv * lax.rsqrt(jnp.mean(ckv * ckv, -1, keepdims=True) + EPS) * kva_ref[...]
    qf = _dot(cqn, wuq_ref[...])
    kf = _dot(ckvn, wuk_ref[...]) + kr
    vf = _dot(ckvn, wuv_ref[...])
    cos, sin = cos_ref[...], sin_ref[...]
    for h in range(N_HEADS):
        qh = _head_norm(qf[:, h * LANES:(h + 1) * LANES], qg_ref[...], 1.0 / d_qk)
        kh = _head_norm(kf[:, h * LANES:(h + 1) * LANES], kg_ref[...], 1.0 / d_qk)
        q_ref[0, h] = (_rope(qh, cos, sin, MLA_NOPE, MLA_ROPE // 2) * (d_qk ** -0.5 * LOG2E)).astype(BF16)
        k_ref[0, h] = _rope(kh, cos, sin, MLA_NOPE, MLA_ROPE // 2).astype(BF16)
        v_ref[0, h] = vf[:, h * HD:(h + 1) * HD].astype(BF16)


def _mla_prep(pc, qa_g, kva_g, wuq, wuk, wuv, qg, kg, cos, sin):
    b, n, ncol = pc.shape
    hm = lambda bi, i: (bi, 0, i, 0)
    full = lambda a: pl.BlockSpec(a.shape, lambda bi, i: (0,) * a.ndim)
    tab = pl.BlockSpec((TM, LANES), lambda bi, i: (i, 0))
    return pl.pallas_call(
        _mla_prep_kernel,
        grid=(b, n // TM),
        in_specs=[pl.BlockSpec((1, TM, ncol), lambda bi, i: (bi, i, 0)),
                  full(qa_g), full(kva_g), full(wuq), full(wuk), full(wuv), full(qg), full(kg), tab, tab],
        out_specs=[pl.BlockSpec((1, N_HEADS, TM, LANES), hm),
                   pl.BlockSpec((1, N_HEADS, TM, LANES), hm),
                   pl.BlockSpec((1, N_HEADS, TM, HD), hm)],
        out_shape=[jax.ShapeDtypeStruct((b, N_HEADS, n, LANES), BF16),
                   jax.ShapeDtypeStruct((b, N_HEADS, n, LANES), BF16),
                   jax.ShapeDtypeStruct((b, N_HEADS, n, HD), BF16)],
        compiler_params=_params("parallel", "parallel"),
        name="mla_prep",
    )(pc, qa_g, kva_g, wuq, wuk, wuv, qg, kg, cos, sin)


def _flash_kernel(q_ref, k_ref, vt_ref, o_ref, m_sc, acc_sc, s_sc, *, tk, nk, nq):
    m_sc[...] = jnp.full(m_sc.shape, -jnp.inf, F32)
    acc_sc[...] = jnp.zeros(acc_sc.shape, F32)

    n_sub = tk // FLASH_KB
    chains = range(nq)

    def scores(row0):
        k = k_ref[0, 0, pl.ds(row0, FLASH_KB), :]
        return [lax.dot_general(k, q_ref[0, 0, c * FLASH_QB:(c + 1) * FLASH_QB, :],
                                (((1,), (1,)), ((), ())), preferred_element_type=F32) for c in chains]

    for c, s0 in zip(chains, scores(0)):
        s_sc[c] = s0

    def body(j, carry):
        off = pl.multiple_of(j * tk, tk)
        nxt = pl.multiple_of(jnp.minimum(j + 1, nk - 1) * tk, tk)
        m = [m_sc[c] for c in chains]
        acc = [acc_sc[c] for c in chains]
        s = [s_sc[c] for c in chains]
        for u in range(n_sub):
            s_next = scores(off + (u + 1) * FLASH_KB if u + 1 < n_sub else nxt)
            vt = vt_ref[0, 0, :, pl.ds(off + u * FLASH_KB, FLASH_KB)]
            for c in chains:
                m_new = jnp.maximum(m[c], jnp.max(s[c], 0, keepdims=True))
                alpha = jnp.exp2(m[c] - m_new)
                p = jnp.exp2(s[c] - m_new)
                acc[c] = alpha * acc[c] + jnp.dot(vt, p.astype(BF16), preferred_element_type=F32)
                m[c] = m_new
            s = s_next
        for c in chains:
            m_sc[c], acc_sc[c], s_sc[c] = m[c], acc[c], s[c]
        return carry

    lax.fori_loop(0, nk, body, 0)
    for c in range(nq):
        o_ref[0, 0, :, c * FLASH_QB:(c + 1) * FLASH_QB] = acc_sc[c, :HD] / acc_sc[c, HD:HD + 1]


def _flash(q, k, vt, n_keys, tq, tk):
    b, hq, t, _ = q.shape
    hk = k.shape[1]
    rep = hq // hk
    assert t % tq == 0 and n_keys % tk == 0 and tq % FLASH_QB == 0 and tk % FLASH_KB == 0
    hv = vt.shape[2]
    nq = tq // FLASH_QB
    kv_map = lambda bi, h, i: (bi, h // rep, 0, 0)
    return pl.pallas_call(
        functools.partial(_flash_kernel, tk=tk, nk=n_keys // tk, nq=nq),
        grid=(b, hq, t // tq),
        in_specs=[pl.BlockSpec((1, 1, tq, LANES), lambda bi, h, i: (bi, h, i, 0)),
                  pl.BlockSpec((1, 1, n_keys, LANES), kv_map),
                  pl.BlockSpec((1, 1, hv, n_keys), kv_map)],
        out_specs=pl.BlockSpec((1, 1, HD, tq), lambda bi, h, i: (bi, h, 0, i)),
        out_shape=jax.ShapeDtypeStruct((b, hq, HD, t), F32),
        scratch_shapes=[pltpu.VMEM((nq, 1, FLASH_QB), F32), pltpu.VMEM((nq, hv, FLASH_QB), F32),
                        pltpu.VMEM((nq, FLASH_KB, FLASH_QB), F32)],
        compiler_params=_params("parallel", "parallel", "arbitrary"),
        name="flash_attention",
    )(q, k, vt)


def _na_kernel(q_ref, k_ref, v_ref, kc_ref, vc_ref, bias_ref, o_ref, *, rows):
    kr = NA_WIN_R
    kc = kc_ref[0, 0]
    vc = vc_ref[0, 0]
    r0 = pl.program_id(2) * NA_ROWS

    def body(i, carry):
        r = r0 + i
        rs = jnp.clip(r - kr // 2, 0, rows - kr)
        q = q_ref[0, 0, pl.ds(pl.multiple_of(i * GRID_W, GRID_W), GRID_W), :]
        koff = pl.multiple_of(rs * GRID_W, GRID_W)
        kw = k_ref[0, 0, pl.ds(koff, kr * GRID_W), :]
        vw = v_ref[0, 0, pl.ds(koff, kr * GRID_W), :]
        s_w = lax.dot_general(q, kw, (((1,), (1,)), ((), ())), preferred_element_type=F32)
        s_w = s_w + bias_ref[r - rs, 0]
        s_c = lax.dot_general(q, kc, (((1,), (1,)), ((), ())), preferred_element_type=F32)
        m = jnp.maximum(jnp.max(s_w, -1, keepdims=True), jnp.max(s_c, -1, keepdims=True))
        p_w = jnp.exp2(s_w - m)
        p_c = jnp.exp2(s_c - m)
        den = jnp.sum(p_w, -1, keepdims=True) + jnp.sum(p_c, -1, keepdims=True)
        o = (jnp.dot(p_w.astype(BF16), vw, preferred_element_type=F32)
             + jnp.dot(p_c.astype(BF16), vc, preferred_element_type=F32))
        o_ref[0, 0, pl.ds(pl.multiple_of(i * GRID_W, GRID_W), GRID_W), :] = o / den
        return carry

    lax.fori_loop(0, NA_ROWS, body, 0)


def _na_bias_table(rpb):
    cols = np.arange(GRID_W)
    cs = np.clip(cols - NA_WIN_C // 2, 0, GRID_W - NA_WIN_C)
    kcol = np.arange(GRID_W)
    inwin = (kcol[None, :] >= cs[:, None]) & (kcol[None, :] < cs[:, None] + NA_WIN_C)
    by_off = jnp.stack([rpb[:, NA_WIN_R - 1 - o:2 * NA_WIN_R - 1 - o, :] for o in range(NA_WIN_R)], 0)
    pad = GRID_W
    padded = jnp.pad(by_off, ((0, 0), (0, 0), (0, 0), (pad, pad)))
    per_q = jnp.stack([padded[..., pad + NA_WIN_C - 1 - q:pad + NA_WIN_C - 1 - q + GRID_W]
                       for q in range(GRID_W)], 2)
    tab = jnp.where(inwin[None, None, :, None, :], per_q * LOG2E, -1e30)
    return tab.reshape(NA_WIN_R, rpb.shape[0], GRID_W, NA_WIN_R * GRID_W)


def _neighbourhood(q, k, v, n_ctx, bias):
    b, h, n, _ = q.shape
    s = n - n_ctx
    rows = s // GRID_W
    assert rows >= NA_WIN_R and rows % NA_ROWS == 0
    tile = NA_ROWS * GRID_W
    q_lat, k_lat, v_lat = q[:, :, n_ctx:], k[:, :, n_ctx:], v[:, :, n_ctx:]
    whole = lambda bi, hi, i: (bi, hi, 0, 0)
    return pl.pallas_call(
        functools.partial(_na_kernel, rows=rows),
        grid=(b, h, rows // NA_ROWS),
        in_specs=[pl.BlockSpec((1, 1, tile, LANES), lambda bi, hi, i: (bi, hi, i, 0)),
                  pl.BlockSpec((1, 1, s, LANES), whole),
                  pl.BlockSpec((1, 1, s, HD), whole),
                  pl.BlockSpec((1, 1, n_ctx, LANES), whole),
                  pl.BlockSpec((1, 1, n_ctx, HD), whole),
                  pl.BlockSpec((NA_WIN_R, 1, GRID_W, NA_WIN_R * GRID_W), lambda bi, hi, i: (0, hi, 0, 0))],
        out_specs=pl.BlockSpec((1, 1, tile, HD), lambda bi, hi, i: (bi, hi, i, 0)),
        out_shape=jax.ShapeDtypeStruct((b, h, s, HD), F32),
        compiler_params=_params("parallel", "parallel", "arbitrary"),
        name="neighbourhood_attention",
    )(q_lat, k_lat, v_lat, k, v, bias)


def _seg_sum(t, ones_blk):
    return _hdot(t, ones_blk)


def _rw_prep_kernel(p_ref, prev_ref, next_ref, mu_ref, kk_ref, ka_ref, rk_ref, w0_ref, w2_ref, a0_ref,
                    a2_ref, ones_ref, r_o, v_o, nkk_o, bonus_o, lw_o, kd_o, bd_o, *, n_ctx_tiles, n_tiles):
    i = pl.program_id(1)
    u = p_ref[0, :, BR:]
    first = jnp.logical_or(i == 0, i == n_ctx_tiles)
    last = jnp.logical_or(i == n_ctx_tiles - 1, i == n_tiles - 1)
    prev_row = jnp.where(first, 0.0, prev_ref[0, 7:8, BR:])
    next_row = jnp.where(last, 0.0, next_ref[0, 0:1, BR:])
    row = lax.broadcasted_iota(jnp.int32, u.shape, 0)
    up = jnp.where(row == 0, prev_row, pltpu.roll(u, 1, 0))
    dn = jnp.where(row == TM - 1, next_row, pltpu.roll(u, TM - 1, 0))
    u = u + mu_ref[...] * (0.5 * (up + dn) - u)
    r, k, v = u[:, :BR], u[:, BR:2 * BR], u[:, 2 * BR:3 * BR]
    wl, al = u[:, 3 * BR:3 * BR + 2 * RW_LORA], u[:, 3 * BR + 2 * RW_LORA:]
    ones_blk = ones_ref[...]
    kk = k * kk_ref[...]
    kk = kk * lax.rsqrt(_seg_sum(kk * kk, ones_blk) + 1e-12)
    r_o[0] = r
    v_o[0] = v
    nkk_o[0] = -kk
    k_sum = jnp.zeros_like(k)
    for d in range(2):
        wl_d = wl[:, d * RW_LORA:(d + 1) * RW_LORA]
        al_d = al[:, d * RW_LORA:(d + 1) * RW_LORA]
        z = -(w0_ref[d] + _dot(jnp.tanh(wl_d), w2_ref[d]))
        softplus = jnp.maximum(z, 0.0) + jnp.log(1.0 + jnp.exp(-jnp.abs(z)))
        lw_o[d, 0] = -jnp.exp(-softplus - 0.5)
        a = _sigmoid(a0_ref[d] + _dot(al_d, a2_ref[d]))
        k_d = k * (1.0 + (a - 1.0) * ka_ref[...])
        kd_o[d, 0] = k_d
        bd_o[d, 0] = kk * a
        k_sum = k_sum + k_d
    bonus_o[0] = _seg_sum(r * k_sum * rk_ref[...], ones_blk) * v


def _rw_prep(pb, mu, kk, ka, rk, w0, w2, a0, a2, ones_blk, n_ctx_tiles):
    b, n, ncol = pb.shape
    nt = n // TM
    r8 = TM // 8
    full = lambda a: pl.BlockSpec(a.shape, lambda bi, i: (0,) * a.ndim)
    tok = pl.BlockSpec((1, TM, BR), lambda bi, i: (bi, i, 0))
    tok2 = pl.BlockSpec((2, 1, TM, BR), lambda bi, i: (0, bi, i, 0))
    one = jax.ShapeDtypeStruct((b, n, BR), F32)
    two = jax.ShapeDtypeStruct((2, b, n, BR), F32)
    return pl.pallas_call(
        functools.partial(_rw_prep_kernel, n_ctx_tiles=n_ctx_tiles, n_tiles=nt),
        grid=(b, nt),
        in_specs=[pl.BlockSpec((1, TM, ncol), lambda bi, i: (bi, i, 0)),
                  pl.BlockSpec((1, 8, ncol), lambda bi, i: (bi, jnp.maximum(i * r8 - 1, 0), 0)),
                  pl.BlockSpec((1, 8, ncol), lambda bi, i: (bi, jnp.minimum((i + 1) * r8, n // 8 - 1), 0)),
                  full(mu), full(kk), full(ka), full(rk), full(w0), full(w2), full(a0), full(a2),
                  full(ones_blk)],
        out_specs=[tok, tok, tok, tok, tok2, tok2, tok2],
        out_shape=[one, one, one, one, two, two, two],
        compiler_params=_params("parallel", "parallel"),
        name="rwkv_prep",
    )(pb, pb, pb, mu, kk, ka, rk, w0, w2, a0, a2, ones_blk)


def _rw_scan_kernel(r_ref, v_ref, a_ref, lw_ref, k_ref, b_ref, y_ref, st_ref, *, rev, n_b):
    c, w = CHUNK, RW_PACK * HD

    @pl.when(pl.program_id(0) == 0)
    def _():
        st_ref[...] = jnp.zeros(st_ref.shape, F32)

    tt = lax.broadcasted_iota(jnp.int32, (c, w), 0)
    ss = lax.broadcasted_iota(jnp.int32, (c, w), 1) & (c - 1)
    strict = (ss > tt) if rev else (ss < tt)
    incl = (ss >= tt) if rev else (ss <= tt)
    eye = (ss == tt).astype(F32)
    t2 = lax.broadcasted_iota(jnp.int32, (c, c), 0)
    s2 = lax.broadcasted_iota(jnp.int32, (c, c), 1)
    m_incl = ((s2 >= t2) if rev else (s2 <= t2)).astype(BF16)
    blk = (lax.broadcasted_iota(jnp.int32, (w, w), 0) >> 6) == (lax.broadcasted_iota(jnp.int32, (w, w), 1) >> 6)

    def bd(t):
        tb = t.astype(BF16)
        return jnp.where(blk, jnp.concatenate([tb] * RW_PACK, axis=0), jnp.zeros((), BF16))

    def split3(t):
        hi = t.astype(BF16)
        r1 = t - hi.astype(F32)
        mid = r1.astype(BF16)
        return hi, mid, (r1 - mid.astype(F32)).astype(BF16)

    def dtn(x, y):
        return lax.dot_general(x.astype(BF16), y.astype(BF16), (((0,), (0,)), ((), ())),
                               preferred_element_type=F32)

    chunks = tuple(range(RW_TT // c))
    chunks = chunks[::-1] if rev else chunks
    chains = [(bi, g) for bi in range(n_b) for g in range(BR // w)]
    units = [(ci, bi, g) for ci in chunks for (bi, g) in chains]
    sl = lambda ref, u: ref[u[1], u[0] * c:(u[0] + 1) * c, u[2] * w:(u[2] + 1) * w]

    lw = [sl(lw_ref, u) for u in units]
    cum = []
    for t in lw:
        hi, mid, lo = split3(t)
        cum.append(jnp.dot(m_incl, hi, preferred_element_type=F32)
                   + jnp.dot(m_incl, mid, preferred_element_type=F32)
                   + jnp.dot(m_incl, lo, preferred_element_type=F32))
    tot = [jnp.sum(t, 0, keepdims=True) for t in lw]
    p_inv = [jnp.exp(-x) for x in cum]
    p_end = [jnp.exp(t - x) for t, x in zip(tot, cum)]
    a_t = [sl(a_ref, u) * jnp.exp(x - l) for u, x, l in zip(units, cum, lw)]
    r_t = [sl(r_ref, u) * jnp.exp(x) for u, x in zip(units, cum)]
    b_raw = [sl(b_ref, u) for u in units]
    k_raw = [sl(k_ref, u) for u in units]
    v = [sl(v_ref, u) for u in units]
    b_t = [x * p for x, p in zip(b_raw, p_inv)]
    k_t = [x * p for x, p in zip(k_raw, p_inv)]
    b_h = [x * p for x, p in zip(b_raw, p_end)]
    k_h = [x * p for x, p in zip(k_raw, p_end)]
    ar = [jnp.concatenate([x, y], axis=0) for x, y in zip(a_t, r_t)]
    g_b = [_dot_nt(x, bd(y)) for x, y in zip(ar, b_t)]
    g_k = [_dot_nt(x, bd(y)) for x, y in zip(ar, k_t)]
    lab = [jnp.where(strict, x[:c], 0.0) for x in g_b]
    lak = [jnp.where(strict, x[:c], 0.0) for x in g_k]
    qrb = [jnp.where(incl, x[c:], 0.0) for x in g_b]
    qrk = [jnp.where(incl, x[c:], 0.0) for x in g_k]
    tinv = [eye + x for x in lab]
    lp = lab
    for i in range(6):
        lp_bd = [bd(x) for x in lp]
        if i > 0:
            tinv = [x + _dot(x, y) for x, y in zip(tinv, lp_bd)]
        if i < 5:
            lp = [_dot(x, y) for x, y in zip(lp, lp_bd)]
    v_bd = [bd(x) for x in v]
    a_hat = [_dot(x, bd(y)) for x, y in zip(tinv, a_t)]
    wv = [_dot(x, y) for x, y in zip(lak, v_bd)]
    u_hat = [_dot(x, bd(y)) for x, y in zip(tinv, wv)]
    y_loc = [_dot(x, y) for x, y in zip(qrk, v_bd)]
    vk = [jnp.where(blk, dtn(x, y), 0.0) for x, y in zip(v, k_h)]
    p_c = [jnp.exp(t) for t in tot]

    n_ch = len(chains)
    for j in range(len(chunks)):
        idx = range(j * n_ch, (j + 1) * n_ch)
        st = [st_ref[q] for q in range(n_ch)]
        st_b = [s.astype(BF16) for s in st]
        u = [_dot_nt(a_hat[i], s) + u_hat[i] for i, s in zip(idx, st_b)]
        y = [_dot_nt(r_t[i], s) + _dot(qrb[i], bd(x)) + y_loc[i] for i, s, x in zip(idx, st_b, u)]
        new = [s * p_c[i] + jnp.where(blk, dtn(x, b_h[i]), 0.0) + vk[i] for i, s, x in zip(idx, st, u)]
        for q, i in enumerate(idx):
            y_ref[units[i][1], units[i][0] * c:(units[i][0] + 1) * c, units[i][2] * w:(units[i][2] + 1) * w] = y[q]
            st_ref[q] = new[q]


def _rw_scan(r, v, nkk, lw, kd, bd, n_ctx, rev):
    b, n, _ = r.shape
    nblk = n // RW_TT
    nctx = n_ctx // RW_TT
    if rev:
        blk_of = lambda i: jnp.where(i < nctx, nctx - 1 - i, nblk - 1 + nctx - i)
    else:
        blk_of = lambda i: i
    spec = pl.BlockSpec((b, RW_TT, BR), lambda i: (0, blk_of(i), 0))
    n_chains = b * (BR // (RW_PACK * HD))
    return pl.pallas_call(
        functools.partial(_rw_scan_kernel, rev=rev, n_b=b),
        grid=(nblk,),
        in_specs=[spec] * 6,
        out_specs=spec,
        out_shape=jax.ShapeDtypeStruct((b, n, BR), F32),
        scratch_shapes=[pltpu.VMEM((n_chains, RW_PACK * HD, RW_PACK * HD), F32)],
        compiler_params=_params("arbitrary"),
        name="rwkv_scan",
    )(r, v, nkk, lw, kd, bd)


def _merge_kernel(x_ref, h_ref, ya_ref, ybf_ref, ybr_ref, yc_ref, yd_ref, ga_ref, gb_ref, gc_ref, gd_ref,
                  bonus_ref, gnw_ref, gnb_ref, ones_ref, mgw_ref, mgb_ref, wbr_ref, wout_ref, gt_ref, o_ref):
    h = h_ref[0]
    ones_blk = ones_ref[...]
    yb = ybf_ref[0] + ybr_ref[0]
    mean = _seg_sum(yb, ones_blk) * (1.0 / HD)
    cen = yb - mean
    var = _seg_sum(cen * cen, ones_blk) * (1.0 / HD)
    yb = cen * lax.rsqrt(var + RW_GN_EPS) * gnw_ref[...] + gnb_ref[...] + bonus_ref[0]
    ys = (ya_ref[0], yb, yc_ref[0], yd_ref[0])
    gs = (ga_ref[0], gb_ref[0], gc_ref[0], gd_ref[0])
    acc = None
    for i in range(4):
        gate = _sigmoid(jnp.dot(h, mgw_ref[i], preferred_element_type=F32) + mgb_ref[i])
        term = gate * _dot(ys[i] * _silu(gs[i]), wbr_ref[i])
        acc = term if acc is None else acc + term
    o_ref[0] = x_ref[0] + gt_ref[0, 0] * _dot(acc, wout_ref[...])


def _merge(x_all, h, ya, ybf, ybr, yc, yd, pa, pb, pc, pd, bonus, gnw, gnb, ones_blk, mgw, mgb, wbr, wout, gt,
           n_ctx_tiles):
    b, n, d = x_all.shape
    kind = lambda i: jnp.where(i < n_ctx_tiles, 0, 1)
    tok = lambda w: pl.BlockSpec((1, TM, w), lambda bi, i: (bi, i, 0))
    full = lambda a: pl.BlockSpec(a.shape, lambda bi, i: (0,) * a.ndim)
    return pl.pallas_call(
        _merge_kernel,
        grid=(b, n // TM),
        in_specs=[tok(d), tok(d), tok(BR), tok(BR), tok(BR),
                  tok(BR), tok(BR), tok(BR), tok(BR), tok(BR), tok(BR), tok(BR),
                  full(gnw), full(gnb), full(ones_blk), full(mgw), full(mgb), full(wbr), full(wout),
                  pl.BlockSpec((1, 1, 1, d), lambda bi, i: (bi, kind(i), 0, 0))],
        out_specs=tok(d),
        out_shape=jax.ShapeDtypeStruct((b, n, d), F32),
        compiler_params=_params("parallel", "parallel"),
        name="merge_out",
    )(x_all, h, ya, ybf, ybr, yc, yd, pa, pb, pc, pd, bonus, gnw, gnb, ones_blk, mgw, mgb, wbr, wout, gt)


def _pad_heads(w, n_heads, d):
    lead = w.shape[:-1]
    w = w.reshape(lead + (n_heads, d))
    w = jnp.pad(w, [(0, 0)] * len(lead) + [(0, 0), (0, LANES - d)])
    return w.reshape(lead + (n_heads * LANES,))


def _pad_vec(g):
    return jnp.pad(g, (0, LANES - g.shape[0])).reshape(1, LANES)


def _rope_tables(n_lat, n_ctx, d_rot, lo):
    t = jnp.arange(n_lat)
    row = (t // GRID_W).astype(F32)
    col = (t % GRID_W).astype(F32)
    n_freq = d_rot // 4
    inv = ROPE_BASE ** (-jnp.arange(n_freq, dtype=F32) / n_freq)
    ang = jnp.concatenate([row[:, None] * inv, col[:, None] * inv], -1)
    cos, sin = jnp.cos(ang), jnp.sin(ang)
    half = d_rot // 2
    cos_t = jnp.ones((n_lat, LANES), F32).at[:, lo:lo + half].set(cos).at[:, lo + half:lo + d_rot].set(cos)
    sin_t = jnp.zeros((n_lat, LANES), F32).at[:, lo:lo + half].set(-sin).at[:, lo + half:lo + d_rot].set(sin)
    cos_t = jnp.concatenate([jnp.ones((n_ctx, LANES), F32), cos_t], 0)
    sin_t = jnp.concatenate([jnp.zeros((n_ctx, LANES), F32), sin_t], 0)
    return cos_t, sin_t


def _heads_to_cols(o):
    b, h, t, d = o.shape
    return jnp.transpose(o, (0, 2, 1, 3)).reshape(b, t, h * d)


def _headsT_to_cols(o):
    b, h, d, t = o.shape
    return jnp.transpose(o, (0, 3, 1, 2)).reshape(b, t, h * d)


def _values_t(v):
    vt = jnp.swapaxes(v, -1, -2)
    extra = jnp.zeros(vt.shape[:2] + (BF16_SUBLANES, vt.shape[3]), vt.dtype).at[:, :, 0].set(1)
    return jnp.concatenate([vt, extra], axis=2)


def _pick_tile(n, candidates):
    for c in candidates:
        if n % c == 0:
            return c
    raise ValueError(f"no tile for {n}")


def kernel(x, c, ctx, c_ctx, norm_g, mod_w, mod_b, w_in, na_qg, na_kg, na_rpb, rw_mu, rw_w0, rw_w2, rw_a0, rw_a2, rw_kk, rw_ka, rw_rk, rw_gn_w, rw_gn_b, mla_qa_g, mla_kva_g, mla_wuq, mla_wukv, mla_qg, mla_kg, gqa_qg, gqa_kg, mg_w, mg_b, w_br, w_out):
    bsz, seq, d = x.shape
    n_ctx = ctx.shape[1]
    depth = w_in.shape[0]
    assert d == D_MODEL and n_ctx == TM and seq % (NA_ROWS * GRID_W) == 0
    n = n_ctx + seq
    n_ctx_tiles = n_ctx // TM

    gqa_cos, gqa_sin = _rope_tables(seq, n_ctx, HD, 0)
    mla_cos, mla_sin = _rope_tables(seq, n_ctx, MLA_ROPE, MLA_NOPE)
    ones_blk = jnp.asarray(np.kron(np.eye(N_HEADS), np.ones((HD, HD))), F32)
    tq = _pick_tile(seq, (1024, 512, 256))
    tk = _pick_tile(n, (1280, 640, 256))

    x_all = jnp.concatenate([ctx, x], axis=1)
    cc = jnp.concatenate([c_ctx[None], c], axis=0)
    for l in range(depth):
        w = w_in[l]
        o = 0
        aq, ak, av, ag = (w[:, o + i * BR:o + (i + 1) * BR] for i in range(4))
        o += 4 * BR
        bu, bg = w[:, o:o + U_COLS], w[:, o + U_COLS:o + U_COLS + BR]
        o += U_COLS + BR
        ccq, cckv = w[:, o:o + MLA_RANK], w[:, o + MLA_RANK:o + 2 * MLA_RANK]
        ckr = w[:, o + 2 * MLA_RANK:o + 2 * MLA_RANK + MLA_ROPE]
        cg = w[:, o + 2 * MLA_RANK + MLA_ROPE:o + 2 * MLA_RANK + MLA_ROPE + BR]
        o += 2 * MLA_RANK + MLA_ROPE + BR
        dq = w[:, o:o + BR]
        dk = w[:, o + BR:o + BR + GQA_KV_HEADS * HD]
        dv = w[:, o + BR + GQA_KV_HEADS * HD:o + BR + 2 * GQA_KV_HEADS * HD]
        dg = w[:, o + BR + 2 * GQA_KV_HEADS * HD:]
        ckr_t = jnp.tile(jnp.pad(ckr, ((0, 0), (MLA_NOPE, LANES - MLA_NOPE - MLA_ROPE))), (1, N_HEADS))
        w_a = jnp.concatenate([ag, _pad_heads(aq, N_HEADS, HD), _pad_heads(ak, N_HEADS, HD), av], 1).astype(BF16)
        w_b = jnp.concatenate([bg, bu], 1).astype(BF16)
        w_c = jnp.concatenate([cg, ccq, cckv, ckr_t], 1).astype(BF16)
        w_d = jnp.concatenate([dg, _pad_heads(dq, N_HEADS, HD), _pad_heads(dk, GQA_KV_HEADS, HD), dv], 1).astype(BF16)
        wuq = _pad_heads(mla_wuq[l], N_HEADS, MLA_NOPE + MLA_ROPE).astype(BF16)
        wukv = mla_wukv[l].reshape(MLA_RANK, N_HEADS, MLA_NOPE + HD)
        wuk = _pad_heads(wukv[:, :, :MLA_NOPE].reshape(MLA_RANK, -1), N_HEADS, MLA_NOPE).astype(BF16)
        wuv = wukv[:, :, MLA_NOPE:].reshape(MLA_RANK, -1).astype(BF16)

        mod = _modulation(cc, mod_w[l], mod_b[l])
        sh, sc, gt = jnp.split(mod, 3, axis=-1)
        pair = lambda m: jnp.stack([jnp.broadcast_to(m[0], (bsz, d)), m[1:]], axis=1)[:, :, None, :]
        sh2, sc2, gt2 = pair(sh), pair(sc), pair(gt)

        pa, h = _norm_proj(x_all, norm_g[l], sc2, sh2, w_a, n_ctx_tiles, True)
        pb, = _norm_proj(x_all, norm_g[l], sc2, sh2, w_b, n_ctx_tiles, False)
        pc, = _norm_proj(x_all, norm_g[l], sc2, sh2, w_c, n_ctx_tiles, False)
        pd, = _norm_proj(x_all, norm_g[l], sc2, sh2, w_d, n_ctx_tiles, False)

        qa, ka, va = _na_prep(pa, _pad_vec(na_qg[l]), _pad_vec(na_kg[l]))
        ya_lat = _neighbourhood(qa, ka, va, n_ctx, _na_bias_table(na_rpb[l]))
        ya_ctx = _flash(qa[:, :, :n_ctx], ka, _values_t(va[:, :, :n_ctx]), n_ctx, n_ctx, n_ctx)
        ya = jnp.concatenate([_headsT_to_cols(ya_ctx), _heads_to_cols(ya_lat)], 1)

        row = lambda t: t.reshape(1, -1)
        r, v, nkk, bonus, lw, kd, bd = _rw_prep(
            pb, row(rw_mu[l]), row(rw_kk[l]), row(rw_ka[l]), row(rw_rk[l]),
            rw_w0[l][:, None, :], rw_w2[l].astype(BF16), rw_a0[l][:, None, :], rw_a2[l].astype(BF16),
            ones_blk, n_ctx_tiles)
        ybf = _rw_scan(r, v, nkk, lw[0], kd[0], bd[0], n_ctx, False)
        ybr = _rw_scan(r, v, nkk, lw[1], kd[1], bd[1], n_ctx, True)

        qc, kc, vc = _mla_prep(pc, row(mla_qa_g[l]), row(mla_kva_g[l]), wuq, wuk, wuv,
                               _pad_vec(mla_qg[l]), _pad_vec(mla_kg[l]), mla_cos, mla_sin)
        vct = _values_t(vc)
        yc_lat = _flash(qc[:, :, n_ctx:], kc, vct, n, tq, tk)
        yc_ctx = _flash(qc[:, :, :n_ctx], kc, vct, n_ctx, n_ctx, n_ctx)
        yc = _headsT_to_cols(jnp.concatenate([yc_ctx, yc_lat], 3))

        qd, kd_, vd = _gqa_prep(pd, _pad_vec(gqa_qg[l]), _pad_vec(gqa_kg[l]), gqa_cos, gqa_sin)
        vdt = _values_t(vd)
        yd_lat = _flash(qd[:, :, n_ctx:], kd_, vdt, n, tq, tk)
        yd_ctx = _flash(qd[:, :, :n_ctx], kd_, vdt, n_ctx, n_ctx, n_ctx)
        yd = _headsT_to_cols(jnp.concatenate([yd_ctx, yd_lat], 3))

        x_all = _merge(x_all, h, ya, ybf, ybr, yc, yd, pa, pb, pc, pd, bonus,
                       row(rw_gn_w[l]), row(rw_gn_b[l]), ones_blk,
                       mg_w[l].astype(BF16), mg_b[l][:, None, :], w_br[l].astype(BF16), w_out[l].astype(BF16),
                       gt2, n_ctx_tiles)
    return x_all[:, n_ctx:]
```

```python
import functools

import numpy as np
import jax
import jax.numpy as jnp
from jax import lax
from jax.experimental import pallas as pl
from jax.experimental.pallas import tpu as pltpu

F32 = jnp.float32
BF16 = jnp.bfloat16
HIGHEST = lax.Precision.HIGHEST

D_MODEL = 1024
GRID_W = 64
BR = 512
HD = 64
N_HEADS = BR // HD
EPS = 1e-6
ROPE_BASE = 10000.0
NA_WIN_R = 8
NA_WIN_C = 16
RW_LORA = 64
RW_GN_EPS = 64e-5
MLA_RANK = 256
MLA_NOPE = 64
MLA_ROPE = 32
GQA_KV_HEADS = 2
U_COLS = 3 * BR + 4 * RW_LORA

LANES = 128
BF16_SUBLANES = 16
V_ROWS = HD + BF16_SUBLANES
VMEM_LIMIT = 56 * 1024 * 1024

TM = 256
CHUNK = 64
RW_TT = 128
RW_PACK = 4
FLASH_QB = 256
FLASH_KB = 256
LOG2E = 1.4426950408889634
NA_ROWS = 32
NA_QROWS = 4
NA_KROWS = 12


def _params(*sem):
    return pltpu.CompilerParams(dimension_semantics=sem, vmem_limit_bytes=VMEM_LIMIT)


def _dot(a, b):
    return jnp.dot(a.astype(BF16), b.astype(BF16), preferred_element_type=F32)


def _dot_nt(a, b):
    return lax.dot_general(a.astype(BF16), b.astype(BF16), (((1,), (1,)), ((), ())),
                           preferred_element_type=F32)


def _split3(t):
    hi = t.astype(BF16)
    r1 = t - hi.astype(F32)
    mid = r1.astype(BF16)
    return hi, mid, (r1 - mid.astype(F32)).astype(BF16)


def _dot_exact_rhs(a, b):
    hi, mid, lo = _split3(a)
    return (jnp.dot(hi, b, preferred_element_type=F32) + jnp.dot(mid, b, preferred_element_type=F32)
            + jnp.dot(lo, b, preferred_element_type=F32))


def _silu(t):
    return t / (1.0 + jnp.exp(-t))


def _sigmoid(t):
    return 1.0 / (1.0 + jnp.exp(-t))


def _mod_kernel(c_ref, w_ref, b_ref, o_ref):
    o_ref[...] = _dot(_silu(c_ref[...]), w_ref[...]) + b_ref[...]


def _modulation(cc, w, b):
    n = cc.shape[0]
    return pl.pallas_call(
        _mod_kernel,
        out_shape=jax.ShapeDtypeStruct((n, w.shape[1]), F32),
        compiler_params=_params(),
        name="adaln_mod",
    )(cc, w, b.reshape(1, -1))


def _norm_proj_kernel(x_ref, g_ref, sc_ref, sh_ref, w_ref, o_ref, *h_ref):
    x = x_ref[0]
    xn = x * lax.rsqrt(jnp.mean(x * x, -1, keepdims=True) + EPS) * g_ref[...]
    h = (xn * (1.0 + sc_ref[0, 0]) + sh_ref[0, 0]).astype(BF16)
    o_ref[0] = jnp.dot(h, w_ref[...], preferred_element_type=F32)
    if h_ref:
        h_ref[0][0] = h


def _norm_proj(x_all, g, sc, sh, w, n_lat_tiles, want_h):
    b, n, d = x_all.shape
    ncol = w.shape[1]
    kind = lambda i: jnp.where(i < n_lat_tiles, 1, 0)
    out_shape = [jax.ShapeDtypeStruct((b, n, ncol), F32)]
    out_specs = [pl.BlockSpec((1, TM, ncol), lambda bi, i: (bi, i, 0))]
    if want_h:
        out_shape.append(jax.ShapeDtypeStruct((b, n, d), BF16))
        out_specs.append(pl.BlockSpec((1, TM, d), lambda bi, i: (bi, i, 0)))
    return pl.pallas_call(
        _norm_proj_kernel,
        grid=(b, n // TM),
        in_specs=[
            pl.BlockSpec((1, TM, d), lambda bi, i: (bi, i, 0)),
            pl.BlockSpec((1, d), lambda bi, i: (0, 0)),
            pl.BlockSpec((1, 1, 1, d), lambda bi, i: (bi, kind(i), 0, 0)),
            pl.BlockSpec((1, 1, 1, d), lambda bi, i: (bi, kind(i), 0, 0)),
            pl.BlockSpec((d, ncol), lambda bi, i: (0, 0)),
        ],
        out_specs=out_specs,
        out_shape=out_shape,
        compiler_params=_params("parallel", "parallel"),
        name="norm_proj",
    )(x_all, g.reshape(1, d), sc, sh, w)


def _head_norm(xh, gain, inv_d):
    ms = jnp.sum(xh * xh, -1, keepdims=True) * inv_d
    return xh * lax.rsqrt(ms + EPS) * gain


def _rope(xh, cos, sin, lo, half):
    lane = lax.broadcasted_iota(jnp.int32, xh.shape, 1)
    swapped = jnp.where(lane < lo + half, pltpu.roll(xh, LANES - half, 1), pltpu.roll(xh, half, 1))
    return xh * cos + swapped * sin


def _store_values_t(vt_ref, vt, n_heads):
    lead = lax.broadcasted_iota(jnp.int32, (V_ROWS - HD, vt.shape[1]), 0) == 0
    tail = jnp.where(lead, 1.0, 0.0).astype(BF16)
    for h in range(n_heads):
        vt_ref[0, h, :HD] = vt[h * HD:(h + 1) * HD].astype(BF16)
        vt_ref[0, h, HD:] = tail


def _na_prep_kernel(p_ref, qg_ref, kg_ref, q_ref, k_ref, vt_ref):
    qoff, koff, voff = BR, BR + N_HEADS * LANES, BR + 2 * N_HEADS * LANES
    for h in range(N_HEADS):
        qh = p_ref[0, :, qoff + h * LANES:qoff + (h + 1) * LANES]
        kh = p_ref[0, :, koff + h * LANES:koff + (h + 1) * LANES]
        q_ref[0, h] = (_head_norm(qh, qg_ref[...], 1.0 / HD) * (HD ** -0.5 * LOG2E)).astype(BF16)
        k_ref[0, h] = _head_norm(kh, kg_ref[...], 1.0 / HD).astype(BF16)
    _store_values_t(vt_ref, p_ref[0, :, voff:voff + BR].T, N_HEADS)


def _na_prep(pa, qg, kg):
    b, n, ncol = pa.shape
    hm = lambda bi, i: (bi, 0, i, 0)
    vm = lambda bi, i: (bi, 0, 0, i)
    return pl.pallas_call(
        _na_prep_kernel,
        grid=(b, n // TM),
        in_specs=[pl.BlockSpec((1, TM, ncol), lambda bi, i: (bi, i, 0)),
                  pl.BlockSpec((1, LANES), lambda bi, i: (0, 0)),
                  pl.BlockSpec((1, LANES), lambda bi, i: (0, 0))],
        out_specs=[pl.BlockSpec((1, N_HEADS, TM, LANES), hm),
                   pl.BlockSpec((1, N_HEADS, TM, LANES), hm),
                   pl.BlockSpec((1, N_HEADS, V_ROWS, TM), vm)],
        out_shape=[jax.ShapeDtypeStruct((b, N_HEADS, n, LANES), BF16),
                   jax.ShapeDtypeStruct((b, N_HEADS, n, LANES), BF16),
                   jax.ShapeDtypeStruct((b, N_HEADS, V_ROWS, n), BF16)],
        compiler_params=_params("parallel", "parallel"),
        name="na_prep",
    )(pa, qg, kg)


def _gqa_prep_kernel(p_ref, qg_ref, kg_ref, cos_ref, sin_ref, q_ref, k_ref, vt_ref):
    qoff, koff = BR, BR + N_HEADS * LANES
    voff = koff + GQA_KV_HEADS * LANES
    cos, sin = cos_ref[...], sin_ref[...]
    for h in range(N_HEADS):
        qh = _head_norm(p_ref[0, :, qoff + h * LANES:qoff + (h + 1) * LANES], qg_ref[...], 1.0 / HD)
        q_ref[0, h] = (_rope(qh, cos, sin, 0, HD // 2) * (HD ** -0.5 * LOG2E)).astype(BF16)
    for h in range(GQA_KV_HEADS):
        kh = _head_norm(p_ref[0, :, koff + h * LANES:koff + (h + 1) * LANES], kg_ref[...], 1.0 / HD)
        k_ref[0, h] = _rope(kh, cos, sin, 0, HD // 2).astype(BF16)
    _store_values_t(vt_ref, p_ref[0, :, voff:voff + GQA_KV_HEADS * HD].T, GQA_KV_HEADS)


def _gqa_prep(pd, qg, kg, cos, sin):
    b, n, ncol = pd.shape
    hm = lambda bi, i: (bi, 0, i, 0)
    vm = lambda bi, i: (bi, 0, 0, i)
    vec = pl.BlockSpec((1, LANES), lambda bi, i: (0, 0))
    tab = pl.BlockSpec((TM, LANES), lambda bi, i: (i, 0))
    return pl.pallas_call(
        _gqa_prep_kernel,
        grid=(b, n // TM),
        in_specs=[pl.BlockSpec((1, TM, ncol), lambda bi, i: (bi, i, 0)), vec, vec, tab, tab],
        out_specs=[pl.BlockSpec((1, N_HEADS, TM, LANES), hm),
                   pl.BlockSpec((1, GQA_KV_HEADS, TM, LANES), hm),
                   pl.BlockSpec((1, GQA_KV_HEADS, V_ROWS, TM), vm)],
        out_shape=[jax.ShapeDtypeStruct((b, N_HEADS, n, LANES), BF16),
                   jax.ShapeDtypeStruct((b, GQA_KV_HEADS, n, LANES), BF16),
                   jax.ShapeDtypeStruct((b, GQA_KV_HEADS, V_ROWS, n), BF16)],
        compiler_params=_params("parallel", "parallel"),
        name="gqa_prep",
    )(pd, qg, kg, cos, sin)


def _mla_prep_kernel(p_ref, qa_ref, kva_ref, wuq_ref, wuk_ref, wuv_ref, qg_ref, kg_ref,
                     cos_ref, sin_ref, q_ref, k_ref, vt_ref):
    d_qk = MLA_NOPE + MLA_ROPE
    cq = p_ref[0, :, BR:BR + MLA_RANK]
    ckv = p_ref[0, :, BR + MLA_RANK:BR + 2 * MLA_RANK]
    kr = p_ref[0, :, BR + 2 * MLA_RANK:BR + 2 * MLA_RANK + N_HEADS * LANES]
    cqn = cq * lax.rsqrt(jnp.mean(cq * cq, -1, keepdims=True) + EPS) * qa_ref[...]
    ckvn = ckv * lax.rsqrt(jnp.mean(ckv * ckv, -1, keepdims=True) + EPS) * kva_ref[...]
    qf = _dot(cqn, wuq_ref[...])
    kf = _dot(ckvn, wuk_ref[...]) + kr
    vtf = _dot_nt(wuv_ref[...], ckvn)
    cos, sin = cos_ref[...], sin_ref[...]
    for h in range(N_HEADS):
        qh = _head_norm(qf[:, h * LANES:(h + 1) * LANES], qg_ref[...], 1.0 / d_qk)
        kh = _head_norm(kf[:, h * LANES:(h + 1) * LANES], kg_ref[...], 1.0 / d_qk)
        q_ref[0, h] = (_rope(qh, cos, sin, MLA_NOPE, MLA_ROPE // 2) * (d_qk ** -0.5 * LOG2E)).astype(BF16)
        k_ref[0, h] = _rope(kh, cos, sin, MLA_NOPE, MLA_ROPE // 2).astype(BF16)
    _store_values_t(vt_ref, vtf, N_HEADS)


def _mla_prep(pc, qa_g, kva_g, wuq, wuk, wuv, qg, kg, cos, sin):
    b, n, ncol = pc.shape
    hm = lambda bi, i: (bi, 0, i, 0)
    vm = lambda bi, i: (bi, 0, 0, i)
    full = lambda a: pl.BlockSpec(a.shape, lambda bi, i: (0,) * a.ndim)
    tab = pl.BlockSpec((TM, LANES), lambda bi, i: (i, 0))
    return pl.pallas_call(
        _mla_prep_kernel,
        grid=(b, n // TM),
        in_specs=[pl.BlockSpec((1, TM, ncol), lambda bi, i: (bi, i, 0)),
                  full(qa_g), full(kva_g), full(wuq), full(wuk), full(wuv), full(qg), full(kg), tab, tab],
        out_specs=[pl.BlockSpec((1, N_HEADS, TM, LANES), hm),
                   pl.BlockSpec((1, N_HEADS, TM, LANES), hm),
                   pl.BlockSpec((1, N_HEADS, V_ROWS, TM), vm)],
        out_shape=[jax.ShapeDtypeStruct((b, N_HEADS, n, LANES), BF16),
                   jax.ShapeDtypeStruct((b, N_HEADS, n, LANES), BF16),
                   jax.ShapeDtypeStruct((b, N_HEADS, V_ROWS, n), BF16)],
        compiler_params=_params("parallel", "parallel"),
        name="mla_prep",
    )(pc, qa_g, kva_g, wuq, wuk, wuv, qg, kg, cos, sin)


def _flash_kernel(q_ref, k_ref, vt_ref, o_ref, m_sc, acc_sc, s_sc, *, tk, nk, nq):
    m_sc[...] = jnp.full(m_sc.shape, -jnp.inf, F32)
    acc_sc[...] = jnp.zeros(acc_sc.shape, F32)

    n_sub = tk // FLASH_KB
    chains = range(nq)

    def scores(row0):
        k = k_ref[0, 0, pl.ds(row0, FLASH_KB), :]
        return [lax.dot_general(k, q_ref[0, 0, c * FLASH_QB:(c + 1) * FLASH_QB, :],
                                (((1,), (1,)), ((), ())), preferred_element_type=F32) for c in chains]

    for c, s0 in zip(chains, scores(0)):
        s_sc[c] = s0

    def body(j, carry):
        off = pl.multiple_of(j * tk, tk)
        nxt = pl.multiple_of(jnp.minimum(j + 1, nk - 1) * tk, tk)
        m = [m_sc[c] for c in chains]
        acc = [acc_sc[c] for c in chains]
        s = [s_sc[c] for c in chains]
        for u in range(n_sub):
            s_next = scores(off + (u + 1) * FLASH_KB if u + 1 < n_sub else nxt)
            vt = vt_ref[0, 0, :, pl.ds(off + u * FLASH_KB, FLASH_KB)]
            for c in chains:
                m_new = jnp.maximum(m[c], jnp.max(s[c], 0, keepdims=True))
                alpha = jnp.exp2(m[c] - m_new)
                p = jnp.exp2(s[c] - m_new)
                acc[c] = alpha * acc[c] + jnp.dot(vt, p.astype(BF16), preferred_element_type=F32)
                m[c] = m_new
            s = s_next
        for c in chains:
            m_sc[c], acc_sc[c], s_sc[c] = m[c], acc[c], s[c]
        return carry

    lax.fori_loop(0, nk, body, 0)
    for c in range(nq):
        o_ref[0, 0, :, c * FLASH_QB:(c + 1) * FLASH_QB] = acc_sc[c, :HD] / acc_sc[c, HD:HD + 1]


def _flash(q, k, vt, q0, n_q, k0, n_keys, tq, tk):
    b, hq = q.shape[:2]
    hk = k.shape[1]
    rep = hq // hk
    assert n_q % tq == 0 and n_keys % tk == 0 and tq % FLASH_QB == 0 and tk % FLASH_KB == 0
    assert q0 % tq == 0 and k0 % n_keys == 0
    hv = vt.shape[2]
    nq = tq // FLASH_QB
    qb0, kb0 = q0 // tq, k0 // n_keys
    return pl.pallas_call(
        functools.partial(_flash_kernel, tk=tk, nk=n_keys // tk, nq=nq),
        grid=(b, hq, n_q // tq),
        in_specs=[pl.BlockSpec((1, 1, tq, LANES), lambda bi, h, i: (bi, h, qb0 + i, 0)),
                  pl.BlockSpec((1, 1, n_keys, LANES), lambda bi, h, i: (bi, h // rep, kb0, 0)),
                  pl.BlockSpec((1, 1, hv, n_keys), lambda bi, h, i: (bi, h // rep, 0, kb0))],
        out_specs=pl.BlockSpec((1, 1, HD, tq), lambda bi, h, i: (bi, h, 0, i)),
        out_shape=jax.ShapeDtypeStruct((b, hq, HD, n_q), F32),
        scratch_shapes=[pltpu.VMEM((nq, 1, FLASH_QB), F32), pltpu.VMEM((nq, hv, FLASH_QB), F32),
                        pltpu.VMEM((nq, FLASH_KB, FLASH_QB), F32)],
        compiler_params=_params("parallel", "parallel", "arbitrary"),
        name="flash_attention",
    )(q, k, vt)


def _na_kernel(q_ref, k_ref, vt_ref, kc_ref, vtc_ref, bias_ref, o_ref, *, rows):
    kc = kc_ref[0, 0]
    vtc = vtc_ref[0, 0]
    nq, nk = NA_QROWS * GRID_W, NA_KROWS * GRID_W
    batches = range(NA_ROWS // NA_QROWS)
    r0 = [pl.program_id(2) * NA_ROWS + bi * NA_QROWS for bi in batches]
    koff = [pl.multiple_of(jnp.clip(r - NA_WIN_R // 2, 0, rows - NA_KROWS) * GRID_W, 2 * GRID_W) for r in r0]
    variant = [jnp.where(r == 0, 0, jnp.where(r == rows - NA_QROWS, 2, 1)) for r in r0]
    q = [q_ref[0, 0, bi * nq:(bi + 1) * nq, :] for bi in batches]
    nt = (((1,), (1,)), ((), ()))
    s_w = [lax.dot_general(k_ref[0, 0, pl.ds(o, nk), :], x, nt, preferred_element_type=F32) + bias_ref[v, 0]
           for o, x, v in zip(koff, q, variant)]
    s_c = [lax.dot_general(kc, x, nt, preferred_element_type=F32) for x in q]
    m = [jnp.maximum(jnp.max(x, 0, keepdims=True), jnp.max(y, 0, keepdims=True)) for x, y in zip(s_w, s_c)]
    p_w = [jnp.exp2(x - y).astype(BF16) for x, y in zip(s_w, m)]
    p_c = [jnp.exp2(x - y).astype(BF16) for x, y in zip(s_c, m)]
    acc = [jnp.dot(vt_ref[0, 0, :, pl.ds(o, nk)], x, preferred_element_type=F32)
           + jnp.dot(vtc, y, preferred_element_type=F32) for o, x, y in zip(koff, p_w, p_c)]
    for bi in batches:
        o_ref[0, 0, :, bi * nq:(bi + 1) * nq] = acc[bi][:HD] / acc[bi][HD:HD + 1]


def _na_bias_table(rpb):
    n_dr, n_dc = 2 * NA_WIN_R - 1, 2 * NA_WIN_C - 1
    sel_r = np.zeros((3, NA_KROWS, NA_QROWS, n_dr), np.float32)
    ok_r = np.zeros((3, NA_KROWS, NA_QROWS), bool)
    for kind in range(3):
        for i in range(NA_QROWS):
            start = (0, i, NA_KROWS - NA_WIN_R)[kind]
            shift = (0, -(NA_WIN_R // 2), NA_QROWS - NA_KROWS)[kind]
            for j in range(start, start + NA_WIN_R):
                sel_r[kind, j, i, j + shift - i + NA_WIN_R - 1] = 1.0
                ok_r[kind, j, i] = True
    cols = np.arange(GRID_W)
    cs = np.clip(cols - NA_WIN_C // 2, 0, GRID_W - NA_WIN_C)
    sel_c = np.zeros((GRID_W, GRID_W, n_dc), np.float32)
    ok_c = np.zeros((GRID_W, GRID_W), bool)
    for qc in range(GRID_W):
        for kcol in range(cs[qc], cs[qc] + NA_WIN_C):
            sel_c[kcol, qc, kcol - qc + NA_WIN_C - 1] = 1.0
            ok_c[kcol, qc] = True
    tab = jnp.einsum('hrc,vjir,kqc->vhjkiq', rpb, sel_r, sel_c, precision=HIGHEST)
    ok = ok_r[:, None, :, None, :, None] & ok_c[None, None, None, :, None, :]
    tab = jnp.where(ok, tab * LOG2E, -1e30)
    return tab.reshape(3, rpb.shape[0], NA_KROWS * GRID_W, NA_QROWS * GRID_W)


def _neighbourhood(q, k, vt, seq, n_ctx, bias):
    b, h = q.shape[:2]
    rows = seq // GRID_W
    assert rows >= NA_KROWS and rows % NA_ROWS == 0 and seq % n_ctx == 0
    tile = NA_ROWS * GRID_W
    hv = vt.shape[2]
    cb = seq // n_ctx
    return pl.pallas_call(
        functools.partial(_na_kernel, rows=rows),
        grid=(b, h, rows // NA_ROWS),
        in_specs=[pl.BlockSpec((1, 1, tile, LANES), lambda bi, hi, i: (bi, hi, i, 0)),
                  pl.BlockSpec((1, 1, seq, LANES), lambda bi, hi, i: (bi, hi, 0, 0)),
                  pl.BlockSpec((1, 1, hv, seq), lambda bi, hi, i: (bi, hi, 0, 0)),
                  pl.BlockSpec((1, 1, n_ctx, LANES), lambda bi, hi, i: (bi, hi, cb, 0)),
                  pl.BlockSpec((1, 1, hv, n_ctx), lambda bi, hi, i: (bi, hi, 0, cb)),
                  pl.BlockSpec((3, 1, NA_KROWS * GRID_W, NA_QROWS * GRID_W), lambda bi, hi, i: (0, hi, 0, 0))],
        out_specs=pl.BlockSpec((1, 1, HD, tile), lambda bi, hi, i: (bi, hi, 0, i)),
        out_shape=jax.ShapeDtypeStruct((b, h, HD, seq), F32),
        compiler_params=_params("parallel", "parallel", "arbitrary"),
        name="neighbourhood_attention",
    )(q, k, vt, k, vt, bias)


def _seg_sum(t, ones_blk):
    return _dot_exact_rhs(t, ones_blk)


def _rw_prep_kernel(p_ref, prev_ref, next_ref, mu_ref, kk_ref, ka_ref, rk_ref, w0_ref, w2_ref, a0_ref,
                    a2_ref, ones_ref, r_o, v_o, nkk_o, bonus_o, lw_o, kd_o, bd_o, *, n_lat_tiles, n_tiles):
    i = pl.program_id(1)
    u = p_ref[0, :, BR:]
    first = jnp.logical_or(i == 0, i == n_lat_tiles)
    last = jnp.logical_or(i == n_lat_tiles - 1, i == n_tiles - 1)
    prev_row = jnp.where(first, 0.0, prev_ref[0, 7:8, BR:])
    next_row = jnp.where(last, 0.0, next_ref[0, 0:1, BR:])
    row = lax.broadcasted_iota(jnp.int32, u.shape, 0)
    up = jnp.where(row == 0, prev_row, pltpu.roll(u, 1, 0))
    dn = jnp.where(row == TM - 1, next_row, pltpu.roll(u, TM - 1, 0))
    u = u + mu_ref[...] * (0.5 * (up + dn) - u)
    r, k, v = u[:, :BR], u[:, BR:2 * BR], u[:, 2 * BR:3 * BR]
    wl, al = u[:, 3 * BR:3 * BR + 2 * RW_LORA], u[:, 3 * BR + 2 * RW_LORA:]
    ones_blk = ones_ref[...]
    kk = k * kk_ref[...]
    kk = kk * lax.rsqrt(_seg_sum(kk * kk, ones_blk) + 1e-12)
    r_o[0] = r
    v_o[0] = v
    nkk_o[0] = -kk
    k_sum = jnp.zeros_like(k)
    for d in range(2):
        wl_d = wl[:, d * RW_LORA:(d + 1) * RW_LORA]
        al_d = al[:, d * RW_LORA:(d + 1) * RW_LORA]
        z = -(w0_ref[d] + _dot(jnp.tanh(wl_d), w2_ref[d]))
        softplus = jnp.maximum(z, 0.0) + jnp.log(1.0 + jnp.exp(-jnp.abs(z)))
        lw_o[d, 0] = -jnp.exp(-softplus - 0.5)
        a = _sigmoid(a0_ref[d] + _dot(al_d, a2_ref[d]))
        k_d = k * (1.0 + (a - 1.0) * ka_ref[...])
        kd_o[d, 0] = k_d
        bd_o[d, 0] = kk * a
        k_sum = k_sum + k_d
    bonus_o[0] = _seg_sum(r * k_sum * rk_ref[...], ones_blk) * v


def _rw_prep(pb, mu, kk, ka, rk, w0, w2, a0, a2, ones_blk, n_lat_tiles):
    b, n, ncol = pb.shape
    nt = n // TM
    r8 = TM // 8
    full = lambda a: pl.BlockSpec(a.shape, lambda bi, i: (0,) * a.ndim)
    tok = pl.BlockSpec((1, TM, BR), lambda bi, i: (bi, i, 0))
    tok2 = pl.BlockSpec((2, 1, TM, BR), lambda bi, i: (0, bi, i, 0))
    one = jax.ShapeDtypeStruct((b, n, BR), F32)
    two = jax.ShapeDtypeStruct((2, b, n, BR), F32)
    return pl.pallas_call(
        functools.partial(_rw_prep_kernel, n_lat_tiles=n_lat_tiles, n_tiles=nt),
        grid=(b, nt),
        in_specs=[pl.BlockSpec((1, TM, ncol), lambda bi, i: (bi, i, 0)),
                  pl.BlockSpec((1, 8, ncol), lambda bi, i: (bi, jnp.maximum(i * r8 - 1, 0), 0)),
                  pl.BlockSpec((1, 8, ncol), lambda bi, i: (bi, jnp.minimum((i + 1) * r8, n // 8 - 1), 0)),
                  full(mu), full(kk), full(ka), full(rk), full(w0), full(w2), full(a0), full(a2),
                  full(ones_blk)],
        out_specs=[tok, tok, tok, tok, tok2, tok2, tok2],
        out_shape=[one, one, one, one, two, two, two],
        compiler_params=_params("parallel", "parallel"),
        name="rwkv_prep",
    )(pb, pb, pb, mu, kk, ka, rk, w0, w2, a0, a2, ones_blk)


def _rw_scan_kernel(r_ref, v_ref, a_ref, lw_ref, k_ref, b_ref, y_ref, st_ref, *, rev, n_b):
    c, w = CHUNK, RW_PACK * HD

    @pl.when(pl.program_id(0) == 0)
    def _():
        st_ref[...] = jnp.zeros(st_ref.shape, F32)

    tt = lax.broadcasted_iota(jnp.int32, (c, w), 0)
    ss = lax.broadcasted_iota(jnp.int32, (c, w), 1) & (c - 1)
    strict = (ss > tt) if rev else (ss < tt)
    incl = (ss >= tt) if rev else (ss <= tt)
    eye = (ss == tt).astype(F32)
    t2 = lax.broadcasted_iota(jnp.int32, (c, c), 0)
    s2 = lax.broadcasted_iota(jnp.int32, (c, c), 1)
    m_incl = ((s2 >= t2) if rev else (s2 <= t2)).astype(BF16)
    blk = (lax.broadcasted_iota(jnp.int32, (w, w), 0) >> 6) == (lax.broadcasted_iota(jnp.int32, (w, w), 1) >> 6)

    def bd(t):
        tb = t.astype(BF16)
        return jnp.where(blk, jnp.concatenate([tb] * RW_PACK, axis=0), jnp.zeros((), BF16))

    def dtn(x, y):
        return lax.dot_general(x.astype(BF16), y.astype(BF16), (((0,), (0,)), ((), ())),
                               preferred_element_type=F32)

    chunks = tuple(range(RW_TT // c))
    chunks = chunks[::-1] if rev else chunks
    chains = [(bi, g) for bi in range(n_b) for g in range(BR // w)]
    units = [(ci, bi, g) for ci in chunks for (bi, g) in chains]
    sl = lambda ref, u: ref[u[1], u[0] * c:(u[0] + 1) * c, u[2] * w:(u[2] + 1) * w]

    lw = [sl(lw_ref, u) for u in units]
    cum = []
    for t in lw:
        hi, mid, lo = _split3(t)
        cum.append(jnp.dot(m_incl, hi, preferred_element_type=F32)
                   + jnp.dot(m_incl, mid, preferred_element_type=F32)
                   + jnp.dot(m_incl, lo, preferred_element_type=F32))
    tot = [jnp.sum(t, 0, keepdims=True) for t in lw]
    p_inv = [jnp.exp(-x) for x in cum]
    p_end = [jnp.exp(t - x) for t, x in zip(tot, cum)]
    a_t = [sl(a_ref, u) * jnp.exp(x - l) for u, x, l in zip(units, cum, lw)]
    r_t = [sl(r_ref, u) * jnp.exp(x) for u, x in zip(units, cum)]
    b_raw = [sl(b_ref, u) for u in units]
    k_raw = [sl(k_ref, u) for u in units]
    v = [sl(v_ref, u) for u in units]
    b_t = [x * p for x, p in zip(b_raw, p_inv)]
    k_t = [x * p for x, p in zip(k_raw, p_inv)]
    b_h = [x * p for x, p in zip(b_raw, p_end)]
    k_h = [x * p for x, p in zip(k_raw, p_end)]
    ar = [jnp.concatenate([x, y], axis=0) for x, y in zip(a_t, r_t)]
    g_b = [_dot_nt(x, bd(y)) for x, y in zip(ar, b_t)]
    g_k = [_dot_nt(x, bd(y)) for x, y in zip(ar, k_t)]
    lab = [jnp.where(strict, x[:c], 0.0) for x in g_b]
    lak = [jnp.where(strict, x[:c], 0.0) for x in g_k]
    qrb = [jnp.where(incl, x[c:], 0.0) for x in g_b]
    qrk = [jnp.where(incl, x[c:], 0.0) for x in g_k]
    tinv = [eye + x for x in lab]
    lp = lab
    for i in range(6):
        lp_bd = [bd(x) for x in lp]
        if i > 0:
            tinv = [x + _dot(x, y) for x, y in zip(tinv, lp_bd)]
        if i < 5:
            lp = [_dot(x, y) for x, y in zip(lp, lp_bd)]
    v_bd = [bd(x) for x in v]
    a_hat = [_dot(x, bd(y)) for x, y in zip(tinv, a_t)]
    wv = [_dot(x, y) for x, y in zip(lak, v_bd)]
    u_hat = [_dot(x, bd(y)) for x, y in zip(tinv, wv)]
    y_loc = [_dot(x, y) for x, y in zip(qrk, v_bd)]
    vk = [jnp.where(blk, dtn(x, y), 0.0) for x, y in zip(v, k_h)]
    p_c = [jnp.exp(t) for t in tot]

    n_ch = len(chains)
    for j in range(len(chunks)):
        idx = range(j * n_ch, (j + 1) * n_ch)
        st = [st_ref[q] for q in range(n_ch)]
        st_b = [s.astype(BF16) for s in st]
        u = [_dot_nt(a_hat[i], s) + u_hat[i] for i, s in zip(idx, st_b)]
        y = [_dot_nt(r_t[i], s) + _dot(qrb[i], bd(x)) + y_loc[i] for i, s, x in zip(idx, st_b, u)]
        new = [s * p_c[i] + jnp.where(blk, dtn(x, b_h[i]), 0.0) + vk[i] for i, s, x in zip(idx, st, u)]
        for q, i in enumerate(idx):
            y_ref[units[i][1], units[i][0] * c:(units[i][0] + 1) * c, units[i][2] * w:(units[i][2] + 1) * w] = y[q]
            st_ref[q] = new[q]


def _rw_scan(r, v, nkk, lw, kd, bd, n_ctx, rev):
    b, n, _ = r.shape
    nblk = n // RW_TT
    nctx = n_ctx // RW_TT
    if rev:
        blk_of = lambda i: nblk - 1 - i
    else:
        blk_of = lambda i: jnp.where(i < nctx, nblk - nctx + i, i - nctx)
    spec = pl.BlockSpec((b, RW_TT, BR), lambda i: (0, blk_of(i), 0))
    n_chains = b * (BR // (RW_PACK * HD))
    return pl.pallas_call(
        functools.partial(_rw_scan_kernel, rev=rev, n_b=b),
        grid=(nblk,),
        in_specs=[spec] * 6,
        out_specs=spec,
        out_shape=jax.ShapeDtypeStruct((b, n, BR), F32),
        scratch_shapes=[pltpu.VMEM((n_chains, RW_PACK * HD, RW_PACK * HD), F32)],
        compiler_params=_params("arbitrary"),
        name="rwkv_scan",
    )(r, v, nkk, lw, kd, bd)


def _merge_kernel(x_ref, h_ref, ya_ref, yac_ref, ybf_ref, ybr_ref, yc_ref, ycc_ref, yd_ref, ydc_ref,
                  ga_ref, gb_ref, gc_ref, gd_ref, bonus_ref, gnw_ref, gnb_ref, ones_ref, mgw_ref, mgb_ref,
                  wbr_ref, wout_ref, gt_ref, o_ref, *, n_lat_tiles):
    h = h_ref[0]
    ones_blk = ones_ref[...]
    is_ctx = pl.program_id(1) >= n_lat_tiles

    def token_major(lat_ref, ctx_ref):
        yt = jnp.where(is_ctx, ctx_ref[0], lat_ref[0])
        return yt.reshape(BR, TM).T

    yb = ybf_ref[0] + ybr_ref[0]
    mean = _seg_sum(yb, ones_blk) * (1.0 / HD)
    cen = yb - mean
    var = _seg_sum(cen * cen, ones_blk) * (1.0 / HD)
    yb = cen * lax.rsqrt(var + RW_GN_EPS) * gnw_ref[...] + gnb_ref[...] + bonus_ref[0]
    ys = (token_major(ya_ref, yac_ref), yb, token_major(yc_ref, ycc_ref), token_major(yd_ref, ydc_ref))
    gs = (ga_ref[0], gb_ref[0], gc_ref[0], gd_ref[0])
    acc = None
    for i in range(4):
        gate = _sigmoid(jnp.dot(h, mgw_ref[i], preferred_element_type=F32) + mgb_ref[i])
        term = gate * _dot(ys[i] * _silu(gs[i]), wbr_ref[i])
        acc = term if acc is None else acc + term
    o_ref[0] = x_ref[0] + gt_ref[0, 0] * _dot(acc, wout_ref[...])


def _merge(x_all, h, ya, ybf, ybr, yc, yd, pa, pb, pc, pd, bonus, gnw, gnb, ones_blk, mgw, mgb, wbr, wout, gt,
           n_lat_tiles, n_out):
    b, n, d = x_all.shape
    kind = lambda i: jnp.where(i < n_lat_tiles, 1, 0)
    tok = lambda w: pl.BlockSpec((1, TM, w), lambda bi, i: (bi, i, 0))
    full = lambda a: pl.BlockSpec(a.shape, lambda bi, i: (0,) * a.ndim)
    lat = pl.BlockSpec((1, N_HEADS, HD, TM), lambda bi, i: (bi, 0, 0, jnp.minimum(i, n_lat_tiles - 1)))
    ctx = pl.BlockSpec((1, N_HEADS, HD, TM), lambda bi, i: (bi, 0, 0, 0))
    return pl.pallas_call(
        functools.partial(_merge_kernel, n_lat_tiles=n_lat_tiles),
        grid=(b, n_out // TM),
        in_specs=[tok(d), tok(d), lat, ctx, tok(BR), tok(BR), lat, ctx, lat, ctx,
                  tok(BR), tok(BR), tok(BR), tok(BR), tok(BR),
                  full(gnw), full(gnb), full(ones_blk), full(mgw), full(mgb), full(wbr), full(wout),
                  pl.BlockSpec((1, 1, 1, d), lambda bi, i: (bi, kind(i), 0, 0))],
        out_specs=tok(d),
        out_shape=jax.ShapeDtypeStruct((b, n_out, d), F32),
        compiler_params=_params("parallel", "parallel"),
        name="merge_out",
    )(x_all, h, ya[0], ya[1], ybf, ybr, yc[0], yc[1], yd[0], yd[1], pa, pb, pc, pd, bonus,
      gnw, gnb, ones_blk, mgw, mgb, wbr, wout, gt)


def _pad_heads(w, n_heads, d):
    lead = w.shape[:-1]
    w = w.reshape(lead + (n_heads, d))
    w = jnp.pad(w, [(0, 0)] * len(lead) + [(0, 0), (0, LANES - d)])
    return w.reshape(lead + (n_heads * LANES,))


def _pad_vec(g):
    return jnp.pad(g, (0, LANES - g.shape[0])).reshape(1, LANES)


def _rope_tables(n_lat, n_ctx, d_rot, lo):
    t = jnp.arange(n_lat)
    row = (t // GRID_W).astype(F32)
    col = (t % GRID_W).astype(F32)
    n_freq = d_rot // 4
    inv = ROPE_BASE ** (-jnp.arange(n_freq, dtype=F32) / n_freq)
    ang = jnp.concatenate([row[:, None] * inv, col[:, None] * inv], -1)
    cos, sin = jnp.cos(ang), jnp.sin(ang)
    half = d_rot // 2
    cos_t = jnp.ones((n_lat, LANES), F32).at[:, lo:lo + half].set(cos).at[:, lo + half:lo + d_rot].set(cos)
    sin_t = jnp.zeros((n_lat, LANES), F32).at[:, lo:lo + half].set(-sin).at[:, lo + half:lo + d_rot].set(sin)
    cos_t = jnp.concatenate([cos_t, jnp.ones((n_ctx, LANES), F32)], 0)
    sin_t = jnp.concatenate([sin_t, jnp.zeros((n_ctx, LANES), F32)], 0)
    return cos_t, sin_t


def _pick_tile(n, candidates):
    for c in candidates:
        if n % c == 0:
            return c
    raise ValueError(f"no tile for {n}")


def kernel(x, c, ctx, c_ctx, norm_g, mod_w, mod_b, w_in, na_qg, na_kg, na_rpb, rw_mu, rw_w0, rw_w2, rw_a0, rw_a2, rw_kk, rw_ka, rw_rk, rw_gn_w, rw_gn_b, mla_qa_g, mla_kva_g, mla_wuq, mla_wukv, mla_qg, mla_kg, gqa_qg, gqa_kg, mg_w, mg_b, w_br, w_out):
    bsz, seq, d = x.shape
    n_ctx = ctx.shape[1]
    depth = w_in.shape[0]
    assert d == D_MODEL and n_ctx == TM and seq % (NA_ROWS * GRID_W) == 0
    n = seq + n_ctx
    n_lat_tiles = seq // TM

    gqa_cos, gqa_sin = _rope_tables(seq, n_ctx, HD, 0)
    mla_cos, mla_sin = _rope_tables(seq, n_ctx, MLA_ROPE, MLA_NOPE)
    ones_blk = jnp.asarray(np.kron(np.eye(N_HEADS), np.ones((HD, HD))), BF16)
    tq = _pick_tile(seq, (1024, 512, 256))
    tk = _pick_tile(n, (3328, 1280, 640, 256))

    def attend(q, k, vt):
        return (_flash(q, k, vt, 0, seq, 0, n, tq, tk), _flash(q, k, vt, seq, n_ctx, seq, n_ctx, n_ctx, n_ctx))

    x_all = jnp.concatenate([x, ctx], axis=1)
    cc = jnp.concatenate([c_ctx[None], c], axis=0)
    for l in range(depth):
        w = w_in[l]
        o = 0
        aq, ak, av, ag = (w[:, o + i * BR:o + (i + 1) * BR] for i in range(4))
        o += 4 * BR
        bu, bg = w[:, o:o + U_COLS], w[:, o + U_COLS:o + U_COLS + BR]
        o += U_COLS + BR
        ccq, cckv = w[:, o:o + MLA_RANK], w[:, o + MLA_RANK:o + 2 * MLA_RANK]
        ckr = w[:, o + 2 * MLA_RANK:o + 2 * MLA_RANK + MLA_ROPE]
        cg = w[:, o + 2 * MLA_RANK + MLA_ROPE:o + 2 * MLA_RANK + MLA_ROPE + BR]
        o += 2 * MLA_RANK + MLA_ROPE + BR
        dq = w[:, o:o + BR]
        dk = w[:, o + BR:o + BR + GQA_KV_HEADS * HD]
        dv = w[:, o + BR + GQA_KV_HEADS * HD:o + BR + 2 * GQA_KV_HEADS * HD]
        dg = w[:, o + BR + 2 * GQA_KV_HEADS * HD:]
        ckr_t = jnp.tile(jnp.pad(ckr, ((0, 0), (MLA_NOPE, LANES - MLA_NOPE - MLA_ROPE))), (1, N_HEADS))
        w_a = jnp.concatenate([ag, _pad_heads(aq, N_HEADS, HD), _pad_heads(ak, N_HEADS, HD), av], 1).astype(BF16)
        w_b = jnp.concatenate([bg, bu], 1).astype(BF16)
        w_c = jnp.concatenate([cg, ccq, cckv, ckr_t], 1).astype(BF16)
        w_d = jnp.concatenate([dg, _pad_heads(dq, N_HEADS, HD), _pad_heads(dk, GQA_KV_HEADS, HD), dv], 1).astype(BF16)
        wuq = _pad_heads(mla_wuq[l], N_HEADS, MLA_NOPE + MLA_ROPE).astype(BF16)
        wukv = mla_wukv[l].reshape(MLA_RANK, N_HEADS, MLA_NOPE + HD)
        wuk = _pad_heads(wukv[:, :, :MLA_NOPE].reshape(MLA_RANK, -1), N_HEADS, MLA_NOPE).astype(BF16)
        wuv = wukv[:, :, MLA_NOPE:].reshape(MLA_RANK, -1).T.astype(BF16)

        mod = _modulation(cc, mod_w[l], mod_b[l])
        sh, sc, gt = jnp.split(mod, 3, axis=-1)
        pair = lambda m: jnp.stack([jnp.broadcast_to(m[0], (bsz, d)), m[1:]], axis=1)[:, :, None, :]
        sh2, sc2, gt2 = pair(sh), pair(sc), pair(gt)

        pa, h = _norm_proj(x_all, norm_g[l], sc2, sh2, w_a, n_lat_tiles, True)
        pb, = _norm_proj(x_all, norm_g[l], sc2, sh2, w_b, n_lat_tiles, False)
        pc, = _norm_proj(x_all, norm_g[l], sc2, sh2, w_c, n_lat_tiles, False)
        pd, = _norm_proj(x_all, norm_g[l], sc2, sh2, w_d, n_lat_tiles, False)

        qa, ka, vat = _na_prep(pa, _pad_vec(na_qg[l]), _pad_vec(na_kg[l]))
        ya = (_neighbourhood(qa, ka, vat, seq, n_ctx, _na_bias_table(na_rpb[l])),
              _flash(qa, ka, vat, seq, n_ctx, seq, n_ctx, n_ctx, n_ctx))

        row = lambda t: t.reshape(1, -1)
        r, v, nkk, bonus, lw, kd, bd = _rw_prep(
            pb, row(rw_mu[l]), row(rw_kk[l]), row(rw_ka[l]), row(rw_rk[l]),
            rw_w0[l][:, None, :], rw_w2[l].astype(BF16), rw_a0[l][:, None, :], rw_a2[l].astype(BF16),
            ones_blk, n_lat_tiles)
        ybf = _rw_scan(r, v, nkk, lw[0], kd[0], bd[0], n_ctx, False)
        ybr = _rw_scan(r, v, nkk, lw[1], kd[1], bd[1], n_ctx, True)

        qc, kc, vct = _mla_prep(pc, row(mla_qa_g[l]), row(mla_kva_g[l]), wuq, wuk, wuv,
                                _pad_vec(mla_qg[l]), _pad_vec(mla_kg[l]), mla_cos, mla_sin)
        yc = attend(qc, kc, vct)

        qd, kd_, vdt = _gqa_prep(pd, _pad_vec(gqa_qg[l]), _pad_vec(gqa_kg[l]), gqa_cos, gqa_sin)
        yd = attend(qd, kd_, vdt)

        n_out = n if l + 1 < depth else seq
        x_all = _merge(x_all, h, ya, ybf, ybr, yc, yd, pa, pb, pc, pd, bonus,
                       row(rw_gn_w[l]), row(rw_gn_b[l]), ones_blk,
                       mg_w[l].astype(BF16), mg_b[l][:, None, :], w_br[l].astype(BF16), w_out[l].astype(BF16),
                       gt2, n_lat_tiles, n_out)
    return x_all
```

```python
import functools

import numpy as np
import jax
import jax.numpy as jnp
from jax import lax
from jax.experimental import pallas as pl
from jax.experimental.pallas import tpu as pltpu

F32 = jnp.float32
BF16 = jnp.bfloat16
HIGHEST = lax.Precision.HIGHEST

D_MODEL = 1024
GRID_W = 64
BR = 512
HD = 64
N_HEADS = BR // HD
EPS = 1e-6
ROPE_BASE = 10000.0
NA_WIN_R = 8
NA_WIN_C = 16
RW_LORA = 64
RW_GN_EPS = 64e-5
MLA_RANK = 256
MLA_NOPE = 64
MLA_ROPE = 32
GQA_KV_HEADS = 2
U_COLS = 3 * BR + 4 * RW_LORA

LANES = 128
BF16_SUBLANES = 16
V_ROWS = HD + BF16_SUBLANES
VMEM_LIMIT = 56 * 1024 * 1024

TM = 256
CHUNK = 64
RW_TT = 128
RW_PACK = 4
FLASH_QB = 256
FLASH_KB = 256
LOG2E = 1.4426950408889634
NA_ROWS = 32
NA_QROWS = 4
NA_KROWS = 12


def _params(*sem):
    return pltpu.CompilerParams(dimension_semantics=sem, vmem_limit_bytes=VMEM_LIMIT)


def _dot(a, b):
    return jnp.dot(a.astype(BF16), b.astype(BF16), preferred_element_type=F32)


def _dot_nt(a, b):
    return lax.dot_general(a.astype(BF16), b.astype(BF16), (((1,), (1,)), ((), ())),
                           preferred_element_type=F32)


def _split3(t):
    hi = t.astype(BF16)
    r1 = t - hi.astype(F32)
    mid = r1.astype(BF16)
    return hi, mid, (r1 - mid.astype(F32)).astype(BF16)


def _dot_exact_rhs(a, b):
    hi, mid, lo = _split3(a)
    return (jnp.dot(hi, b, preferred_element_type=F32) + jnp.dot(mid, b, preferred_element_type=F32)
            + jnp.dot(lo, b, preferred_element_type=F32))


def _silu(t):
    return t / (1.0 + jnp.exp(-t))


def _sigmoid(t):
    return 1.0 / (1.0 + jnp.exp(-t))


def _mod_kernel(c_ref, w_ref, b_ref, o_ref):
    o_ref[...] = _dot(_silu(c_ref[...]), w_ref[...]) + b_ref[...]


def _modulation(cc, w, b):
    n = cc.shape[0]
    return pl.pallas_call(
        _mod_kernel,
        out_shape=jax.ShapeDtypeStruct((n, w.shape[1]), F32),
        compiler_params=_params(),
        name="adaln_mod",
    )(cc, w, b.reshape(1, -1))


def _norm_proj_kernel(x_ref, g_ref, sc_ref, sh_ref, w_ref, *refs, n_extra, want_h, epilogue):
    x = x_ref[0]
    xn = x * lax.rsqrt(jnp.mean(x * x, -1, keepdims=True) + EPS) * g_ref[...]
    h = (xn * (1.0 + sc_ref[0, 0]) + sh_ref[0, 0]).astype(BF16)
    p = jnp.dot(h, w_ref[...], preferred_element_type=F32)
    extra, outs = refs[:n_extra], refs[n_extra:]
    if want_h:
        outs[-1][0] = h
        outs = outs[:-1]
    epilogue(p, extra, outs)


def _store_projection(p, extra, outs):
    outs[0][0] = p


def _norm_proj(x_all, g, sc, sh, w, n_lat_tiles, want_h, epilogue=_store_projection, extra=(), extra_specs=(),
               out_specs=None, out_shape=None, name="norm_proj"):
    b, n, d = x_all.shape
    ncol = w.shape[1]
    kind = lambda i: jnp.where(i < n_lat_tiles, 1, 0)
    if out_specs is None:
        out_shape = [jax.ShapeDtypeStruct((b, n, ncol), F32)]
        out_specs = [pl.BlockSpec((1, TM, ncol), lambda bi, i: (bi, i, 0))]
    out_shape, out_specs = list(out_shape), list(out_specs)
    if want_h:
        out_shape.append(jax.ShapeDtypeStruct((b, n, d), BF16))
        out_specs.append(pl.BlockSpec((1, TM, d), lambda bi, i: (bi, i, 0)))
    return pl.pallas_call(
        functools.partial(_norm_proj_kernel, n_extra=len(extra), want_h=want_h, epilogue=epilogue),
        grid=(b, n // TM),
        in_specs=[
            pl.BlockSpec((1, TM, d), lambda bi, i: (bi, i, 0)),
            pl.BlockSpec((1, d), lambda bi, i: (0, 0)),
            pl.BlockSpec((1, 1, 1, d), lambda bi, i: (bi, kind(i), 0, 0)),
            pl.BlockSpec((1, 1, 1, d), lambda bi, i: (bi, kind(i), 0, 0)),
            pl.BlockSpec((d, ncol), lambda bi, i: (0, 0)),
        ] + list(extra_specs),
        out_specs=out_specs,
        out_shape=out_shape,
        compiler_params=_params("parallel", "parallel"),
        name=name,
    )(x_all, g.reshape(1, d), sc, sh, w, *extra)


def _head_norm(xh, gain, inv_d):
    ms = jnp.sum(xh * xh, -1, keepdims=True) * inv_d
    return xh * lax.rsqrt(ms + EPS) * gain


def _rope(xh, cos, sin, lo, half):
    lane = lax.broadcasted_iota(jnp.int32, xh.shape, 1)
    swapped = jnp.where(lane < lo + half, pltpu.roll(xh, LANES - half, 1), pltpu.roll(xh, half, 1))
    return xh * cos + swapped * sin


def _store_values_t(vt_ref, vt, n_heads):
    lead = lax.broadcasted_iota(jnp.int32, (V_ROWS - HD, vt.shape[1]), 0) == 0
    tail = jnp.where(lead, 1.0, 0.0).astype(BF16)
    for h in range(n_heads):
        vt_ref[0, h, :HD] = vt[h * HD:(h + 1) * HD].astype(BF16)
        vt_ref[0, h, HD:] = tail


def _head_specs(b, n, n_kv):
    hm = lambda bi, i: (bi, 0, i, 0)
    vm = lambda bi, i: (bi, 0, 0, i)
    specs = [pl.BlockSpec((1, TM, BR), lambda bi, i: (bi, i, 0)),
             pl.BlockSpec((1, N_HEADS, TM, LANES), hm),
             pl.BlockSpec((1, n_kv, TM, LANES), hm),
             pl.BlockSpec((1, n_kv, V_ROWS, TM), vm)]
    shapes = [jax.ShapeDtypeStruct((b, n, BR), F32),
              jax.ShapeDtypeStruct((b, N_HEADS, n, LANES), BF16),
              jax.ShapeDtypeStruct((b, n_kv, n, LANES), BF16),
              jax.ShapeDtypeStruct((b, n_kv, V_ROWS, n), BF16)]
    return specs, shapes


def _na_epilogue(p, extra, outs):
    qg_ref, kg_ref = extra
    g_ref, q_ref, k_ref, vt_ref = outs
    qoff, koff, voff = BR, BR + N_HEADS * LANES, BR + 2 * N_HEADS * LANES
    g_ref[0] = p[:, :BR]
    for h in range(N_HEADS):
        qh = p[:, qoff + h * LANES:qoff + (h + 1) * LANES]
        kh = p[:, koff + h * LANES:koff + (h + 1) * LANES]
        q_ref[0, h] = (_head_norm(qh, qg_ref[...], 1.0 / HD) * (HD ** -0.5 * LOG2E)).astype(BF16)
        k_ref[0, h] = _head_norm(kh, kg_ref[...], 1.0 / HD).astype(BF16)
    _store_values_t(vt_ref, p[:, voff:voff + BR].T, N_HEADS)


def _gqa_epilogue(p, extra, outs):
    qg_ref, kg_ref, cos_ref, sin_ref = extra
    g_ref, q_ref, k_ref, vt_ref = outs
    qoff, koff = BR, BR + N_HEADS * LANES
    voff = koff + GQA_KV_HEADS * LANES
    cos, sin = cos_ref[...], sin_ref[...]
    g_ref[0] = p[:, :BR]
    for h in range(N_HEADS):
        qh = _head_norm(p[:, qoff + h * LANES:qoff + (h + 1) * LANES], qg_ref[...], 1.0 / HD)
        q_ref[0, h] = (_rope(qh, cos, sin, 0, HD // 2) * (HD ** -0.5 * LOG2E)).astype(BF16)
    for h in range(GQA_KV_HEADS):
        kh = _head_norm(p[:, koff + h * LANES:koff + (h + 1) * LANES], kg_ref[...], 1.0 / HD)
        k_ref[0, h] = _rope(kh, cos, sin, 0, HD // 2).astype(BF16)
    _store_values_t(vt_ref, p[:, voff:voff + GQA_KV_HEADS * HD].T, GQA_KV_HEADS)


def _mla_epilogue(p, extra, outs):
    qa_ref, kva_ref, wuq_ref, wuk_ref, wuv_ref, qg_ref, kg_ref, cos_ref, sin_ref = extra
    g_ref, q_ref, k_ref, vt_ref = outs
    d_qk = MLA_NOPE + MLA_ROPE
    g_ref[0] = p[:, :BR]
    groups = [slice(0, TM // 2), slice(TM // 2, TM)]
    rms = lambda t, gain: t * lax.rsqrt(jnp.mean(t * t, -1, keepdims=True) + EPS) * gain
    cqn = [rms(p[r, BR:BR + MLA_RANK], qa_ref[...]) for r in groups]
    ckvn = [rms(p[r, BR + MLA_RANK:BR + 2 * MLA_RANK], kva_ref[...]) for r in groups]
    qf = [_dot(t, wuq_ref[...]) for t in cqn]
    kf = [_dot(t, wuk_ref[...]) + p[r, BR + 2 * MLA_RANK:BR + 2 * MLA_RANK + N_HEADS * LANES]
          for t, r in zip(ckvn, groups)]
    vtf = [_dot_nt(wuv_ref[...], t) for t in ckvn]
    cos = [cos_ref[r, :] for r in groups]
    sin = [sin_ref[r, :] for r in groups]
    for h in range(N_HEADS):
        qh = [_head_norm(t[:, h * LANES:(h + 1) * LANES], qg_ref[...], 1.0 / d_qk) for t in qf]
        kh = [_head_norm(t[:, h * LANES:(h + 1) * LANES], kg_ref[...], 1.0 / d_qk) for t in kf]
        for r, x, y, c, s in zip(groups, qh, kh, cos, sin):
            q_ref[0, h, r] = (_rope(x, c, s, MLA_NOPE, MLA_ROPE // 2) * (d_qk ** -0.5 * LOG2E)).astype(BF16)
            k_ref[0, h, r] = _rope(y, c, s, MLA_NOPE, MLA_ROPE // 2).astype(BF16)
    _store_values_t(vt_ref, jnp.concatenate(vtf, axis=1), N_HEADS)


def _prep_kernel(p_ref, *refs, n_extra, epilogue):
    epilogue(p_ref[0], refs[:n_extra], refs[n_extra:])


def _branch_prep(p, epilogue, consts, tables, n_kv, name):
    b, n, ncol = p.shape
    full = lambda a: pl.BlockSpec(a.shape, lambda bi, i: (0,) * a.ndim)
    tab = pl.BlockSpec((TM, LANES), lambda bi, i: (i, 0))
    specs, shapes = _head_specs(b, n, n_kv)
    return pl.pallas_call(
        functools.partial(_prep_kernel, n_extra=len(consts) + len(tables), epilogue=epilogue),
        grid=(b, n // TM),
        in_specs=[pl.BlockSpec((1, TM, ncol), lambda bi, i: (bi, i, 0))]
        + [full(a) for a in consts] + [tab] * len(tables),
        out_specs=specs,
        out_shape=shapes,
        compiler_params=_params("parallel", "parallel"),
        name=name,
    )(p, *consts, *tables)


def _branch_proj(x_all, g, sc, sh, w, n_lat_tiles, want_h, epilogue, consts, tables, n_kv, name):
    b, n, _ = x_all.shape
    full = lambda a: pl.BlockSpec(a.shape, lambda bi, i: (0,) * a.ndim)
    tab = pl.BlockSpec((TM, LANES), lambda bi, i: (i, 0))
    specs, shapes = _head_specs(b, n, n_kv)
    return _norm_proj(x_all, g, sc, sh, w, n_lat_tiles, want_h, epilogue=epilogue,
                      extra=tuple(consts) + tuple(tables),
                      extra_specs=[full(a) for a in consts] + [tab] * len(tables),
                      out_specs=specs, out_shape=shapes, name=name)


def _flash_kernel(q_ref, k_ref, vt_ref, o_ref, m_sc, acc_sc, s_sc, *, tk, nk, nq, kb):
    m_sc[...] = jnp.full(m_sc.shape, -jnp.inf, F32)
    acc_sc[...] = jnp.zeros(acc_sc.shape, F32)

    n_sub = tk // kb
    chains = range(nq)

    def scores(row0):
        k = k_ref[0, 0, pl.ds(row0, kb), :]
        return [lax.dot_general(k, q_ref[0, 0, c * FLASH_QB:(c + 1) * FLASH_QB, :],
                                (((1,), (1,)), ((), ())), preferred_element_type=F32) for c in chains]

    for c, s0 in zip(chains, scores(0)):
        s_sc[c] = s0

    def body(j, carry):
        off = pl.multiple_of(j * tk, tk)
        nxt = pl.multiple_of(jnp.minimum(j + 1, nk - 1) * tk, tk)
        m = [m_sc[c] for c in chains]
        acc = [acc_sc[c] for c in chains]
        s = [s_sc[c] for c in chains]
        for u in range(n_sub):
            s_next = scores(off + (u + 1) * kb if u + 1 < n_sub else nxt)
            vt = vt_ref[0, 0, :, pl.ds(off + u * kb, kb)]
            for c in chains:
                m_new = jnp.maximum(m[c], jnp.max(s[c], 0, keepdims=True))
                alpha = jnp.exp2(m[c] - m_new)
                p = jnp.exp2(s[c] - m_new)
                acc[c] = alpha * acc[c] + jnp.dot(vt, p.astype(BF16), preferred_element_type=F32)
                m[c] = m_new
            s = s_next
        for c in chains:
            m_sc[c], acc_sc[c], s_sc[c] = m[c], acc[c], s[c]
        return carry

    lax.fori_loop(0, nk, body, 0)
    for c in range(nq):
        o_ref[0, 0, :, c * FLASH_QB:(c + 1) * FLASH_QB] = acc_sc[c, :HD] / acc_sc[c, HD:HD + 1]


def _flash(q, k, vt, q0, n_q, k0, n_keys, tq, tk):
    b, hq = q.shape[:2]
    hk = k.shape[1]
    rep = hq // hk
    assert n_q % tq == 0 and n_keys % tk == 0 and tq % FLASH_QB == 0
    kb = min(FLASH_KB, tk)
    assert tk % kb == 0
    assert q0 % tq == 0 and k0 % n_keys == 0
    hv = vt.shape[2]
    nq = tq // FLASH_QB
    qb0, kb0 = q0 // tq, k0 // n_keys
    return pl.pallas_call(
        functools.partial(_flash_kernel, tk=tk, nk=n_keys // tk, nq=nq, kb=kb),
        grid=(b, hq, n_q // tq),
        in_specs=[pl.BlockSpec((1, 1, tq, LANES), lambda bi, h, i: (bi, h, qb0 + i, 0)),
                  pl.BlockSpec((1, 1, n_keys, LANES), lambda bi, h, i: (bi, h // rep, kb0, 0)),
                  pl.BlockSpec((1, 1, hv, n_keys), lambda bi, h, i: (bi, h // rep, 0, kb0))],
        out_specs=pl.BlockSpec((1, 1, HD, tq), lambda bi, h, i: (bi, h, 0, i)),
        out_shape=jax.ShapeDtypeStruct((b, hq, HD, n_q), F32),
        scratch_shapes=[pltpu.VMEM((nq, 1, FLASH_QB), F32), pltpu.VMEM((nq, hv, FLASH_QB), F32),
                        pltpu.VMEM((nq, kb, FLASH_QB), F32)],
        compiler_params=_params("parallel", "parallel", "arbitrary"),
        name="flash_attention",
    )(q, k, vt)


def _na_kernel(q_ref, k_ref, vt_ref, kc_ref, vtc_ref, bias_ref, o_ref, *, rows):
    kc = kc_ref[0, 0]
    vtc = vtc_ref[0, 0]
    nq, nk = NA_QROWS * GRID_W, NA_KROWS * GRID_W
    batches = range(NA_ROWS // NA_QROWS)
    r0 = [pl.program_id(2) * NA_ROWS + bi * NA_QROWS for bi in batches]
    koff = [pl.multiple_of(jnp.clip(r - NA_WIN_R // 2, 0, rows - NA_KROWS) * GRID_W, 2 * GRID_W) for r in r0]
    variant = [jnp.where(r == 0, 0, jnp.where(r == rows - NA_QROWS, 2, 1)) for r in r0]
    q = [q_ref[0, 0, bi * nq:(bi + 1) * nq, :] for bi in batches]
    nt = (((1,), (1,)), ((), ()))
    s_w = [lax.dot_general(k_ref[0, 0, pl.ds(o, nk), :], x, nt, preferred_element_type=F32) + bias_ref[v, 0]
           for o, x, v in zip(koff, q, variant)]
    s_c = [lax.dot_general(kc, x, nt, preferred_element_type=F32) for x in q]
    m = [jnp.maximum(jnp.max(x, 0, keepdims=True), jnp.max(y, 0, keepdims=True)) for x, y in zip(s_w, s_c)]
    p_w = [jnp.exp2(x - y).astype(BF16) for x, y in zip(s_w, m)]
    p_c = [jnp.exp2(x - y).astype(BF16) for x, y in zip(s_c, m)]
    acc = [jnp.dot(vt_ref[0, 0, :, pl.ds(o, nk)], x, preferred_element_type=F32)
           + jnp.dot(vtc, y, preferred_element_type=F32) for o, x, y in zip(koff, p_w, p_c)]
    for bi in batches:
        o_ref[0, 0, :, bi * nq:(bi + 1) * nq] = acc[bi][:HD] / acc[bi][HD:HD + 1]


def _na_bias_table(rpb):
    n_dr, n_dc = 2 * NA_WIN_R - 1, 2 * NA_WIN_C - 1
    sel_r = np.zeros((3, NA_KROWS, NA_QROWS, n_dr), np.float32)
    ok_r = np.zeros((3, NA_KROWS, NA_QROWS), bool)
    for kind in range(3):
        for i in range(NA_QROWS):
            start = (0, i, NA_KROWS - NA_WIN_R)[kind]
            shift = (0, -(NA_WIN_R // 2), NA_QROWS - NA_KROWS)[kind]
            for j in range(start, start + NA_WIN_R):
                sel_r[kind, j, i, j + shift - i + NA_WIN_R - 1] = 1.0
                ok_r[kind, j, i] = True
    cols = np.arange(GRID_W)
    cs = np.clip(cols - NA_WIN_C // 2, 0, GRID_W - NA_WIN_C)
    sel_c = np.zeros((GRID_W, GRID_W, n_dc), np.float32)
    ok_c = np.zeros((GRID_W, GRID_W), bool)
    for qc in range(GRID_W):
        for kcol in range(cs[qc], cs[qc] + NA_WIN_C):
            sel_c[kcol, qc, kcol - qc + NA_WIN_C - 1] = 1.0
            ok_c[kcol, qc] = True
    tab = jnp.einsum('hrc,vjir,kqc->vhjkiq', rpb, sel_r, sel_c, precision=HIGHEST)
    ok = ok_r[:, None, :, None, :, None] & ok_c[None, None, None, :, None, :]
    tab = jnp.where(ok, tab * LOG2E, -1e30)
    return tab.reshape(3, rpb.shape[0], NA_KROWS * GRID_W, NA_QROWS * GRID_W)


def _neighbourhood(q, k, vt, seq, n_ctx, bias):
    b, h = q.shape[:2]
    rows = seq // GRID_W
    assert rows >= NA_KROWS and rows % NA_ROWS == 0 and seq % n_ctx == 0
    tile = NA_ROWS * GRID_W
    hv = vt.shape[2]
    cb = seq // n_ctx
    return pl.pallas_call(
        functools.partial(_na_kernel, rows=rows),
        grid=(b, h, rows // NA_ROWS),
        in_specs=[pl.BlockSpec((1, 1, tile, LANES), lambda bi, hi, i: (bi, hi, i, 0)),
                  pl.BlockSpec((1, 1, seq, LANES), lambda bi, hi, i: (bi, hi, 0, 0)),
                  pl.BlockSpec((1, 1, hv, seq), lambda bi, hi, i: (bi, hi, 0, 0)),
                  pl.BlockSpec((1, 1, n_ctx, LANES), lambda bi, hi, i: (bi, hi, cb, 0)),
                  pl.BlockSpec((1, 1, hv, n_ctx), lambda bi, hi, i: (bi, hi, 0, cb)),
                  pl.BlockSpec((3, 1, NA_KROWS * GRID_W, NA_QROWS * GRID_W), lambda bi, hi, i: (0, hi, 0, 0))],
        out_specs=pl.BlockSpec((1, 1, HD, tile), lambda bi, hi, i: (bi, hi, 0, i)),
        out_shape=jax.ShapeDtypeStruct((b, h, HD, seq), F32),
        compiler_params=_params("parallel", "parallel", "arbitrary"),
        name="neighbourhood_attention",
    )(q, k, vt, k, vt, bias)


def _seg_sum(t, ones_blk):
    return _dot_exact_rhs(t, ones_blk)


def _rw_prep_kernel(p_ref, prev_ref, next_ref, mu_ref, kk_ref, ka_ref, rk_ref, w0_ref, w2_ref, a0_ref,
                    a2_ref, ones_ref, r_o, v_o, nkk_o, bonus_o, lw0_o, kd0_o, bd0_o, lw1_o, kd1_o, bd1_o,
                    *, n_lat_tiles, n_tiles):
    lw_o, kd_o, bd_o = (lw0_o, lw1_o), (kd0_o, kd1_o), (bd0_o, bd1_o)
    i = pl.program_id(1)
    u = p_ref[0, :, BR:]
    first = jnp.logical_or(i == 0, i == n_lat_tiles)
    last = jnp.logical_or(i == n_lat_tiles - 1, i == n_tiles - 1)
    prev_row = jnp.where(first, 0.0, prev_ref[0, 7:8, BR:])
    next_row = jnp.where(last, 0.0, next_ref[0, 0:1, BR:])
    row = lax.broadcasted_iota(jnp.int32, u.shape, 0)
    up = jnp.where(row == 0, prev_row, pltpu.roll(u, 1, 0))
    dn = jnp.where(row == TM - 1, next_row, pltpu.roll(u, TM - 1, 0))
    u = u + mu_ref[...] * (0.5 * (up + dn) - u)
    r, k, v = u[:, :BR], u[:, BR:2 * BR], u[:, 2 * BR:3 * BR]
    wl, al = u[:, 3 * BR:3 * BR + 2 * RW_LORA], u[:, 3 * BR + 2 * RW_LORA:]
    ones_blk = ones_ref[...]
    kk = k * kk_ref[...]
    kk = kk * lax.rsqrt(_seg_sum(kk * kk, ones_blk) + 1e-12)
    r_o[0] = r
    v_o[0] = v
    nkk_o[0] = -kk
    k_sum = jnp.zeros_like(k)
    for d in range(2):
        wl_d = wl[:, d * RW_LORA:(d + 1) * RW_LORA]
        al_d = al[:, d * RW_LORA:(d + 1) * RW_LORA]
        z = -(w0_ref[d] + _dot(jnp.tanh(wl_d), w2_ref[d]))
        softplus = jnp.maximum(z, 0.0) + jnp.log(1.0 + jnp.exp(-jnp.abs(z)))
        lw_o[d][0] = -jnp.exp(-softplus - 0.5)
        a = _sigmoid(a0_ref[d] + _dot(al_d, a2_ref[d]))
        k_d = k * (1.0 + (a - 1.0) * ka_ref[...])
        kd_o[d][0] = k_d
        bd_o[d][0] = kk * a
        k_sum = k_sum + k_d
    bonus_o[0] = _seg_sum(r * k_sum * rk_ref[...], ones_blk) * v


def _rw_prep(pb, mu, kk, ka, rk, w0, w2, a0, a2, ones_blk, n_lat_tiles):
    b, n, ncol = pb.shape
    nt = n // TM
    r8 = TM // 8
    full = lambda a: pl.BlockSpec(a.shape, lambda bi, i: (0,) * a.ndim)
    tok = pl.BlockSpec((1, TM, BR), lambda bi, i: (bi, i, 0))
    one = jax.ShapeDtypeStruct((b, n, BR), F32)
    return pl.pallas_call(
        functools.partial(_rw_prep_kernel, n_lat_tiles=n_lat_tiles, n_tiles=nt),
        grid=(b, nt),
        in_specs=[pl.BlockSpec((1, TM, ncol), lambda bi, i: (bi, i, 0)),
                  pl.BlockSpec((1, 8, ncol), lambda bi, i: (bi, jnp.maximum(i * r8 - 1, 0), 0)),
                  pl.BlockSpec((1, 8, ncol), lambda bi, i: (bi, jnp.minimum((i + 1) * r8, n // 8 - 1), 0)),
                  full(mu), full(kk), full(ka), full(rk), full(w0), full(w2), full(a0), full(a2),
                  full(ones_blk)],
        out_specs=[tok] * 10,
        out_shape=[one] * 10,
        compiler_params=_params("parallel", "parallel"),
        name="rwkv_prep",
    )(pb, pb, pb, mu, kk, ka, rk, w0, w2, a0, a2, ones_blk)


def _rw_scan_kernel(r_ref, v_ref, a_ref, lw_ref, k_ref, b_ref, y_ref, st_ref, *, rev, n_b):
    c, w = CHUNK, RW_PACK * HD

    @pl.when(pl.program_id(0) == 0)
    def _():
        st_ref[...] = jnp.zeros(st_ref.shape, F32)

    tt = lax.broadcasted_iota(jnp.int32, (c, w), 0)
    ss = lax.broadcasted_iota(jnp.int32, (c, w), 1) & (c - 1)
    strict = (ss > tt) if rev else (ss < tt)
    incl = (ss >= tt) if rev else (ss <= tt)
    eye = (ss == tt).astype(F32)
    t2 = lax.broadcasted_iota(jnp.int32, (c, c), 0)
    s2 = lax.broadcasted_iota(jnp.int32, (c, c), 1)
    m_incl = ((s2 >= t2) if rev else (s2 <= t2)).astype(BF16)
    blk = (lax.broadcasted_iota(jnp.int32, (w, w), 0) >> 6) == (lax.broadcasted_iota(jnp.int32, (w, w), 1) >> 6)

    def bd(t):
        tb = t.astype(BF16)
        return jnp.where(blk, jnp.concatenate([tb] * RW_PACK, axis=0), jnp.zeros((), BF16))

    def dtn(x, y):
        return lax.dot_general(x.astype(BF16), y.astype(BF16), (((0,), (0,)), ((), ())),
                               preferred_element_type=F32)

    chunks = tuple(range(RW_TT // c))
    chunks = chunks[::-1] if rev else chunks
    chains = [(bi, g) for bi in range(n_b) for g in range(BR // w)]
    units = [(ci, bi, g) for ci in chunks for (bi, g) in chains]
    sl = lambda ref, u: ref[u[1], u[0] * c:(u[0] + 1) * c, u[2] * w:(u[2] + 1) * w]

    lw = [sl(lw_ref, u) for u in units]
    cum = []
    for t in lw:
        hi, mid, lo = _split3(t)
        cum.append(jnp.dot(m_incl, hi, preferred_element_type=F32)
                   + jnp.dot(m_incl, mid, preferred_element_type=F32)
                   + jnp.dot(m_incl, lo, preferred_element_type=F32))
    tot = [jnp.sum(t, 0, keepdims=True) for t in lw]
    p_inv = [jnp.exp(-x) for x in cum]
    p_end = [jnp.exp(t - x) for t, x in zip(tot, cum)]
    a_t = [sl(a_ref, u) * jnp.exp(x - l) for u, x, l in zip(units, cum, lw)]
    r_t = [sl(r_ref, u) * jnp.exp(x) for u, x in zip(units, cum)]
    b_raw = [sl(b_ref, u) for u in units]
    k_raw = [sl(k_ref, u) for u in units]
    v = [sl(v_ref, u) for u in units]
    b_t = [x * p for x, p in zip(b_raw, p_inv)]
    k_t = [x * p for x, p in zip(k_raw, p_inv)]
    b_h = [x * p for x, p in zip(b_raw, p_end)]
    k_h = [x * p for x, p in zip(k_raw, p_end)]
    ar = [jnp.concatenate([x, y], axis=0) for x, y in zip(a_t, r_t)]
    g_b = [_dot_nt(x, bd(y)) for x, y in zip(ar, b_t)]
    g_k = [_dot_nt(x, bd(y)) for x, y in zip(ar, k_t)]
    lab = [jnp.where(strict, x[:c], 0.0) for x in g_b]
    lak = [jnp.where(strict, x[:c], 0.0) for x in g_k]
    qrb = [jnp.where(incl, x[c:], 0.0) for x in g_b]
    qrk = [jnp.where(incl, x[c:], 0.0) for x in g_k]
    tinv = [eye + x for x in lab]
    lp = lab
    for i in range(6):
        lp_bd = [bd(x) for x in lp]
        if i > 0:
            tinv = [x + _dot(x, y) for x, y in zip(tinv, lp_bd)]
        if i < 5:
            lp = [_dot(x, y) for x, y in zip(lp, lp_bd)]
    v_bd = [bd(x) for x in v]
    a_hat = [_dot(x, bd(y)) for x, y in zip(tinv, a_t)]
    wv = [_dot(x, y) for x, y in zip(lak, v_bd)]
    u_hat = [_dot(x, bd(y)) for x, y in zip(tinv, wv)]
    y_loc = [_dot(x, y) for x, y in zip(qrk, v_bd)]
    vk = [jnp.where(blk, dtn(x, y), 0.0) for x, y in zip(v, k_h)]
    p_c = [jnp.exp(t) for t in tot]

    n_ch = len(chains)
    for j in range(len(chunks)):
        idx = range(j * n_ch, (j + 1) * n_ch)
        st = [st_ref[q] for q in range(n_ch)]
        st_b = [s.astype(BF16) for s in st]
        u = [_dot_nt(a_hat[i], s) + u_hat[i] for i, s in zip(idx, st_b)]
        y = [_dot_nt(r_t[i], s) + _dot(qrb[i], bd(x)) + y_loc[i] for i, s, x in zip(idx, st_b, u)]
        new = [s * p_c[i] + jnp.where(blk, dtn(x, b_h[i]), 0.0) + vk[i] for i, s, x in zip(idx, st, u)]
        for q, i in enumerate(idx):
            y_ref[units[i][1], units[i][0] * c:(units[i][0] + 1) * c, units[i][2] * w:(units[i][2] + 1) * w] = y[q]
            st_ref[q] = new[q]


def _rw_scan(r, v, nkk, lw, kd, bd, n_ctx, rev):
    b, n, _ = r.shape
    nblk = n // RW_TT
    nctx = n_ctx // RW_TT
    if rev:
        blk_of = lambda i: nblk - 1 - i
    else:
        blk_of = lambda i: jnp.where(i < nctx, nblk - nctx + i, i - nctx)
    spec = pl.BlockSpec((b, RW_TT, BR), lambda i: (0, blk_of(i), 0))
    n_chains = b * (BR // (RW_PACK * HD))
    return pl.pallas_call(
        functools.partial(_rw_scan_kernel, rev=rev, n_b=b),
        grid=(nblk,),
        in_specs=[spec] * 6,
        out_specs=spec,
        out_shape=jax.ShapeDtypeStruct((b, n, BR), F32),
        scratch_shapes=[pltpu.VMEM((n_chains, RW_PACK * HD, RW_PACK * HD), F32)],
        compiler_params=_params("arbitrary"),
        name="rwkv_scan",
    )(r, v, nkk, lw, kd, bd)


def _merge_kernel(x_ref, h_ref, ya_ref, yac_ref, ybf_ref, ybr_ref, yc_ref, ycc_ref, yd_ref, ydc_ref,
                  ga_ref, gb_ref, gc_ref, gd_ref, bonus_ref, gnw_ref, gnb_ref, ones_ref, mgw_ref, mgb_ref,
                  wbr_ref, wout_ref, gt_ref, o_ref, *, n_lat_tiles):
    h = h_ref[0]
    ones_blk = ones_ref[...]
    is_ctx = pl.program_id(1) >= n_lat_tiles

    def token_major(lat_ref, ctx_ref):
        yt = jnp.where(is_ctx, ctx_ref[0], lat_ref[0])
        return yt.reshape(BR, TM).T

    yb = ybf_ref[0] + ybr_ref[0]
    mean = _seg_sum(yb, ones_blk) * (1.0 / HD)
    cen = yb - mean
    var = _seg_sum(cen * cen, ones_blk) * (1.0 / HD)
    yb = cen * lax.rsqrt(var + RW_GN_EPS) * gnw_ref[...] + gnb_ref[...] + bonus_ref[0]
    ys = (token_major(ya_ref, yac_ref), yb, token_major(yc_ref, ycc_ref), token_major(yd_ref, ydc_ref))
    gs = (ga_ref[0], gb_ref[0], gc_ref[0], gd_ref[0])
    acc = None
    for i in range(4):
        gate = _sigmoid(jnp.dot(h, mgw_ref[i], preferred_element_type=F32) + mgb_ref[i])
        term = gate * _dot(ys[i] * _silu(gs[i]), wbr_ref[i])
        acc = term if acc is None else acc + term
    o_ref[0] = x_ref[0] + gt_ref[0, 0] * _dot(acc, wout_ref[...])


def _merge(x_all, h, ya, ybf, ybr, yc, yd, pa, pb, pc, pd, bonus, gnw, gnb, ones_blk, mgw, mgb, wbr, wout, gt,
           n_lat_tiles, n_out):
    b, n, d = x_all.shape
    kind = lambda i: jnp.where(i < n_lat_tiles, 1, 0)
    tok = lambda w: pl.BlockSpec((1, TM, w), lambda bi, i: (bi, i, 0))
    full = lambda a: pl.BlockSpec(a.shape, lambda bi, i: (0,) * a.ndim)
    lat = pl.BlockSpec((1, N_HEADS, HD, TM), lambda bi, i: (bi, 0, 0, jnp.minimum(i, n_lat_tiles - 1)))
    ctx = pl.BlockSpec((1, N_HEADS, HD, TM), lambda bi, i: (bi, 0, 0, 0))
    return pl.pallas_call(
        functools.partial(_merge_kernel, n_lat_tiles=n_lat_tiles),
        grid=(b, n_out // TM),
        in_specs=[tok(d), tok(d), lat, ctx, tok(BR), tok(BR), lat, ctx, lat, ctx,
                  tok(BR), tok(BR), tok(BR), tok(BR), tok(BR),
                  full(gnw), full(gnb), full(ones_blk), full(mgw), full(mgb), full(wbr), full(wout),
                  pl.BlockSpec((1, 1, 1, d), lambda bi, i: (bi, kind(i), 0, 0))],
        out_specs=tok(d),
        out_shape=jax.ShapeDtypeStruct((b, n_out, d), F32),
        compiler_params=_params("parallel", "parallel"),
        name="merge_out",
    )(x_all, h, ya[0], ya[1], ybf, ybr, yc[0], yc[1], yd[0], yd[1], pa, pb, pc, pd, bonus,
      gnw, gnb, ones_blk, mgw, mgb, wbr, wout, gt)


def _pad_heads(w, n_heads, d):
    lead = w.shape[:-1]
    w = w.reshape(lead + (n_heads, d))
    w = jnp.pad(w, [(0, 0)] * len(lead) + [(0, 0), (0, LANES - d)])
    return w.reshape(lead + (n_heads * LANES,))


def _pad_vec(g):
    return jnp.pad(g, (0, LANES - g.shape[0])).reshape(1, LANES)


def _rope_tables(n_lat, n_ctx, d_rot, lo):
    t = np.arange(n_lat)
    row = (t // GRID_W).astype(np.float32)
    col = (t % GRID_W).astype(np.float32)
    n_freq = d_rot // 4
    inv = np.float32(ROPE_BASE) ** (-np.arange(n_freq, dtype=np.float32) / np.float32(n_freq))
    ang = np.concatenate([row[:, None] * inv, col[:, None] * inv], -1).astype(np.float32)
    cos, sin = np.cos(ang), np.sin(ang)
    half = d_rot // 2
    cos_t = np.ones((n_lat + n_ctx, LANES), np.float32)
    sin_t = np.zeros((n_lat + n_ctx, LANES), np.float32)
    cos_t[:n_lat, lo:lo + half] = cos
    cos_t[:n_lat, lo + half:lo + d_rot] = cos
    sin_t[:n_lat, lo:lo + half] = -sin
    sin_t[:n_lat, lo + half:lo + d_rot] = sin
    return jnp.asarray(cos_t), jnp.asarray(sin_t)


def _pick_tile(n, candidates):
    for c in candidates:
        if n % c == 0:
            return c
    raise ValueError(f"no tile for {n}")


def kernel(x, c, ctx, c_ctx, norm_g, mod_w, mod_b, w_in, na_qg, na_kg, na_rpb, rw_mu, rw_w0, rw_w2, rw_a0, rw_a2, rw_kk, rw_ka, rw_rk, rw_gn_w, rw_gn_b, mla_qa_g, mla_kva_g, mla_wuq, mla_wukv, mla_qg, mla_kg, gqa_qg, gqa_kg, mg_w, mg_b, w_br, w_out):
    bsz, seq, d = x.shape
    n_ctx = ctx.shape[1]
    depth = w_in.shape[0]
    assert d == D_MODEL and n_ctx == TM and seq % (NA_ROWS * GRID_W) == 0
    n = seq + n_ctx
    n_lat_tiles = seq // TM

    gqa_cos, gqa_sin = _rope_tables(seq, n_ctx, HD, 0)
    mla_cos, mla_sin = _rope_tables(seq, n_ctx, MLA_ROPE, MLA_NOPE)
    ones_blk = jnp.asarray(np.kron(np.eye(N_HEADS), np.ones((HD, HD))), BF16)
    tq = _pick_tile(seq, (1024, 512, 256))
    tk = _pick_tile(n, (3328, 1280, 256))

    def attend(q, k, vt):
        return (_flash(q, k, vt, 0, seq, 0, n, tq, tk), _flash(q, k, vt, seq, n_ctx, seq, n_ctx, n_ctx, n_ctx))

    x_all = jnp.concatenate([x, ctx], axis=1)
    cc = jnp.concatenate([c_ctx[None], c], axis=0)
    for l in range(depth):
        w = w_in[l]
        o = 0
        aq, ak, av, ag = (w[:, o + i * BR:o + (i + 1) * BR] for i in range(4))
        o += 4 * BR
        bu, bg = w[:, o:o + U_COLS], w[:, o + U_COLS:o + U_COLS + BR]
        o += U_COLS + BR
        ccq, cckv = w[:, o:o + MLA_RANK], w[:, o + MLA_RANK:o + 2 * MLA_RANK]
        ckr = w[:, o + 2 * MLA_RANK:o + 2 * MLA_RANK + MLA_ROPE]
        cg = w[:, o + 2 * MLA_RANK + MLA_ROPE:o + 2 * MLA_RANK + MLA_ROPE + BR]
        o += 2 * MLA_RANK + MLA_ROPE + BR
        dq = w[:, o:o + BR]
        dk = w[:, o + BR:o + BR + GQA_KV_HEADS * HD]
        dv = w[:, o + BR + GQA_KV_HEADS * HD:o + BR + 2 * GQA_KV_HEADS * HD]
        dg = w[:, o + BR + 2 * GQA_KV_HEADS * HD:]
        ckr_t = jnp.tile(jnp.pad(ckr, ((0, 0), (MLA_NOPE, LANES - MLA_NOPE - MLA_ROPE))), (1, N_HEADS))
        w_a = jnp.concatenate([ag, _pad_heads(aq, N_HEADS, HD), _pad_heads(ak, N_HEADS, HD), av], 1).astype(BF16)
        w_b = jnp.concatenate([bg, bu], 1).astype(BF16)
        w_c = jnp.concatenate([cg, ccq, cckv, ckr_t], 1).astype(BF16)
        w_d = jnp.concatenate([dg, _pad_heads(dq, N_HEADS, HD), _pad_heads(dk, GQA_KV_HEADS, HD), dv], 1).astype(BF16)
        wuq = _pad_heads(mla_wuq[l], N_HEADS, MLA_NOPE + MLA_ROPE).astype(BF16)
        wukv = mla_wukv[l].reshape(MLA_RANK, N_HEADS, MLA_NOPE + HD)
        wuk = _pad_heads(wukv[:, :, :MLA_NOPE].reshape(MLA_RANK, -1), N_HEADS, MLA_NOPE).astype(BF16)
        wuv = wukv[:, :, MLA_NOPE:].reshape(MLA_RANK, -1).T.astype(BF16)

        mod = _modulation(cc, mod_w[l], mod_b[l])
        sh, sc, gt = jnp.split(mod, 3, axis=-1)
        pair = lambda m: jnp.stack([jnp.broadcast_to(m[0], (bsz, d)), m[1:]], axis=1)[:, :, None, :]
        sh2, sc2, gt2 = pair(sh), pair(sc), pair(gt)

        row = lambda t: t.reshape(1, -1)
        norm = (x_all, norm_g[l], sc2, sh2)

        ga, qa, ka, vat, h = _branch_proj(*norm, w_a, n_lat_tiles, True, _na_epilogue,
                                          (_pad_vec(na_qg[l]), _pad_vec(na_kg[l])), (), N_HEADS, "proj_na")
        ya = (_neighbourhood(qa, ka, vat, seq, n_ctx, _na_bias_table(na_rpb[l])),
              _flash(qa, ka, vat, seq, n_ctx, seq, n_ctx, n_ctx, n_ctx))

        pb, = _norm_proj(*norm, w_b, n_lat_tiles, False)
        r, v, nkk, bonus, lw0, kd0, bd0, lw1, kd1, bd1 = _rw_prep(
            pb, row(rw_mu[l]), row(rw_kk[l]), row(rw_ka[l]), row(rw_rk[l]),
            rw_w0[l][:, None, :], rw_w2[l].astype(BF16), rw_a0[l][:, None, :], rw_a2[l].astype(BF16),
            ones_blk, n_lat_tiles)
        ybf = _rw_scan(r, v, nkk, lw0, kd0, bd0, n_ctx, False)
        ybr = _rw_scan(r, v, nkk, lw1, kd1, bd1, n_ctx, True)

        pc, = _norm_proj(*norm, w_c, n_lat_tiles, False)
        gc, qc, kc, vct = _branch_prep(pc, _mla_epilogue,
                                       (row(mla_qa_g[l]), row(mla_kva_g[l]), wuq, wuk, wuv,
                                        _pad_vec(mla_qg[l]), _pad_vec(mla_kg[l])), (mla_cos, mla_sin),
                                       N_HEADS, "mla_prep")
        yc = attend(qc, kc, vct)

        gd, qd, kd_, vdt = _branch_proj(*norm, w_d, n_lat_tiles, False, _gqa_epilogue,
                                        (_pad_vec(gqa_qg[l]), _pad_vec(gqa_kg[l])), (gqa_cos, gqa_sin),
                                        GQA_KV_HEADS, "proj_gqa")
        yd = attend(qd, kd_, vdt)

        n_out = n if l + 1 < depth else seq
        x_all = _merge(x_all, h, ya, ybf, ybr, yc, yd, ga, pb, gc, gd, bonus,
                       row(rw_gn_w[l]), row(rw_gn_b[l]), ones_blk,
                       mg_w[l].astype(BF16), mg_b[l][:, None, :], w_br[l].astype(BF16), w_out[l].astype(BF16),
                       gt2, n_lat_tiles, n_out)
    return x_all
```

```python
import functools

import numpy as np
import jax
import jax.numpy as jnp
from jax import lax
from jax.experimental import pallas as pl
from jax.experimental.pallas import tpu as pltpu

F32 = jnp.float32
BF16 = jnp.bfloat16
HIGHEST = lax.Precision.HIGHEST

D_MODEL = 1024
GRID_W = 64
BR = 512
HD = 64
N_HEADS = BR // HD
EPS = 1e-6
ROPE_BASE = 10000.0
NA_WIN_R = 8
NA_WIN_C = 16
RW_LORA = 64
RW_GN_EPS = 64e-5
MLA_RANK = 256
MLA_NOPE = 64
MLA_ROPE = 32
GQA_KV_HEADS = 2
U_COLS = 3 * BR + 4 * RW_LORA

LANES = 128
BF16_SUBLANES = 16
V_ROWS = HD + BF16_SUBLANES
VMEM_LIMIT = 56 * 1024 * 1024

TM = 256
CHUNK = 64
RW_TT = 128
RW_PACK = 4
FLASH_QB = 256
FLASH_KB = 256
LOG2E = 1.4426950408889634
NA_ROWS = 32
NA_QROWS = 4
NA_KROWS = 12


def _params(*sem):
    return pltpu.CompilerParams(dimension_semantics=sem, vmem_limit_bytes=VMEM_LIMIT)


def _dot(a, b):
    return jnp.dot(a.astype(BF16), b.astype(BF16), preferred_element_type=F32)


def _dot_nt(a, b):
    return lax.dot_general(a.astype(BF16), b.astype(BF16), (((1,), (1,)), ((), ())),
                           preferred_element_type=F32)


def _split3(t):
    hi = t.astype(BF16)
    r1 = t - hi.astype(F32)
    mid = r1.astype(BF16)
    return hi, mid, (r1 - mid.astype(F32)).astype(BF16)


def _dot_exact_rhs(a, b):
    hi, mid, lo = _split3(a)
    return (jnp.dot(hi, b, preferred_element_type=F32) + jnp.dot(mid, b, preferred_element_type=F32)
            + jnp.dot(lo, b, preferred_element_type=F32))


def _silu(t):
    return t / (1.0 + jnp.exp(-t))


def _sigmoid(t):
    return 1.0 / (1.0 + jnp.exp(-t))


def _mod_kernel(c_ref, w_ref, b_ref, o_ref):
    o_ref[...] = _dot(_silu(c_ref[...]), w_ref[...]) + b_ref[...]


def _modulation(cc, w, b):
    n = cc.shape[0]
    return pl.pallas_call(
        _mod_kernel,
        out_shape=jax.ShapeDtypeStruct((n, w.shape[1]), F32),
        compiler_params=_params(),
        name="adaln_mod",
    )(cc, w, b.reshape(1, -1))


def _norm_proj_kernel(x_ref, g_ref, sc_ref, sh_ref, w_ref, *refs, n_extra, want_h, epilogue):
    x = x_ref[0]
    xn = x * lax.rsqrt(jnp.mean(x * x, -1, keepdims=True) + EPS) * g_ref[...]
    h = (xn * (1.0 + sc_ref[0, 0]) + sh_ref[0, 0]).astype(BF16)
    p = jnp.dot(h, w_ref[...], preferred_element_type=F32)
    extra, outs = refs[:n_extra], refs[n_extra:]
    if want_h:
        outs[-1][0] = h
        outs = outs[:-1]
    epilogue(p, extra, outs)


def _store_projection(p, extra, outs):
    outs[0][0] = p


def _norm_proj(x_all, g, sc, sh, w, n_lat_tiles, want_h, epilogue=_store_projection, extra=(), extra_specs=(),
               out_specs=None, out_shape=None, name="norm_proj"):
    b, n, d = x_all.shape
    ncol = w.shape[1]
    kind = lambda i: jnp.where(i < n_lat_tiles, 1, 0)
    if out_specs is None:
        out_shape = [jax.ShapeDtypeStruct((b, n, ncol), F32)]
        out_specs = [pl.BlockSpec((1, TM, ncol), lambda bi, i: (bi, i, 0))]
    out_shape, out_specs = list(out_shape), list(out_specs)
    if want_h:
        out_shape.append(jax.ShapeDtypeStruct((b, n, d), BF16))
        out_specs.append(pl.BlockSpec((1, TM, d), lambda bi, i: (bi, i, 0)))
    return pl.pallas_call(
        functools.partial(_norm_proj_kernel, n_extra=len(extra), want_h=want_h, epilogue=epilogue),
        grid=(b, n // TM),
        in_specs=[
            pl.BlockSpec((1, TM, d), lambda bi, i: (bi, i, 0)),
            pl.BlockSpec((1, d), lambda bi, i: (0, 0)),
            pl.BlockSpec((1, 1, 1, d), lambda bi, i: (bi, kind(i), 0, 0)),
            pl.BlockSpec((1, 1, 1, d), lambda bi, i: (bi, kind(i), 0, 0)),
            pl.BlockSpec((d, ncol), lambda bi, i: (0, 0)),
        ] + list(extra_specs),
        out_specs=out_specs,
        out_shape=out_shape,
        compiler_params=_params("parallel", "parallel"),
        name=name,
    )(x_all, g.reshape(1, d), sc, sh, w, *extra)


def _head_norm(xh, gain, inv_d):
    ms = jnp.sum(xh * xh, -1, keepdims=True) * inv_d
    return xh * lax.rsqrt(ms + EPS) * gain


def _rope(xh, cos, sin, lo, half):
    lane = lax.broadcasted_iota(jnp.int32, xh.shape, 1)
    swapped = jnp.where(lane < lo + half, pltpu.roll(xh, LANES - half, 1), pltpu.roll(xh, half, 1))
    return xh * cos + swapped * sin


def _store_values_t(vt_ref, vt, n_heads):
    lead = lax.broadcasted_iota(jnp.int32, (V_ROWS - HD, vt.shape[1]), 0) == 0
    tail = jnp.where(lead, 1.0, 0.0).astype(BF16)
    for h in range(n_heads):
        vt_ref[0, h, :HD] = vt[h * HD:(h + 1) * HD].astype(BF16)
        vt_ref[0, h, HD:] = tail


def _head_specs(b, n, n_kv):
    hm = lambda bi, i: (bi, 0, i, 0)
    vm = lambda bi, i: (bi, 0, 0, i)
    specs = [pl.BlockSpec((1, TM, BR), lambda bi, i: (bi, i, 0)),
             pl.BlockSpec((1, N_HEADS, TM, LANES), hm),
             pl.BlockSpec((1, n_kv, TM, LANES), hm),
             pl.BlockSpec((1, n_kv, V_ROWS, TM), vm)]
    shapes = [jax.ShapeDtypeStruct((b, n, BR), F32),
              jax.ShapeDtypeStruct((b, N_HEADS, n, LANES), BF16),
              jax.ShapeDtypeStruct((b, n_kv, n, LANES), BF16),
              jax.ShapeDtypeStruct((b, n_kv, V_ROWS, n), BF16)]
    return specs, shapes


def _na_epilogue(p, extra, outs):
    qg_ref, kg_ref = extra
    g_ref, q_ref, k_ref, vt_ref = outs
    qoff, koff, voff = BR, BR + N_HEADS * LANES, BR + 2 * N_HEADS * LANES
    g_ref[0] = p[:, :BR]
    for h in range(N_HEADS):
        qh = p[:, qoff + h * LANES:qoff + (h + 1) * LANES]
        kh = p[:, koff + h * LANES:koff + (h + 1) * LANES]
        q_ref[0, h] = (_head_norm(qh, qg_ref[...], 1.0 / HD) * (HD ** -0.5 * LOG2E)).astype(BF16)
        k_ref[0, h] = _head_norm(kh, kg_ref[...], 1.0 / HD).astype(BF16)
    _store_values_t(vt_ref, p[:, voff:voff + BR].T, N_HEADS)


def _gqa_epilogue(p, extra, outs):
    qg_ref, kg_ref, cos_ref, sin_ref = extra
    g_ref, q_ref, k_ref, vt_ref = outs
    qoff, koff = BR, BR + N_HEADS * LANES
    voff = koff + GQA_KV_HEADS * LANES
    cos, sin = cos_ref[...], sin_ref[...]
    g_ref[0] = p[:, :BR]
    for h in range(N_HEADS):
        qh = _head_norm(p[:, qoff + h * LANES:qoff + (h + 1) * LANES], qg_ref[...], 1.0 / HD)
        q_ref[0, h] = (_rope(qh, cos, sin, 0, HD // 2) * (HD ** -0.5 * LOG2E)).astype(BF16)
    for h in range(GQA_KV_HEADS):
        kh = _head_norm(p[:, koff + h * LANES:koff + (h + 1) * LANES], kg_ref[...], 1.0 / HD)
        k_ref[0, h] = _rope(kh, cos, sin, 0, HD // 2).astype(BF16)
    _store_values_t(vt_ref, p[:, voff:voff + GQA_KV_HEADS * HD].T, GQA_KV_HEADS)


def _mla_epilogue(p, extra, outs):
    qa_ref, kva_ref, wuq_ref, wuk_ref, wuv_ref, qg_ref, kg_ref, cos_ref, sin_ref = extra
    g_ref, q_ref, k_ref, vt_ref = outs
    d_qk = MLA_NOPE + MLA_ROPE
    g_ref[0] = p[:, :BR]
    groups = [slice(0, TM // 2), slice(TM // 2, TM)]
    rms = lambda t, gain: t * lax.rsqrt(jnp.mean(t * t, -1, keepdims=True) + EPS) * gain
    cqn = [rms(p[r, BR:BR + MLA_RANK], qa_ref[...]) for r in groups]
    ckvn = [rms(p[r, BR + MLA_RANK:BR + 2 * MLA_RANK], kva_ref[...]) for r in groups]
    qf = [_dot(t, wuq_ref[...]) for t in cqn]
    kf = [_dot(t, wuk_ref[...]) + p[r, BR + 2 * MLA_RANK:BR + 2 * MLA_RANK + N_HEADS * LANES]
          for t, r in zip(ckvn, groups)]
    vtf = [_dot_nt(wuv_ref[...], t) for t in ckvn]
    cos = [cos_ref[r, :] for r in groups]
    sin = [sin_ref[r, :] for r in groups]
    for h in range(N_HEADS):
        qh = [_head_norm(t[:, h * LANES:(h + 1) * LANES], qg_ref[...], 1.0 / d_qk) for t in qf]
        kh = [_head_norm(t[:, h * LANES:(h + 1) * LANES], kg_ref[...], 1.0 / d_qk) for t in kf]
        for r, x, y, c, s in zip(groups, qh, kh, cos, sin):
            q_ref[0, h, r] = (_rope(x, c, s, MLA_NOPE, MLA_ROPE // 2) * (d_qk ** -0.5 * LOG2E)).astype(BF16)
            k_ref[0, h, r] = _rope(y, c, s, MLA_NOPE, MLA_ROPE // 2).astype(BF16)
    _store_values_t(vt_ref, jnp.concatenate(vtf, axis=1), N_HEADS)


def _prep_kernel(p_ref, *refs, n_extra, epilogue):
    epilogue(p_ref[0], refs[:n_extra], refs[n_extra:])


def _branch_prep(p, epilogue, consts, tables, n_kv, name):
    b, n, ncol = p.shape
    full = lambda a: pl.BlockSpec(a.shape, lambda bi, i: (0,) * a.ndim)
    tab = pl.BlockSpec((TM, LANES), lambda bi, i: (i, 0))
    specs, shapes = _head_specs(b, n, n_kv)
    return pl.pallas_call(
        functools.partial(_prep_kernel, n_extra=len(consts) + len(tables), epilogue=epilogue),
        grid=(b, n // TM),
        in_specs=[pl.BlockSpec((1, TM, ncol), lambda bi, i: (bi, i, 0))]
        + [full(a) for a in consts] + [tab] * len(tables),
        out_specs=specs,
        out_shape=shapes,
        compiler_params=_params("parallel", "parallel"),
        name=name,
    )(p, *consts, *tables)


def _branch_proj(x_all, g, sc, sh, w, n_lat_tiles, want_h, epilogue, consts, tables, n_kv, name):
    b, n, _ = x_all.shape
    full = lambda a: pl.BlockSpec(a.shape, lambda bi, i: (0,) * a.ndim)
    tab = pl.BlockSpec((TM, LANES), lambda bi, i: (i, 0))
    specs, shapes = _head_specs(b, n, n_kv)
    return _norm_proj(x_all, g, sc, sh, w, n_lat_tiles, want_h, epilogue=epilogue,
                      extra=tuple(consts) + tuple(tables),
                      extra_specs=[full(a) for a in consts] + [tab] * len(tables),
                      out_specs=specs, out_shape=shapes, name=name)


def _flash_kernel(q_ref, k_ref, vt_ref, o_ref, m_sc, acc_sc, s_sc, *, tk, nk, nq, kb):
    m_sc[...] = jnp.full(m_sc.shape, -jnp.inf, F32)
    acc_sc[...] = jnp.zeros(acc_sc.shape, F32)

    n_sub = tk // kb
    chains = range(nq)

    def scores(row0):
        k = k_ref[0, 0, pl.ds(row0, kb), :]
        return [lax.dot_general(k, q_ref[0, 0, c * FLASH_QB:(c + 1) * FLASH_QB, :],
                                (((1,), (1,)), ((), ())), preferred_element_type=F32) for c in chains]

    for c, s0 in zip(chains, scores(0)):
        s_sc[c] = s0

    def body(j, carry):
        off = pl.multiple_of(j * tk, tk)
        nxt = pl.multiple_of(jnp.minimum(j + 1, nk - 1) * tk, tk)
        m = [m_sc[c] for c in chains]
        acc = [acc_sc[c] for c in chains]
        s = [s_sc[c] for c in chains]
        for u in range(n_sub):
            k_next = k_ref[0, 0, pl.ds(off + (u + 1) * kb if u + 1 < n_sub else nxt, kb), :]
            vt = vt_ref[0, 0, :, pl.ds(off + u * kb, kb)]
            s_next = []
            for c in chains:
                s_next.append(lax.dot_general(k_next, q_ref[0, 0, c * FLASH_QB:(c + 1) * FLASH_QB, :],
                                              (((1,), (1,)), ((), ())), preferred_element_type=F32))
                m_new = jnp.maximum(m[c], jnp.max(s[c], 0, keepdims=True))
                alpha = jnp.exp2(m[c] - m_new)
                p = jnp.exp2(s[c] - m_new)
                acc[c] = alpha * acc[c] + jnp.dot(vt, p.astype(BF16), preferred_element_type=F32)
                m[c] = m_new
            s = s_next
        for c in chains:
            m_sc[c], acc_sc[c], s_sc[c] = m[c], acc[c], s[c]
        return carry

    lax.fori_loop(0, nk, body, 0)
    for c in range(nq):
        o_ref[0, 0, :, c * FLASH_QB:(c + 1) * FLASH_QB] = acc_sc[c, :HD] / acc_sc[c, HD:HD + 1]


def _flash(q, k, vt, q0, n_q, k0, n_keys, tq, tk):
    b, hq = q.shape[:2]
    hk = k.shape[1]
    rep = hq // hk
    assert n_q % tq == 0 and n_keys % tk == 0 and tq % FLASH_QB == 0
    kb = min(FLASH_KB, tk)
    assert tk % kb == 0
    assert q0 % tq == 0 and k0 % n_keys == 0
    hv = vt.shape[2]
    nq = tq // FLASH_QB
    qb0, kb0 = q0 // tq, k0 // n_keys
    return pl.pallas_call(
        functools.partial(_flash_kernel, tk=tk, nk=n_keys // tk, nq=nq, kb=kb),
        grid=(b, hq, n_q // tq),
        in_specs=[pl.BlockSpec((1, 1, tq, LANES), lambda bi, h, i: (bi, h, qb0 + i, 0)),
                  pl.BlockSpec((1, 1, n_keys, LANES), lambda bi, h, i: (bi, h // rep, kb0, 0)),
                  pl.BlockSpec((1, 1, hv, n_keys), lambda bi, h, i: (bi, h // rep, 0, kb0))],
        out_specs=pl.BlockSpec((1, 1, HD, tq), lambda bi, h, i: (bi, h, 0, i)),
        out_shape=jax.ShapeDtypeStruct((b, hq, HD, n_q), F32),
        scratch_shapes=[pltpu.VMEM((nq, 1, FLASH_QB), F32), pltpu.VMEM((nq, hv, FLASH_QB), F32),
                        pltpu.VMEM((nq, kb, FLASH_QB), F32)],
        compiler_params=_params("parallel", "parallel", "arbitrary"),
        name="flash_attention",
    )(q, k, vt)


def _na_kernel(q_ref, k_ref, vt_ref, kc_ref, vtc_ref, bias_ref, o_ref, *, rows):
    kc = kc_ref[0, 0]
    vtc = vtc_ref[0, 0]
    nq, nk = NA_QROWS * GRID_W, NA_KROWS * GRID_W
    batches = range(NA_ROWS // NA_QROWS)
    r0 = [pl.program_id(2) * NA_ROWS + bi * NA_QROWS for bi in batches]
    koff = [pl.multiple_of(jnp.clip(r - NA_WIN_R // 2, 0, rows - NA_KROWS) * GRID_W, 2 * GRID_W) for r in r0]
    variant = [jnp.where(r == 0, 0, jnp.where(r == rows - NA_QROWS, 2, 1)) for r in r0]
    q = [q_ref[0, 0, bi * nq:(bi + 1) * nq, :] for bi in batches]
    nt = (((1,), (1,)), ((), ()))
    s_w = [lax.dot_general(k_ref[0, 0, pl.ds(o, nk), :], x, nt, preferred_element_type=F32) + bias_ref[v, 0]
           for o, x, v in zip(koff, q, variant)]
    s_c = [lax.dot_general(kc, x, nt, preferred_element_type=F32) for x in q]
    m = [jnp.maximum(jnp.max(x, 0, keepdims=True), jnp.max(y, 0, keepdims=True)) for x, y in zip(s_w, s_c)]
    p_w = [jnp.exp2(x - y).astype(BF16) for x, y in zip(s_w, m)]
    p_c = [jnp.exp2(x - y).astype(BF16) for x, y in zip(s_c, m)]
    acc = [jnp.dot(vt_ref[0, 0, :, pl.ds(o, nk)], x, preferred_element_type=F32)
           + jnp.dot(vtc, y, preferred_element_type=F32) for o, x, y in zip(koff, p_w, p_c)]
    for bi in batches:
        o_ref[0, 0, :, bi * nq:(bi + 1) * nq] = acc[bi][:HD] / acc[bi][HD:HD + 1]


def _na_bias_table(rpb):
    n_dr, n_dc = 2 * NA_WIN_R - 1, 2 * NA_WIN_C - 1
    sel_r = np.zeros((3, NA_KROWS, NA_QROWS, n_dr), np.float32)
    ok_r = np.zeros((3, NA_KROWS, NA_QROWS), bool)
    for kind in range(3):
        for i in range(NA_QROWS):
            start = (0, i, NA_KROWS - NA_WIN_R)[kind]
            shift = (0, -(NA_WIN_R // 2), NA_QROWS - NA_KROWS)[kind]
            for j in range(start, start + NA_WIN_R):
                sel_r[kind, j, i, j + shift - i + NA_WIN_R - 1] = 1.0
                ok_r[kind, j, i] = True
    cols = np.arange(GRID_W)
    cs = np.clip(cols - NA_WIN_C // 2, 0, GRID_W - NA_WIN_C)
    sel_c = np.zeros((GRID_W, GRID_W, n_dc), np.float32)
    ok_c = np.zeros((GRID_W, GRID_W), bool)
    for qc in range(GRID_W):
        for kcol in range(cs[qc], cs[qc] + NA_WIN_C):
            sel_c[kcol, qc, kcol - qc + NA_WIN_C - 1] = 1.0
            ok_c[kcol, qc] = True
    tab = jnp.einsum('hrc,vjir,kqc->vhjkiq', rpb, sel_r, sel_c, precision=HIGHEST)
    ok = ok_r[:, None, :, None, :, None] & ok_c[None, None, None, :, None, :]
    tab = jnp.where(ok, tab * LOG2E, -1e30)
    return tab.reshape(3, rpb.shape[0], NA_KROWS * GRID_W, NA_QROWS * GRID_W)


def _neighbourhood(q, k, vt, seq, n_ctx, bias):
    b, h = q.shape[:2]
    rows = seq // GRID_W
    assert rows >= NA_KROWS and rows % NA_ROWS == 0 and seq % n_ctx == 0
    tile = NA_ROWS * GRID_W
    hv = vt.shape[2]
    cb = seq // n_ctx
    return pl.pallas_call(
        functools.partial(_na_kernel, rows=rows),
        grid=(b, h, rows // NA_ROWS),
        in_specs=[pl.BlockSpec((1, 1, tile, LANES), lambda bi, hi, i: (bi, hi, i, 0)),
                  pl.BlockSpec((1, 1, seq, LANES), lambda bi, hi, i: (bi, hi, 0, 0)),
                  pl.BlockSpec((1, 1, hv, seq), lambda bi, hi, i: (bi, hi, 0, 0)),
                  pl.BlockSpec((1, 1, n_ctx, LANES), lambda bi, hi, i: (bi, hi, cb, 0)),
                  pl.BlockSpec((1, 1, hv, n_ctx), lambda bi, hi, i: (bi, hi, 0, cb)),
                  pl.BlockSpec((3, 1, NA_KROWS * GRID_W, NA_QROWS * GRID_W), lambda bi, hi, i: (0, hi, 0, 0))],
        out_specs=pl.BlockSpec((1, 1, HD, tile), lambda bi, hi, i: (bi, hi, 0, i)),
        out_shape=jax.ShapeDtypeStruct((b, h, HD, seq), F32),
        compiler_params=_params("parallel", "parallel", "arbitrary"),
        name="neighbourhood_attention",
    )(q, k, vt, k, vt, bias)


def _seg_sum(t, ones_blk):
    return _dot_exact_rhs(t, ones_blk)


def _rw_prep_kernel(p_ref, prev_ref, next_ref, mu_ref, kk_ref, ka_ref, rk_ref, w0_ref, w2_ref, a0_ref,
                    a2_ref, ones_ref, r_o, v_o, nkk_o, bonus_o, lw0_o, kd0_o, bd0_o, lw1_o, kd1_o, bd1_o,
                    *, n_lat_tiles, n_tiles):
    lw_o, kd_o, bd_o = (lw0_o, lw1_o), (kd0_o, kd1_o), (bd0_o, bd1_o)
    i = pl.program_id(1)
    u = p_ref[0, :, BR:]
    first = jnp.logical_or(i == 0, i == n_lat_tiles)
    last = jnp.logical_or(i == n_lat_tiles - 1, i == n_tiles - 1)
    prev_row = jnp.where(first, 0.0, prev_ref[0, 7:8, BR:])
    next_row = jnp.where(last, 0.0, next_ref[0, 0:1, BR:])
    row = lax.broadcasted_iota(jnp.int32, u.shape, 0)
    up = jnp.where(row == 0, prev_row, pltpu.roll(u, 1, 0))
    dn = jnp.where(row == TM - 1, next_row, pltpu.roll(u, TM - 1, 0))
    u = u + mu_ref[...] * (0.5 * (up + dn) - u)
    r, k, v = u[:, :BR], u[:, BR:2 * BR], u[:, 2 * BR:3 * BR]
    wl, al = u[:, 3 * BR:3 * BR + 2 * RW_LORA], u[:, 3 * BR + 2 * RW_LORA:]
    ones_blk = ones_ref[...]
    kk = k * kk_ref[...]
    kk = kk * lax.rsqrt(_seg_sum(kk * kk, ones_blk) + 1e-12)
    r_o[0] = r
    v_o[0] = v
    nkk_o[0] = -kk
    k_sum = jnp.zeros_like(k)
    for d in range(2):
        wl_d = wl[:, d * RW_LORA:(d + 1) * RW_LORA]
        al_d = al[:, d * RW_LORA:(d + 1) * RW_LORA]
        z = -(w0_ref[d] + _dot(jnp.tanh(wl_d), w2_ref[d]))
        softplus = jnp.maximum(z, 0.0) + jnp.log(1.0 + jnp.exp(-jnp.abs(z)))
        lw_o[d][0] = -jnp.exp(-softplus - 0.5)
        a = _sigmoid(a0_ref[d] + _dot(al_d, a2_ref[d]))
        k_d = k * (1.0 + (a - 1.0) * ka_ref[...])
        kd_o[d][0] = k_d
        bd_o[d][0] = kk * a
        k_sum = k_sum + k_d
    bonus_o[0] = _seg_sum(r * k_sum * rk_ref[...], ones_blk) * v


def _rw_prep(pb, mu, kk, ka, rk, w0, w2, a0, a2, ones_blk, n_lat_tiles):
    b, n, ncol = pb.shape
    nt = n // TM
    r8 = TM // 8
    full = lambda a: pl.BlockSpec(a.shape, lambda bi, i: (0,) * a.ndim)
    tok = pl.BlockSpec((1, TM, BR), lambda bi, i: (bi, i, 0))
    one = jax.ShapeDtypeStruct((b, n, BR), F32)
    return pl.pallas_call(
        functools.partial(_rw_prep_kernel, n_lat_tiles=n_lat_tiles, n_tiles=nt),
        grid=(b, nt),
        in_specs=[pl.BlockSpec((1, TM, ncol), lambda bi, i: (bi, i, 0)),
                  pl.BlockSpec((1, 8, ncol), lambda bi, i: (bi, jnp.maximum(i * r8 - 1, 0), 0)),
                  pl.BlockSpec((1, 8, ncol), lambda bi, i: (bi, jnp.minimum((i + 1) * r8, n // 8 - 1), 0)),
                  full(mu), full(kk), full(ka), full(rk), full(w0), full(w2), full(a0), full(a2),
                  full(ones_blk)],
        out_specs=[tok] * 10,
        out_shape=[one] * 10,
        compiler_params=_params("parallel", "parallel"),
        name="rwkv_prep",
    )(pb, pb, pb, mu, kk, ka, rk, w0, w2, a0, a2, ones_blk)


def _rw_scan_kernel(r_ref, v_ref, a_ref, lw_ref, k_ref, b_ref, y_ref, st_ref, *, rev, n_b):
    c, w = CHUNK, RW_PACK * HD

    @pl.when(pl.program_id(0) == 0)
    def _():
        st_ref[...] = jnp.zeros(st_ref.shape, F32)

    tt = lax.broadcasted_iota(jnp.int32, (c, w), 0)
    ss = lax.broadcasted_iota(jnp.int32, (c, w), 1) & (c - 1)
    strict = (ss > tt) if rev else (ss < tt)
    incl = (ss >= tt) if rev else (ss <= tt)
    eye = (ss == tt).astype(F32)
    t2 = lax.broadcasted_iota(jnp.int32, (c, c), 0)
    s2 = lax.broadcasted_iota(jnp.int32, (c, c), 1)
    m_incl = ((s2 >= t2) if rev else (s2 <= t2)).astype(BF16)
    blk = (lax.broadcasted_iota(jnp.int32, (w, w), 0) >> 6) == (lax.broadcasted_iota(jnp.int32, (w, w), 1) >> 6)

    def bd(t):
        tb = t.astype(BF16)
        return jnp.where(blk, jnp.concatenate([tb] * RW_PACK, axis=0), jnp.zeros((), BF16))

    def dtn(x, y):
        return lax.dot_general(x.astype(BF16), y.astype(BF16), (((0,), (0,)), ((), ())),
                               preferred_element_type=F32)

    chunks = tuple(range(RW_TT // c))
    chunks = chunks[::-1] if rev else chunks
    chains = [(bi, g) for bi in range(n_b) for g in range(BR // w)]
    units = [(ci, bi, g) for ci in chunks for (bi, g) in chains]
    sl = lambda ref, u: ref[u[1], u[0] * c:(u[0] + 1) * c, u[2] * w:(u[2] + 1) * w]

    lw = [sl(lw_ref, u) for u in units]
    cum = []
    for t in lw:
        hi, mid, lo = _split3(t)
        cum.append(jnp.dot(m_incl, hi, preferred_element_type=F32)
                   + jnp.dot(m_incl, mid, preferred_element_type=F32)
                   + jnp.dot(m_incl, lo, preferred_element_type=F32))
    tot = [jnp.sum(t, 0, keepdims=True) for t in lw]
    p_inv = [jnp.exp(-x) for x in cum]
    p_end = [jnp.exp(t - x) for t, x in zip(tot, cum)]
    a_t = [sl(a_ref, u) * jnp.exp(x - l) for u, x, l in zip(units, cum, lw)]
    r_t = [sl(r_ref, u) * jnp.exp(x) for u, x in zip(units, cum)]
    b_raw = [sl(b_ref, u) for u in units]
    k_raw = [sl(k_ref, u) for u in units]
    v = [sl(v_ref, u) for u in units]
    b_t = [x * p for x, p in zip(b_raw, p_inv)]
    k_t = [x * p for x, p in zip(k_raw, p_inv)]
    b_h = [x * p for x, p in zip(b_raw, p_end)]
    k_h = [x * p for x, p in zip(k_raw, p_end)]
    ar = [jnp.concatenate([x, y], axis=0) for x, y in zip(a_t, r_t)]
    g_b = [_dot_nt(x, bd(y)) for x, y in zip(ar, b_t)]
    g_k = [_dot_nt(x, bd(y)) for x, y in zip(ar, k_t)]
    lab = [jnp.where(strict, x[:c], 0.0) for x in g_b]
    lak = [jnp.where(strict, x[:c], 0.0) for x in g_k]
    qrb = [jnp.where(incl, x[c:], 0.0) for x in g_b]
    qrk = [jnp.where(incl, x[c:], 0.0) for x in g_k]
    tinv = [eye + x for x in lab]
    lp = lab
    stack = lambda x, y: jnp.concatenate([x, y], axis=0)
    for i in range(6):
        lp_bd = [bd(x) for x in lp]
        if i == 0:
            lp = [_dot(x, y) for x, y in zip(lp, lp_bd)]
        elif i < 5:
            both = [_dot(stack(t, x), y) for t, x, y in zip(tinv, lp, lp_bd)]
            tinv = [t + z[:c] for t, z in zip(tinv, both)]
            lp = [z[c:] for z in both]
        else:
            tinv = [t + _dot(t, y) for t, y in zip(tinv, lp_bd)]
    both = [_dot(stack(x, y), bd(z)) for x, y, z in zip(lak, qrk, v)]
    wv = [z[:c] for z in both]
    y_loc = [z[c:] for z in both]
    a_hat = [_dot(x, bd(y)) for x, y in zip(tinv, a_t)]
    u_hat = [_dot(x, bd(y)) for x, y in zip(tinv, wv)]
    ar_hat = [stack(x, y) for x, y in zip(a_hat, r_t)]
    bk_h = [stack(x, y) for x, y in zip(b_h, k_h)]
    p_c = [jnp.exp(t) for t in tot]

    n_ch = len(chains)
    for j in range(len(chunks)):
        idx = range(j * n_ch, (j + 1) * n_ch)
        st = [st_ref[q] for q in range(n_ch)]
        ur = [_dot_nt(ar_hat[i], s) for i, s in zip(idx, st)]
        u = [z[:c] + u_hat[i] for i, z in zip(idx, ur)]
        y = [z[c:] + _dot(qrb[i], bd(x)) + y_loc[i] for i, z, x in zip(idx, ur, u)]
        new = [s * p_c[i] + jnp.where(blk, dtn(stack(x, v[i]), bk_h[i]), 0.0) for i, s, x in zip(idx, st, u)]
        for q, i in enumerate(idx):
            y_ref[units[i][1], units[i][0] * c:(units[i][0] + 1) * c, units[i][2] * w:(units[i][2] + 1) * w] = y[q]
            st_ref[q] = new[q]


def _rw_scan(r, v, nkk, lw, kd, bd, n_ctx, rev):
    b, n, _ = r.shape
    nblk = n // RW_TT
    nctx = n_ctx // RW_TT
    if rev:
        blk_of = lambda i: nblk - 1 - i
    else:
        blk_of = lambda i: jnp.where(i < nctx, nblk - nctx + i, i - nctx)
    spec = pl.BlockSpec((b, RW_TT, BR), lambda i: (0, blk_of(i), 0))
    n_chains = b * (BR // (RW_PACK * HD))
    return pl.pallas_call(
        functools.partial(_rw_scan_kernel, rev=rev, n_b=b),
        grid=(nblk,),
        in_specs=[spec] * 6,
        out_specs=spec,
        out_shape=jax.ShapeDtypeStruct((b, n, BR), F32),
        scratch_shapes=[pltpu.VMEM((n_chains, RW_PACK * HD, RW_PACK * HD), F32)],
        compiler_params=_params("arbitrary"),
        name="rwkv_scan",
    )(r, v, nkk, lw, kd, bd)


def _merge_kernel(x_ref, h_ref, ya_ref, yac_ref, ybf_ref, ybr_ref, yc_ref, ycc_ref, yd_ref, ydc_ref,
                  ga_ref, gb_ref, gc_ref, gd_ref, bonus_ref, gnw_ref, gnb_ref, ones_ref, mgw_ref, mgb_ref,
                  wbr_ref, wout_ref, gt_ref, o_ref, *, n_lat_tiles):
    h = h_ref[0]
    ones_blk = ones_ref[...]
    is_ctx = pl.program_id(1) >= n_lat_tiles

    def token_major(lat_ref, ctx_ref):
        yt = jnp.where(is_ctx, ctx_ref[0], lat_ref[0])
        return yt.reshape(BR, TM).T

    yb = ybf_ref[0] + ybr_ref[0]
    mean = _seg_sum(yb, ones_blk) * (1.0 / HD)
    cen = yb - mean
    var = _seg_sum(cen * cen, ones_blk) * (1.0 / HD)
    yb = cen * lax.rsqrt(var + RW_GN_EPS) * gnw_ref[...] + gnb_ref[...] + bonus_ref[0]
    ys = (token_major(ya_ref, yac_ref), yb, token_major(yc_ref, ycc_ref), token_major(yd_ref, ydc_ref))
    gs = (ga_ref[0], gb_ref[0], gc_ref[0], gd_ref[0])
    acc = None
    for i in range(4):
        gate = _sigmoid(jnp.dot(h, mgw_ref[i], preferred_element_type=F32) + mgb_ref[i])
        term = gate * _dot(ys[i] * _silu(gs[i]), wbr_ref[i])
        acc = term if acc is None else acc + term
    o_ref[0] = x_ref[0] + gt_ref[0, 0] * _dot(acc, wout_ref[...])


def _merge(x_all, h, ya, ybf, ybr, yc, yd, pa, pb, pc, pd, bonus, gnw, gnb, ones_blk, mgw, mgb, wbr, wout, gt,
           n_lat_tiles, n_out):
    b, n, d = x_all.shape
    kind = lambda i: jnp.where(i < n_lat_tiles, 1, 0)
    tok = lambda w: pl.BlockSpec((1, TM, w), lambda bi, i: (bi, i, 0))
    full = lambda a: pl.BlockSpec(a.shape, lambda bi, i: (0,) * a.ndim)
    lat = pl.BlockSpec((1, N_HEADS, HD, TM), lambda bi, i: (bi, 0, 0, jnp.minimum(i, n_lat_tiles - 1)))
    ctx = pl.BlockSpec((1, N_HEADS, HD, TM), lambda bi, i: (bi, 0, 0, 0))
    return pl.pallas_call(
        functools.partial(_merge_kernel, n_lat_tiles=n_lat_tiles),
        grid=(b, n_out // TM),
        in_specs=[tok(d), tok(d), lat, ctx, tok(BR), tok(BR), lat, ctx, lat, ctx,
                  tok(BR), tok(BR), tok(BR), tok(BR), tok(BR),
                  full(gnw), full(gnb), full(ones_blk), full(mgw), full(mgb), full(wbr), full(wout),
                  pl.BlockSpec((1, 1, 1, d), lambda bi, i: (bi, kind(i), 0, 0))],
        out_specs=tok(d),
        out_shape=jax.ShapeDtypeStruct((b, n_out, d), F32),
        compiler_params=_params("parallel", "parallel"),
        name="merge_out",
    )(x_all, h, ya[0], ya[1], ybf, ybr, yc[0], yc[1], yd[0], yd[1], pa, pb, pc, pd, bonus,
      gnw, gnb, ones_blk, mgw, mgb, wbr, wout, gt)


def _pad_heads(w, n_heads, d):
    lead = w.shape[:-1]
    w = w.reshape(lead + (n_heads, d))
    w = jnp.pad(w, [(0, 0)] * len(lead) + [(0, 0), (0, LANES - d)])
    return w.reshape(lead + (n_heads * LANES,))


def _pad_vec(g):
    return jnp.pad(g, (0, LANES - g.shape[0])).reshape(1, LANES)


def _rope_tables(n_lat, n_ctx, d_rot, lo):
    t = np.arange(n_lat)
    row = (t // GRID_W).astype(np.float32)
    col = (t % GRID_W).astype(np.float32)
    n_freq = d_rot // 4
    inv = np.float32(ROPE_BASE) ** (-np.arange(n_freq, dtype=np.float32) / np.float32(n_freq))
    ang = np.concatenate([row[:, None] * inv, col[:, None] * inv], -1).astype(np.float32)
    cos, sin = np.cos(ang), np.sin(ang)
    half = d_rot // 2
    cos_t = np.ones((n_lat + n_ctx, LANES), np.float32)
    sin_t = np.zeros((n_lat + n_ctx, LANES), np.float32)
    cos_t[:n_lat, lo:lo + half] = cos
    cos_t[:n_lat, lo + half:lo + d_rot] = cos
    sin_t[:n_lat, lo:lo + half] = -sin
    sin_t[:n_lat, lo + half:lo + d_rot] = sin
    return jnp.asarray(cos_t), jnp.asarray(sin_t)


def _pick_tile(n, candidates):
    for c in candidates:
        if n % c == 0:
            return c
    raise ValueError(f"no tile for {n}")


def kernel(x, c, ctx, c_ctx, norm_g, mod_w, mod_b, w_in, na_qg, na_kg, na_rpb, rw_mu, rw_w0, rw_w2, rw_a0, rw_a2, rw_kk, rw_ka, rw_rk, rw_gn_w, rw_gn_b, mla_qa_g, mla_kva_g, mla_wuq, mla_wukv, mla_qg, mla_kg, gqa_qg, gqa_kg, mg_w, mg_b, w_br, w_out):
    bsz, seq, d = x.shape
    n_ctx = ctx.shape[1]
    depth = w_in.shape[0]
    assert d == D_MODEL and n_ctx == TM and seq % (NA_ROWS * GRID_W) == 0
    n = seq + n_ctx
    n_lat_tiles = seq // TM

    gqa_cos, gqa_sin = _rope_tables(seq, n_ctx, HD, 0)
    mla_cos, mla_sin = _rope_tables(seq, n_ctx, MLA_ROPE, MLA_NOPE)
    ones_blk = jnp.asarray(np.kron(np.eye(N_HEADS), np.ones((HD, HD))), BF16)
    tq = _pick_tile(seq, (2048, 1024, 512, 256))
    tk = _pick_tile(n, (3328, 1280, 256))

    def attend(q, k, vt):
        return (_flash(q, k, vt, 0, seq, 0, n, tq, tk), _flash(q, k, vt, seq, n_ctx, seq, n_ctx, n_ctx, n_ctx))

    x_all = jnp.concatenate([x, ctx], axis=1)
    cc = jnp.concatenate([c_ctx[None], c], axis=0)
    for l in range(depth):
        w = w_in[l]
        o = 0
        aq, ak, av, ag = (w[:, o + i * BR:o + (i + 1) * BR] for i in range(4))
        o += 4 * BR
        bu, bg = w[:, o:o + U_COLS], w[:, o + U_COLS:o + U_COLS + BR]
        o += U_COLS + BR
        ccq, cckv = w[:, o:o + MLA_RANK], w[:, o + MLA_RANK:o + 2 * MLA_RANK]
        ckr = w[:, o + 2 * MLA_RANK:o + 2 * MLA_RANK + MLA_ROPE]
        cg = w[:, o + 2 * MLA_RANK + MLA_ROPE:o + 2 * MLA_RANK + MLA_ROPE + BR]
        o += 2 * MLA_RANK + MLA_ROPE + BR
        dq = w[:, o:o + BR]
        dk = w[:, o + BR:o + BR + GQA_KV_HEADS * HD]
        dv = w[:, o + BR + GQA_KV_HEADS * HD:o + BR + 2 * GQA_KV_HEADS * HD]
        dg = w[:, o + BR + 2 * GQA_KV_HEADS * HD:]
        ckr_t = jnp.tile(jnp.pad(ckr, ((0, 0), (MLA_NOPE, LANES - MLA_NOPE - MLA_ROPE))), (1, N_HEADS))
        w_a = jnp.concatenate([ag, _pad_heads(aq, N_HEADS, HD), _pad_heads(ak, N_HEADS, HD), av], 1).astype(BF16)
        w_b = jnp.concatenate([bg, bu], 1).astype(BF16)
        w_c = jnp.concatenate([cg, ccq, cckv, ckr_t], 1).astype(BF16)
        w_d = jnp.concatenate([dg, _pad_heads(dq, N_HEADS, HD), _pad_heads(dk, GQA_KV_HEADS, HD), dv], 1).astype(BF16)
        wuq = _pad_heads(mla_wuq[l], N_HEADS, MLA_NOPE + MLA_ROPE).astype(BF16)
        wukv = mla_wukv[l].reshape(MLA_RANK, N_HEADS, MLA_NOPE + HD)
        wuk = _pad_heads(wukv[:, :, :MLA_NOPE].reshape(MLA_RANK, -1), N_HEADS, MLA_NOPE).astype(BF16)
        wuv = wukv[:, :, MLA_NOPE:].reshape(MLA_RANK, -1).T.astype(BF16)

        mod = _modulation(cc, mod_w[l], mod_b[l])
        sh, sc, gt = jnp.split(mod, 3, axis=-1)
        pair = lambda m: jnp.stack([jnp.broadcast_to(m[0], (bsz, d)), m[1:]], axis=1)[:, :, None, :]
        sh2, sc2, gt2 = pair(sh), pair(sc), pair(gt)

        row = lambda t: t.reshape(1, -1)
        norm = (x_all, norm_g[l], sc2, sh2)

        ga, qa, ka, vat, h = _branch_proj(*norm, w_a, n_lat_tiles, True, _na_epilogue,
                                          (_pad_vec(na_qg[l]), _pad_vec(na_kg[l])), (), N_HEADS, "proj_na")
        ya = (_neighbourhood(qa, ka, vat, seq, n_ctx, _na_bias_table(na_rpb[l])),
              _flash(qa, ka, vat, seq, n_ctx, seq, n_ctx, n_ctx, n_ctx))

        pb, = _norm_proj(*norm, w_b, n_lat_tiles, False)
        r, v, nkk, bonus, lw0, kd0, bd0, lw1, kd1, bd1 = _rw_prep(
            pb, row(rw_mu[l]), row(rw_kk[l]), row(rw_ka[l]), row(rw_rk[l]),
            rw_w0[l][:, None, :], rw_w2[l].astype(BF16), rw_a0[l][:, None, :], rw_a2[l].astype(BF16),
            ones_blk, n_lat_tiles)
        ybf = _rw_scan(r, v, nkk, lw0, kd0, bd0, n_ctx, False)
        ybr = _rw_scan(r, v, nkk, lw1, kd1, bd1, n_ctx, True)

        pc, = _norm_proj(*norm, w_c, n_lat_tiles, False)
        gc, qc, kc, vct = _branch_prep(pc, _mla_epilogue,
                                       (row(mla_qa_g[l]), row(mla_kva_g[l]), wuq, wuk, wuv,
                                        _pad_vec(mla_qg[l]), _pad_vec(mla_kg[l])), (mla_cos, mla_sin),
                                       N_HEADS, "mla_prep")
        yc = attend(qc, kc, vct)

        gd, qd, kd_, vdt = _branch_proj(*norm, w_d, n_lat_tiles, False, _gqa_epilogue,
                                        (_pad_vec(gqa_qg[l]), _pad_vec(gqa_kg[l])), (gqa_cos, gqa_sin),
                                        GQA_KV_HEADS, "proj_gqa")
        yd = attend(qd, kd_, vdt)

        n_out = n if l + 1 < depth else seq
        x_all = _merge(x_all, h, ya, ybf, ybr, yc, yd, ga, pb, gc, gd, bonus,
                       row(rw_gn_w[l]), row(rw_gn_b[l]), ones_blk,
                       mg_w[l].astype(BF16), mg_b[l][:, None, :], w_br[l].astype(BF16), w_out[l].astype(BF16),
                       gt2, n_lat_tiles, n_out)
    return x_all
```

```python
import functools

import numpy as np
import jax
import jax.numpy as jnp
from jax import lax
from jax.experimental import pallas as pl
from jax.experimental.pallas import tpu as pltpu

F32 = jnp.float32
BF16 = jnp.bfloat16
HIGHEST = lax.Precision.HIGHEST

D_MODEL = 1024
GRID_W = 64
BR = 512
HD = 64
N_HEADS = BR // HD
EPS = 1e-6
ROPE_BASE = 10000.0
NA_WIN_R = 8
NA_WIN_C = 16
RW_LORA = 64
RW_GN_EPS = 64e-5
RW_DECAY_SCALE = float(np.exp(-0.5))
MLA_RANK = 256
MLA_NOPE = 64
MLA_ROPE = 32
GQA_KV_HEADS = 2
U_COLS = 3 * BR + 4 * RW_LORA

LANES = 128
BF16_SUBLANES = 16
V_ROWS = HD + BF16_SUBLANES
VMEM_LIMIT = 56 * 1024 * 1024

TM = 256
CHUNK = 64
RW_TT = 256
RW_PACK = 4
FLASH_QB = 256
FLASH_KB = 256
FLASH_MAX_BOUND = 50.0
LOG2E = 1.4426950408889634
NA_ROWS = 32
NA_QROWS = 4
NA_KROWS = 12


def _params(*sem):
    return pltpu.CompilerParams(dimension_semantics=sem, vmem_limit_bytes=VMEM_LIMIT)


def _dot(a, b):
    return jnp.dot(a.astype(BF16), b.astype(BF16), preferred_element_type=F32)


def _dot_nt(a, b):
    return lax.dot_general(a.astype(BF16), b.astype(BF16), (((1,), (1,)), ((), ())),
                           preferred_element_type=F32)


def _split3(t):
    hi = t.astype(BF16)
    r1 = t - hi.astype(F32)
    mid = r1.astype(BF16)
    return hi, mid, (r1 - mid.astype(F32)).astype(BF16)


def _dot_exact_rhs(a, b):
    hi, mid, lo = _split3(a)
    return (jnp.dot(hi, b, preferred_element_type=F32) + jnp.dot(mid, b, preferred_element_type=F32)
            + jnp.dot(lo, b, preferred_element_type=F32))


def _silu(t):
    return t / (1.0 + jnp.exp(-t))


def _sigmoid(t):
    return 1.0 / (1.0 + jnp.exp(-t))


def _mod_kernel(c_ref, w_ref, b_ref, o_ref):
    o_ref[...] = _dot(_silu(c_ref[...]), w_ref[...]) + b_ref[...]


def _modulation(cc, w, b):
    n = cc.shape[0]
    return pl.pallas_call(
        _mod_kernel,
        out_shape=jax.ShapeDtypeStruct((n, w.shape[1]), F32),
        compiler_params=_params(),
        name="adaln_mod",
    )(cc, w, b.reshape(1, -1))


def _norm_proj_kernel(x_ref, g_ref, sc_ref, sh_ref, w_ref, *refs, n_extra, want_h, epilogue):
    x = x_ref[0]
    xn = x * lax.rsqrt(jnp.mean(x * x, -1, keepdims=True) + EPS) * g_ref[...]
    h = (xn * (1.0 + sc_ref[0, 0]) + sh_ref[0, 0]).astype(BF16)
    p = jnp.dot(h, w_ref[...], preferred_element_type=F32)
    extra, outs = refs[:n_extra], refs[n_extra:]
    if want_h:
        outs[-1][0] = h
        outs = outs[:-1]
    epilogue(p, extra, outs)


def _store_projection(p, extra, outs):
    outs[0][0] = p


def _norm_proj(x_all, g, sc, sh, w, n_lat_tiles, want_h, epilogue=_store_projection, extra=(), extra_specs=(),
               out_specs=None, out_shape=None, name="norm_proj"):
    b, n, d = x_all.shape
    ncol = w.shape[1]
    kind = lambda i: jnp.where(i < n_lat_tiles, 1, 0)
    if out_specs is None:
        out_shape = [jax.ShapeDtypeStruct((b, n, ncol), F32)]
        out_specs = [pl.BlockSpec((1, TM, ncol), lambda bi, i: (bi, i, 0))]
    out_shape, out_specs = list(out_shape), list(out_specs)
    if want_h:
        out_shape.append(jax.ShapeDtypeStruct((b, n, d), BF16))
        out_specs.append(pl.BlockSpec((1, TM, d), lambda bi, i: (bi, i, 0)))
    return pl.pallas_call(
        functools.partial(_norm_proj_kernel, n_extra=len(extra), want_h=want_h, epilogue=epilogue),
        grid=(b, n // TM),
        in_specs=[
            pl.BlockSpec((1, TM, d), lambda bi, i: (bi, i, 0)),
            pl.BlockSpec((1, d), lambda bi, i: (0, 0)),
            pl.BlockSpec((1, 1, 1, d), lambda bi, i: (bi, kind(i), 0, 0)),
            pl.BlockSpec((1, 1, 1, d), lambda bi, i: (bi, kind(i), 0, 0)),
            pl.BlockSpec((d, ncol), lambda bi, i: (0, 0)),
        ] + list(extra_specs),
        out_specs=out_specs,
        out_shape=out_shape,
        compiler_params=_params("parallel", "parallel"),
        name=name,
    )(x_all, g.reshape(1, d), sc, sh, w, *extra)


def _head_norm(xh, gain, inv_d):
    ms = jnp.sum(xh * xh, -1, keepdims=True) * inv_d
    return xh * lax.rsqrt(ms + EPS) * gain


def _rope(xh, cos, sin, lo, half):
    lane = lax.broadcasted_iota(jnp.int32, xh.shape, 1)
    swapped = jnp.where(lane < lo + half, pltpu.roll(xh, LANES - half, 1), pltpu.roll(xh, half, 1))
    return xh * cos + swapped * sin


def _store_values_t(vt_ref, vt, n_heads):
    lead = lax.broadcasted_iota(jnp.int32, (V_ROWS - HD, vt.shape[1]), 0) == 0
    tail = jnp.where(lead, 1.0, 0.0).astype(BF16)
    for h in range(n_heads):
        vt_ref[0, h, :HD] = vt[h * HD:(h + 1) * HD].astype(BF16)
        vt_ref[0, h, HD:] = tail


def _head_specs(b, n, n_kv):
    hm = lambda bi, i: (bi, 0, i, 0)
    vm = lambda bi, i: (bi, 0, 0, i)
    specs = [pl.BlockSpec((1, TM, BR), lambda bi, i: (bi, i, 0)),
             pl.BlockSpec((1, N_HEADS, TM, LANES), hm),
             pl.BlockSpec((1, n_kv, TM, LANES), hm),
             pl.BlockSpec((1, n_kv, V_ROWS, TM), vm)]
    shapes = [jax.ShapeDtypeStruct((b, n, BR), F32),
              jax.ShapeDtypeStruct((b, N_HEADS, n, LANES), BF16),
              jax.ShapeDtypeStruct((b, n_kv, n, LANES), BF16),
              jax.ShapeDtypeStruct((b, n_kv, V_ROWS, n), BF16)]
    return specs, shapes


def _na_epilogue(p, extra, outs):
    qg_ref, kg_ref = extra
    g_ref, q_ref, k_ref, vt_ref = outs
    qoff, koff, voff = BR, BR + N_HEADS * LANES, BR + 2 * N_HEADS * LANES
    g_ref[0] = p[:, :BR]
    for h in range(N_HEADS):
        qh = p[:, qoff + h * LANES:qoff + (h + 1) * LANES]
        kh = p[:, koff + h * LANES:koff + (h + 1) * LANES]
        q_ref[0, h] = (_head_norm(qh, qg_ref[...], 1.0 / HD) * (HD ** -0.5 * LOG2E)).astype(BF16)
        k_ref[0, h] = _head_norm(kh, kg_ref[...], 1.0 / HD).astype(BF16)
    _store_values_t(vt_ref, p[:, voff:voff + BR].T, N_HEADS)


def _gqa_epilogue(p, extra, outs):
    qg_ref, kg_ref, cos_ref, sin_ref = extra
    g_ref, q_ref, k_ref, vt_ref = outs
    qoff, koff = BR, BR + N_HEADS * LANES
    voff = koff + GQA_KV_HEADS * LANES
    cos, sin = cos_ref[...], sin_ref[...]
    g_ref[0] = p[:, :BR]
    for h in range(N_HEADS):
        qh = _head_norm(p[:, qoff + h * LANES:qoff + (h + 1) * LANES], qg_ref[...], 1.0 / HD)
        q_ref[0, h] = (_rope(qh, cos, sin, 0, HD // 2) * (HD ** -0.5 * LOG2E)).astype(BF16)
    for h in range(GQA_KV_HEADS):
        kh = _head_norm(p[:, koff + h * LANES:koff + (h + 1) * LANES], kg_ref[...], 1.0 / HD)
        k_ref[0, h] = _rope(kh, cos, sin, 0, HD // 2).astype(BF16)
    _store_values_t(vt_ref, p[:, voff:voff + GQA_KV_HEADS * HD].T, GQA_KV_HEADS)


def _mla_epilogue(p, extra, outs):
    qa_ref, kva_ref, wuq_ref, wuk_ref, wuv_ref, qg_ref, kg_ref, cos_ref, sin_ref = extra
    g_ref, q_ref, k_ref, vt_ref = outs
    d_qk = MLA_NOPE + MLA_ROPE
    g_ref[0] = p[:, :BR]
    groups = [slice(0, TM // 2), slice(TM // 2, TM)]
    rms = lambda t, gain: t * lax.rsqrt(jnp.mean(t * t, -1, keepdims=True) + EPS) * gain
    cqn = [rms(p[r, BR:BR + MLA_RANK], qa_ref[...]) for r in groups]
    ckvn = [rms(p[r, BR + MLA_RANK:BR + 2 * MLA_RANK], kva_ref[...]) for r in groups]
    qf = [_dot(t, wuq_ref[...]) for t in cqn]
    kf = [_dot(t, wuk_ref[...]) + p[r, BR + 2 * MLA_RANK:BR + 2 * MLA_RANK + N_HEADS * LANES]
          for t, r in zip(ckvn, groups)]
    vtf = [_dot_nt(wuv_ref[...], t) for t in ckvn]
    cos = [cos_ref[r, :] for r in groups]
    sin = [sin_ref[r, :] for r in groups]
    for h in range(N_HEADS):
        qh = [_head_norm(t[:, h * LANES:(h + 1) * LANES], qg_ref[...], 1.0 / d_qk) for t in qf]
        kh = [_head_norm(t[:, h * LANES:(h + 1) * LANES], kg_ref[...], 1.0 / d_qk) for t in kf]
        for r, x, y, c, s in zip(groups, qh, kh, cos, sin):
            q_ref[0, h, r] = (_rope(x, c, s, MLA_NOPE, MLA_ROPE // 2) * (d_qk ** -0.5 * LOG2E)).astype(BF16)
            k_ref[0, h, r] = _rope(y, c, s, MLA_NOPE, MLA_ROPE // 2).astype(BF16)
    _store_values_t(vt_ref, jnp.concatenate(vtf, axis=1), N_HEADS)


def _prep_kernel(p_ref, *refs, n_extra, epilogue):
    epilogue(p_ref[0], refs[:n_extra], refs[n_extra:])


def _branch_prep(p, epilogue, consts, tables, n_kv, name):
    b, n, ncol = p.shape
    full = lambda a: pl.BlockSpec(a.shape, lambda bi, i: (0,) * a.ndim)
    tab = pl.BlockSpec((TM, LANES), lambda bi, i: (i, 0))
    specs, shapes = _head_specs(b, n, n_kv)
    return pl.pallas_call(
        functools.partial(_prep_kernel, n_extra=len(consts) + len(tables), epilogue=epilogue),
        grid=(b, n // TM),
        in_specs=[pl.BlockSpec((1, TM, ncol), lambda bi, i: (bi, i, 0))]
        + [full(a) for a in consts] + [tab] * len(tables),
        out_specs=specs,
        out_shape=shapes,
        compiler_params=_params("parallel", "parallel"),
        name=name,
    )(p, *consts, *tables)


def _branch_proj(x_all, g, sc, sh, w, n_lat_tiles, want_h, epilogue, consts, tables, n_kv, name):
    b, n, _ = x_all.shape
    full = lambda a: pl.BlockSpec(a.shape, lambda bi, i: (0,) * a.ndim)
    tab = pl.BlockSpec((TM, LANES), lambda bi, i: (i, 0))
    specs, shapes = _head_specs(b, n, n_kv)
    return _norm_proj(x_all, g, sc, sh, w, n_lat_tiles, want_h, epilogue=epilogue,
                      extra=tuple(consts) + tuple(tables),
                      extra_specs=[full(a) for a in consts] + [tab] * len(tables),
                      out_specs=specs, out_shape=shapes, name=name)


def _flash_kernel(q_ref, k_ref, vt_ref, *rest, tk, nk, nq, kb, bounded):
    if bounded:
        bound_ref, o_ref, m_sc, acc_sc, s_sc = rest
        m_sc[...] = jnp.broadcast_to(bound_ref[...], m_sc.shape)
    else:
        o_ref, m_sc, acc_sc, s_sc = rest
        m_sc[...] = jnp.full(m_sc.shape, -jnp.inf, F32)
    acc_sc[...] = jnp.zeros(acc_sc.shape, F32)

    n_sub = tk // kb
    chains = range(nq)

    def scores(row0):
        k = k_ref[0, 0, pl.ds(row0, kb), :]
        return [lax.dot_general(k, q_ref[0, 0, c * FLASH_QB:(c + 1) * FLASH_QB, :],
                                (((1,), (1,)), ((), ())), preferred_element_type=F32) for c in chains]

    for c, s0 in zip(chains, scores(0)):
        s_sc[c] = s0

    def body(j, carry):
        off = pl.multiple_of(j * tk, tk)
        nxt = pl.multiple_of(jnp.minimum(j + 1, nk - 1) * tk, tk)
        m = [m_sc[c] for c in chains]
        acc = [acc_sc[c] for c in chains]
        s = [s_sc[c] for c in chains]
        for u in range(n_sub):
            k_next = k_ref[0, 0, pl.ds(off + (u + 1) * kb if u + 1 < n_sub else nxt, kb), :]
            vt = vt_ref[0, 0, :, pl.ds(off + u * kb, kb)]
            s_next = []
            for c in chains:
                s_next.append(lax.dot_general(k_next, q_ref[0, 0, c * FLASH_QB:(c + 1) * FLASH_QB, :],
                                              (((1,), (1,)), ((), ())), preferred_element_type=F32))
                if bounded:
                    p = jnp.exp2(s[c] - m[c])
                    acc[c] = acc[c] + jnp.dot(vt, p.astype(BF16), preferred_element_type=F32)
                    continue
                m_new = jnp.maximum(m[c], jnp.max(s[c], 0, keepdims=True))
                alpha = jnp.exp2(m[c] - m_new)
                p = jnp.exp2(s[c] - m_new)
                acc[c] = alpha * acc[c] + jnp.dot(vt, p.astype(BF16), preferred_element_type=F32)
                m[c] = m_new
            s = s_next
        for c in chains:
            m_sc[c], acc_sc[c], s_sc[c] = m[c], acc[c], s[c]
        return carry

    lax.fori_loop(0, nk, body, 0)
    for c in range(nq):
        o_ref[0, 0, :, c * FLASH_QB:(c + 1) * FLASH_QB] = acc_sc[c, :HD] / acc_sc[c, HD:HD + 1]


def _flash(q, k, vt, q0, n_q, k0, n_keys, tq, tk, bound=None):
    b, hq = q.shape[:2]
    hk = k.shape[1]
    rep = hq // hk
    assert n_q % tq == 0 and n_keys % tk == 0 and tq % FLASH_QB == 0
    kb = min(FLASH_KB, tk)
    assert tk % kb == 0
    assert q0 % tq == 0 and k0 % n_keys == 0
    hv = vt.shape[2]
    nq = tq // FLASH_QB
    qb0, kb0 = q0 // tq, k0 // n_keys

    def call(bounded):
        extra_specs = [pl.BlockSpec((1, FLASH_QB), lambda bi, h, i: (0, 0))] if bounded else []
        extra = [jnp.full((1, FLASH_QB), bound, F32)] if bounded else []
        return pl.pallas_call(
            functools.partial(_flash_kernel, tk=tk, nk=n_keys // tk, nq=nq, kb=kb, bounded=bounded),
            grid=(b, hq, n_q // tq),
            in_specs=[pl.BlockSpec((1, 1, tq, LANES), lambda bi, h, i: (bi, h, qb0 + i, 0)),
                      pl.BlockSpec((1, 1, n_keys, LANES), lambda bi, h, i: (bi, h // rep, kb0, 0)),
                      pl.BlockSpec((1, 1, hv, n_keys), lambda bi, h, i: (bi, h // rep, 0, kb0))] + extra_specs,
            out_specs=pl.BlockSpec((1, 1, HD, tq), lambda bi, h, i: (bi, h, 0, i)),
            out_shape=jax.ShapeDtypeStruct((b, hq, HD, n_q), F32),
            scratch_shapes=[pltpu.VMEM((nq, 1, FLASH_QB), F32), pltpu.VMEM((nq, hv, FLASH_QB), F32),
                            pltpu.VMEM((nq, kb, FLASH_QB), F32)],
            compiler_params=_params("parallel", "parallel", "arbitrary"),
            name="flash_attention_bounded" if bounded else "flash_attention",
        )(q, k, vt, *extra)

    if bound is None:
        return call(False)
    return lax.cond(bound <= FLASH_MAX_BOUND, lambda: call(True), lambda: call(False))


def _na_kernel(q_ref, k_ref, vt_ref, kc_ref, vtc_ref, bias_ref, o_ref, *, rows):
    kc = kc_ref[0, 0]
    vtc = vtc_ref[0, 0]
    nq, nk = NA_QROWS * GRID_W, NA_KROWS * GRID_W
    batches = range(NA_ROWS // NA_QROWS)
    r0 = [pl.program_id(2) * NA_ROWS + bi * NA_QROWS for bi in batches]
    koff = [pl.multiple_of(jnp.clip(r - NA_WIN_R // 2, 0, rows - NA_KROWS) * GRID_W, 2 * GRID_W) for r in r0]
    variant = [jnp.where(r == 0, 0, jnp.where(r == rows - NA_QROWS, 2, 1)) for r in r0]
    q = [q_ref[0, 0, bi * nq:(bi + 1) * nq, :] for bi in batches]
    nt = (((1,), (1,)), ((), ()))
    s_w = [lax.dot_general(k_ref[0, 0, pl.ds(o, nk), :], x, nt, preferred_element_type=F32) + bias_ref[v, 0]
           for o, x, v in zip(koff, q, variant)]
    s_c = [lax.dot_general(kc, x, nt, preferred_element_type=F32) for x in q]
    m = [jnp.maximum(jnp.max(x, 0, keepdims=True), jnp.max(y, 0, keepdims=True)) for x, y in zip(s_w, s_c)]
    p_w = [jnp.exp2(x - y).astype(BF16) for x, y in zip(s_w, m)]
    p_c = [jnp.exp2(x - y).astype(BF16) for x, y in zip(s_c, m)]
    acc = [jnp.dot(vt_ref[0, 0, :, pl.ds(o, nk)], x, preferred_element_type=F32)
           + jnp.dot(vtc, y, preferred_element_type=F32) for o, x, y in zip(koff, p_w, p_c)]
    for bi in batches:
        o_ref[0, 0, :, bi * nq:(bi + 1) * nq] = acc[bi][:HD] / acc[bi][HD:HD + 1]


def _na_bias_table(rpb):
    n_dr, n_dc = 2 * NA_WIN_R - 1, 2 * NA_WIN_C - 1
    sel_r = np.zeros((3, NA_KROWS, NA_QROWS, n_dr), np.float32)
    ok_r = np.zeros((3, NA_KROWS, NA_QROWS), bool)
    for kind in range(3):
        for i in range(NA_QROWS):
            start = (0, i, NA_KROWS - NA_WIN_R)[kind]
            shift = (0, -(NA_WIN_R // 2), NA_QROWS - NA_KROWS)[kind]
            for j in range(start, start + NA_WIN_R):
                sel_r[kind, j, i, j + shift - i + NA_WIN_R - 1] = 1.0
                ok_r[kind, j, i] = True
    cols = np.arange(GRID_W)
    cs = np.clip(cols - NA_WIN_C // 2, 0, GRID_W - NA_WIN_C)
    sel_c = np.zeros((GRID_W, GRID_W, n_dc), np.float32)
    ok_c = np.zeros((GRID_W, GRID_W), bool)
    for qc in range(GRID_W):
        for kcol in range(cs[qc], cs[qc] + NA_WIN_C):
            sel_c[kcol, qc, kcol - qc + NA_WIN_C - 1] = 1.0
            ok_c[kcol, qc] = True
    tab = jnp.einsum('hrc,vjir,kqc->vhjkiq', rpb, sel_r, sel_c, precision=HIGHEST)
    ok = ok_r[:, None, :, None, :, None] & ok_c[None, None, None, :, None, :]
    tab = jnp.where(ok, tab * LOG2E, -1e30)
    return tab.reshape(3, rpb.shape[0], NA_KROWS * GRID_W, NA_QROWS * GRID_W)


def _neighbourhood(q, k, vt, seq, n_ctx, bias):
    b, h = q.shape[:2]
    rows = seq // GRID_W
    assert rows >= NA_KROWS and rows % NA_ROWS == 0 and seq % n_ctx == 0
    tile = NA_ROWS * GRID_W
    hv = vt.shape[2]
    cb = seq // n_ctx
    return pl.pallas_call(
        functools.partial(_na_kernel, rows=rows),
        grid=(b, h, rows // NA_ROWS),
        in_specs=[pl.BlockSpec((1, 1, tile, LANES), lambda bi, hi, i: (bi, hi, i, 0)),
                  pl.BlockSpec((1, 1, seq, LANES), lambda bi, hi, i: (bi, hi, 0, 0)),
                  pl.BlockSpec((1, 1, hv, seq), lambda bi, hi, i: (bi, hi, 0, 0)),
                  pl.BlockSpec((1, 1, n_ctx, LANES), lambda bi, hi, i: (bi, hi, cb, 0)),
                  pl.BlockSpec((1, 1, hv, n_ctx), lambda bi, hi, i: (bi, hi, 0, cb)),
                  pl.BlockSpec((3, 1, NA_KROWS * GRID_W, NA_QROWS * GRID_W), lambda bi, hi, i: (0, hi, 0, 0))],
        out_specs=pl.BlockSpec((1, 1, HD, tile), lambda bi, hi, i: (bi, hi, 0, i)),
        out_shape=jax.ShapeDtypeStruct((b, h, HD, seq), F32),
        compiler_params=_params("parallel", "parallel", "arbitrary"),
        name="neighbourhood_attention",
    )(q, k, vt, k, vt, bias)


def _seg_sum(t, ones_blk):
    return _dot_exact_rhs(t, ones_blk)


def _rw_prep_kernel(p_ref, prev_ref, next_ref, mu_ref, kk_ref, ka_ref, rk_ref, w0_ref, w2_ref, a0_ref,
                    a2_ref, ones_ref, r_o, v_o, nkk_o, bonus_o, lw0_o, kd0_o, bd0_o, lw1_o, kd1_o, bd1_o,
                    *, n_lat_tiles, n_tiles):
    lw_o, kd_o, bd_o = (lw0_o, lw1_o), (kd0_o, kd1_o), (bd0_o, bd1_o)
    i = pl.program_id(1)
    u = p_ref[0, :, BR:]
    first = jnp.logical_or(i == 0, i == n_lat_tiles)
    last = jnp.logical_or(i == n_lat_tiles - 1, i == n_tiles - 1)
    prev_row = jnp.where(first, 0.0, prev_ref[0, 7:8, BR:])
    next_row = jnp.where(last, 0.0, next_ref[0, 0:1, BR:])
    row = lax.broadcasted_iota(jnp.int32, u.shape, 0)
    up = jnp.where(row == 0, prev_row, pltpu.roll(u, 1, 0))
    dn = jnp.where(row == TM - 1, next_row, pltpu.roll(u, TM - 1, 0))
    u = u + mu_ref[...] * (0.5 * (up + dn) - u)
    r, k, v = u[:, :BR], u[:, BR:2 * BR], u[:, 2 * BR:3 * BR]
    wl, al = u[:, 3 * BR:3 * BR + 2 * RW_LORA], u[:, 3 * BR + 2 * RW_LORA:]
    ones_blk = ones_ref[...]
    kk = k * kk_ref[...]
    kk = kk * lax.rsqrt(_seg_sum(kk * kk, ones_blk) + 1e-12)
    r_o[0] = r
    v_o[0] = v
    nkk_o[0] = -kk
    k_sum = jnp.zeros_like(k)
    for d in range(2):
        wl_d = wl[:, d * RW_LORA:(d + 1) * RW_LORA]
        al_d = al[:, d * RW_LORA:(d + 1) * RW_LORA]
        t = w0_ref[d] + _dot(jnp.tanh(wl_d), w2_ref[d])
        lw_o[d][0] = -RW_DECAY_SCALE * _sigmoid(t)
        a = _sigmoid(a0_ref[d] + _dot(al_d, a2_ref[d]))
        k_d = k * (1.0 + (a - 1.0) * ka_ref[...])
        kd_o[d][0] = k_d
        bd_o[d][0] = kk * a
        k_sum = k_sum + k_d
    bonus_o[0] = _seg_sum(r * k_sum * rk_ref[...], ones_blk) * v


def _rw_prep(pb, mu, kk, ka, rk, w0, w2, a0, a2, ones_blk, n_lat_tiles):
    b, n, ncol = pb.shape
    nt = n // TM
    r8 = TM // 8
    full = lambda a: pl.BlockSpec(a.shape, lambda bi, i: (0,) * a.ndim)
    tok = pl.BlockSpec((1, TM, BR), lambda bi, i: (bi, i, 0))
    one = jax.ShapeDtypeStruct((b, n, BR), F32)
    return pl.pallas_call(
        functools.partial(_rw_prep_kernel, n_lat_tiles=n_lat_tiles, n_tiles=nt),
        grid=(b, nt),
        in_specs=[pl.BlockSpec((1, TM, ncol), lambda bi, i: (bi, i, 0)),
                  pl.BlockSpec((1, 8, ncol), lambda bi, i: (bi, jnp.maximum(i * r8 - 1, 0), 0)),
                  pl.BlockSpec((1, 8, ncol), lambda bi, i: (bi, jnp.minimum((i + 1) * r8, n // 8 - 1), 0)),
                  full(mu), full(kk), full(ka), full(rk), full(w0), full(w2), full(a0), full(a2),
                  full(ones_blk)],
        out_specs=[tok] * 10,
        out_shape=[one] * 10,
        compiler_params=_params("parallel", "parallel"),
        name="rwkv_prep",
    )(pb, pb, pb, mu, kk, ka, rk, w0, w2, a0, a2, ones_blk)


def _rw_scan_kernel(r_ref, v_ref, a_ref, lw_ref, k_ref, b_ref, y_ref, st_ref, *, rev, n_b):
    c, w = CHUNK, RW_PACK * HD

    @pl.when(pl.program_id(0) == 0)
    def _():
        st_ref[...] = jnp.zeros(st_ref.shape, F32)

    tt = lax.broadcasted_iota(jnp.int32, (c, w), 0)
    ss = lax.broadcasted_iota(jnp.int32, (c, w), 1) & (c - 1)
    strict = (ss > tt) if rev else (ss < tt)
    incl = (ss >= tt) if rev else (ss <= tt)
    eye = (ss == tt).astype(F32)
    t2 = lax.broadcasted_iota(jnp.int32, (c, c), 0)
    s2 = lax.broadcasted_iota(jnp.int32, (c, c), 1)
    m_incl = ((s2 >= t2) if rev else (s2 <= t2)).astype(BF16)
    blk = (lax.broadcasted_iota(jnp.int32, (w, w), 0) >> 6) == (lax.broadcasted_iota(jnp.int32, (w, w), 1) >> 6)

    def bd(t):
        tb = t.astype(BF16)
        return jnp.where(blk, jnp.concatenate([tb] * RW_PACK, axis=0), jnp.zeros((), BF16))

    def dtn(x, y):
        return lax.dot_general(x.astype(BF16), y.astype(BF16), (((0,), (0,)), ((), ())),
                               preferred_element_type=F32)

    chunks = tuple(range(RW_TT // c))
    chunks = chunks[::-1] if rev else chunks
    chains = [(bi, g) for bi in range(n_b) for g in range(BR // w)]
    units = [(ci, bi, g) for ci in chunks for (bi, g) in chains]
    sl = lambda ref, u: ref[u[1], u[0] * c:(u[0] + 1) * c, u[2] * w:(u[2] + 1) * w]

    lw = [sl(lw_ref, u) for u in units]
    cum = []
    for t in lw:
        hi, mid, lo = _split3(t)
        cum.append(jnp.dot(m_incl, hi, preferred_element_type=F32)
                   + jnp.dot(m_incl, mid, preferred_element_type=F32)
                   + jnp.dot(m_incl, lo, preferred_element_type=F32))
    tot = [jnp.sum(t, 0, keepdims=True) for t in lw]
    p_inv = [jnp.exp(-x) for x in cum]
    p_end = [jnp.exp(t - x) for t, x in zip(tot, cum)]
    a_t = [sl(a_ref, u) * jnp.exp(x - l) for u, x, l in zip(units, cum, lw)]
    r_t = [sl(r_ref, u) * jnp.exp(x) for u, x in zip(units, cum)]
    b_raw = [sl(b_ref, u) for u in units]
    k_raw = [sl(k_ref, u) for u in units]
    v = [sl(v_ref, u) for u in units]
    b_t = [x * p for x, p in zip(b_raw, p_inv)]
    k_t = [x * p for x, p in zip(k_raw, p_inv)]
    b_h = [x * p for x, p in zip(b_raw, p_end)]
    k_h = [x * p for x, p in zip(k_raw, p_end)]
    ar = [jnp.concatenate([x, y], axis=0) for x, y in zip(a_t, r_t)]
    g_b = [_dot_nt(x, bd(y)) for x, y in zip(ar, b_t)]
    g_k = [_dot_nt(x, bd(y)) for x, y in zip(ar, k_t)]
    lab = [jnp.where(strict, x[:c], 0.0) for x in g_b]
    lak = [jnp.where(strict, x[:c], 0.0) for x in g_k]
    qrb = [jnp.where(incl, x[c:], 0.0) for x in g_b]
    qrk = [jnp.where(incl, x[c:], 0.0) for x in g_k]
    tinv = [eye + x for x in lab]
    lp = lab
    stack = lambda x, y: jnp.concatenate([x, y], axis=0)
    for i in range(6):
        lp_bd = [bd(x) for x in lp]
        if i == 0:
            lp = [_dot(x, y) for x, y in zip(lp, lp_bd)]
        elif i < 5:
            both = [_dot(stack(t, x), y) for t, x, y in zip(tinv, lp, lp_bd)]
            tinv = [t + z[:c] for t, z in zip(tinv, both)]
            lp = [z[c:] for z in both]
        else:
            tinv = [t + _dot(t, y) for t, y in zip(tinv, lp_bd)]
    both = [_dot(stack(x, y), bd(z)) for x, y, z in zip(lak, qrk, v)]
    wv = [z[:c] for z in both]
    y_loc = [z[c:] for z in both]
    a_hat = [_dot(x, bd(y)) for x, y in zip(tinv, a_t)]
    u_hat = [_dot(x, bd(y)) for x, y in zip(tinv, wv)]
    ar_hat = [stack(x, y) for x, y in zip(a_hat, r_t)]
    bk_h = [stack(x, y) for x, y in zip(b_h, k_h)]
    p_c = [jnp.exp(t) for t in tot]

    n_ch = len(chains)
    for j in range(len(chunks)):
        idx = range(j * n_ch, (j + 1) * n_ch)
        st = [st_ref[q] for q in range(n_ch)]
        ur = [_dot_nt(ar_hat[i], s) for i, s in zip(idx, st)]
        u = [z[:c] + u_hat[i] for i, z in zip(idx, ur)]
        y = [z[c:] + _dot(qrb[i], bd(x)) + y_loc[i] for i, z, x in zip(idx, ur, u)]
        new = [s * p_c[i] + jnp.where(blk, dtn(stack(x, v[i]), bk_h[i]), 0.0) for i, s, x in zip(idx, st, u)]
        for q, i in enumerate(idx):
            y_ref[units[i][1], units[i][0] * c:(units[i][0] + 1) * c, units[i][2] * w:(units[i][2] + 1) * w] = y[q]
            st_ref[q] = new[q]


def _rw_scan(r, v, nkk, lw, kd, bd, n_ctx, rev):
    b, n, _ = r.shape
    nblk = n // RW_TT
    nctx = n_ctx // RW_TT
    if rev:
        blk_of = lambda i: nblk - 1 - i
    else:
        blk_of = lambda i: jnp.where(i < nctx, nblk - nctx + i, i - nctx)
    spec = pl.BlockSpec((b, RW_TT, BR), lambda i: (0, blk_of(i), 0))
    n_chains = b * (BR // (RW_PACK * HD))
    return pl.pallas_call(
        functools.partial(_rw_scan_kernel, rev=rev, n_b=b),
        grid=(nblk,),
        in_specs=[spec] * 6,
        out_specs=spec,
        out_shape=jax.ShapeDtypeStruct((b, n, BR), F32),
        scratch_shapes=[pltpu.VMEM((n_chains, RW_PACK * HD, RW_PACK * HD), F32)],
        compiler_params=_params("arbitrary"),
        name="rwkv_scan",
    )(r, v, nkk, lw, kd, bd)


def _merge_kernel(x_ref, h_ref, ya_ref, yac_ref, ybf_ref, ybr_ref, yc_ref, ycc_ref, yd_ref, ydc_ref,
                  ga_ref, gb_ref, gc_ref, gd_ref, bonus_ref, gnw_ref, gnb_ref, ones_ref, mgw_ref, mgb_ref,
                  wbr_ref, wout_ref, gt_ref, o_ref, *, n_lat_tiles):
    h = h_ref[0]
    ones_blk = ones_ref[...]
    is_ctx = pl.program_id(1) >= n_lat_tiles

    def token_major(lat_ref, ctx_ref):
        yt = jnp.where(is_ctx, ctx_ref[0], lat_ref[0])
        return yt.reshape(BR, TM).T

    yb = ybf_ref[0] + ybr_ref[0]
    mean = _seg_sum(yb, ones_blk) * (1.0 / HD)
    cen = yb - mean
    var = _seg_sum(cen * cen, ones_blk) * (1.0 / HD)
    yb = cen * lax.rsqrt(var + RW_GN_EPS) * gnw_ref[...] + gnb_ref[...] + bonus_ref[0]
    ys = (token_major(ya_ref, yac_ref), yb, token_major(yc_ref, ycc_ref), token_major(yd_ref, ydc_ref))
    gs = (ga_ref[0], gb_ref[0], gc_ref[0], gd_ref[0])
    acc = None
    for i in range(4):
        gate = _sigmoid(jnp.dot(h, mgw_ref[i], preferred_element_type=F32) + mgb_ref[i])
        term = gate * _dot(ys[i] * _silu(gs[i]), wbr_ref[i])
        acc = term if acc is None else acc + term
    o_ref[0] = x_ref[0] + gt_ref[0, 0] * _dot(acc, wout_ref[...])


def _merge(x_all, h, ya, ybf, ybr, yc, yd, pa, pb, pc, pd, bonus, gnw, gnb, ones_blk, mgw, mgb, wbr, wout, gt,
           n_lat_tiles, n_out):
    b, n, d = x_all.shape
    kind = lambda i: jnp.where(i < n_lat_tiles, 1, 0)
    tok = lambda w: pl.BlockSpec((1, TM, w), lambda bi, i: (bi, i, 0))
    full = lambda a: pl.BlockSpec(a.shape, lambda bi, i: (0,) * a.ndim)
    lat = pl.BlockSpec((1, N_HEADS, HD, TM), lambda bi, i: (bi, 0, 0, jnp.minimum(i, n_lat_tiles - 1)))
    ctx = pl.BlockSpec((1, N_HEADS, HD, TM), lambda bi, i: (bi, 0, 0, 0))
    return pl.pallas_call(
        functools.partial(_merge_kernel, n_lat_tiles=n_lat_tiles),
        grid=(b, n_out // TM),
        in_specs=[tok(d), tok(d), lat, ctx, tok(BR), tok(BR), lat, ctx, lat, ctx,
                  tok(BR), tok(BR), tok(BR), tok(BR), tok(BR),
                  full(gnw), full(gnb), full(ones_blk), full(mgw), full(mgb), full(wbr), full(wout),
                  pl.BlockSpec((1, 1, 1, d), lambda bi, i: (bi, kind(i), 0, 0))],
        out_specs=tok(d),
        out_shape=jax.ShapeDtypeStruct((b, n_out, d), F32),
        compiler_params=_params("parallel", "parallel"),
        name="merge_out",
    )(x_all, h, ya[0], ya[1], ybf, ybr, yc[0], yc[1], yd[0], yd[1], pa, pb, pc, pd, bonus,
      gnw, gnb, ones_blk, mgw, mgb, wbr, wout, gt)


def _pad_heads(w, n_heads, d):
    lead = w.shape[:-1]
    w = w.reshape(lead + (n_heads, d))
    w = jnp.pad(w, [(0, 0)] * len(lead) + [(0, 0), (0, LANES - d)])
    return w.reshape(lead + (n_heads * LANES,))


def _pad_vec(g):
    return jnp.pad(g, (0, LANES - g.shape[0])).reshape(1, LANES)


def _rope_tables(n_lat, n_ctx, d_rot, lo):
    t = np.arange(n_lat)
    row = (t // GRID_W).astype(np.float32)
    col = (t % GRID_W).astype(np.float32)
    n_freq = d_rot // 4
    inv = np.float32(ROPE_BASE) ** (-np.arange(n_freq, dtype=np.float32) / np.float32(n_freq))
    ang = np.concatenate([row[:, None] * inv, col[:, None] * inv], -1).astype(np.float32)
    cos, sin = np.cos(ang), np.sin(ang)
    half = d_rot // 2
    cos_t = np.ones((n_lat + n_ctx, LANES), np.float32)
    sin_t = np.zeros((n_lat + n_ctx, LANES), np.float32)
    cos_t[:n_lat, lo:lo + half] = cos
    cos_t[:n_lat, lo + half:lo + d_rot] = cos
    sin_t[:n_lat, lo:lo + half] = -sin
    sin_t[:n_lat, lo + half:lo + d_rot] = sin
    return jnp.asarray(cos_t), jnp.asarray(sin_t)


def _pick_tile(n, candidates):
    for c in candidates:
        if n % c == 0:
            return c
    raise ValueError(f"no tile for {n}")


def kernel(x, c, ctx, c_ctx, norm_g, mod_w, mod_b, w_in, na_qg, na_kg, na_rpb, rw_mu, rw_w0, rw_w2, rw_a0, rw_a2, rw_kk, rw_ka, rw_rk, rw_gn_w, rw_gn_b, mla_qa_g, mla_kva_g, mla_wuq, mla_wukv, mla_qg, mla_kg, gqa_qg, gqa_kg, mg_w, mg_b, w_br, w_out):
    bsz, seq, d = x.shape
    n_ctx = ctx.shape[1]
    depth = w_in.shape[0]
    assert d == D_MODEL and n_ctx == TM and seq % (NA_ROWS * GRID_W) == 0
    n = seq + n_ctx
    n_lat_tiles = seq // TM

    gqa_cos, gqa_sin = _rope_tables(seq, n_ctx, HD, 0)
    mla_cos, mla_sin = _rope_tables(seq, n_ctx, MLA_ROPE, MLA_NOPE)
    ones_blk = jnp.asarray(np.kron(np.eye(N_HEADS), np.ones((HD, HD))), BF16)
    tq = _pick_tile(seq, (2048, 1024, 512, 256))
    tk = _pick_tile(n, (3328, 1280, 256))

    def attend(q, k, vt, bound):
        return (_flash(q, k, vt, 0, seq, 0, n, tq, tk, bound),
                _flash(q, k, vt, seq, n_ctx, seq, n_ctx, n_ctx, n_ctx))

    def logit_bound(q_gain, k_gain, d_head):
        return (d_head ** 0.5 * LOG2E * 1.02) * jnp.max(jnp.abs(q_gain)) * jnp.max(jnp.abs(k_gain))

    x_all = jnp.concatenate([x, ctx], axis=1)
    cc = jnp.concatenate([c_ctx[None], c], axis=0)
    for l in range(depth):
        w = w_in[l]
        o = 0
        aq, ak, av, ag = (w[:, o + i * BR:o + (i + 1) * BR] for i in range(4))
        o += 4 * BR
        bu, bg = w[:, o:o + U_COLS], w[:, o + U_COLS:o + U_COLS + BR]
        o += U_COLS + BR
        ccq, cckv = w[:, o:o + MLA_RANK], w[:, o + MLA_RANK:o + 2 * MLA_RANK]
        ckr = w[:, o + 2 * MLA_RANK:o + 2 * MLA_RANK + MLA_ROPE]
        cg = w[:, o + 2 * MLA_RANK + MLA_ROPE:o + 2 * MLA_RANK + MLA_ROPE + BR]
        o += 2 * MLA_RANK + MLA_ROPE + BR
        dq = w[:, o:o + BR]
        dk = w[:, o + BR:o + BR + GQA_KV_HEADS * HD]
        dv = w[:, o + BR + GQA_KV_HEADS * HD:o + BR + 2 * GQA_KV_HEADS * HD]
        dg = w[:, o + BR + 2 * GQA_KV_HEADS * HD:]
        ckr_t = jnp.tile(jnp.pad(ckr, ((0, 0), (MLA_NOPE, LANES - MLA_NOPE - MLA_ROPE))), (1, N_HEADS))
        w_a = jnp.concatenate([ag, _pad_heads(aq, N_HEADS, HD), _pad_heads(ak, N_HEADS, HD), av], 1).astype(BF16)
        w_b = jnp.concatenate([bg, bu], 1).astype(BF16)
        w_c = jnp.concatenate([cg, ccq, cckv, ckr_t], 1).astype(BF16)
        w_d = jnp.concatenate([dg, _pad_heads(dq, N_HEADS, HD), _pad_heads(dk, GQA_KV_HEADS, HD), dv], 1).astype(BF16)
        wuq = _pad_heads(mla_wuq[l], N_HEADS, MLA_NOPE + MLA_ROPE).astype(BF16)
        wukv = mla_wukv[l].reshape(MLA_RANK, N_HEADS, MLA_NOPE + HD)
        wuk = _pad_heads(wukv[:, :, :MLA_NOPE].reshape(MLA_RANK, -1), N_HEADS, MLA_NOPE).astype(BF16)
        wuv = wukv[:, :, MLA_NOPE:].reshape(MLA_RANK, -1).T.astype(BF16)

        mod = _modulation(cc, mod_w[l], mod_b[l])
        sh, sc, gt = jnp.split(mod, 3, axis=-1)
        pair = lambda m: jnp.stack([jnp.broadcast_to(m[0], (bsz, d)), m[1:]], axis=1)[:, :, None, :]
        sh2, sc2, gt2 = pair(sh), pair(sc), pair(gt)

        row = lambda t: t.reshape(1, -1)
        norm = (x_all, norm_g[l], sc2, sh2)

        ga, qa, ka, vat, h = _branch_proj(*norm, w_a, n_lat_tiles, True, _na_epilogue,
                                          (_pad_vec(na_qg[l]), _pad_vec(na_kg[l])), (), N_HEADS, "proj_na")
        ya = (_neighbourhood(qa, ka, vat, seq, n_ctx, _na_bias_table(na_rpb[l])),
              _flash(qa, ka, vat, seq, n_ctx, seq, n_ctx, n_ctx, n_ctx))

        pb, = _norm_proj(*norm, w_b, n_lat_tiles, False)
        r, v, nkk, bonus, lw0, kd0, bd0, lw1, kd1, bd1 = _rw_prep(
            pb, row(rw_mu[l]), row(rw_kk[l]), row(rw_ka[l]), row(rw_rk[l]),
            rw_w0[l][:, None, :], rw_w2[l].astype(BF16), rw_a0[l][:, None, :], rw_a2[l].astype(BF16),
            ones_blk, n_lat_tiles)
        ybf = _rw_scan(r, v, nkk, lw0, kd0, bd0, n_ctx, False)
        ybr = _rw_scan(r, v, nkk, lw1, kd1, bd1, n_ctx, True)

        pc, = _norm_proj(*norm, w_c, n_lat_tiles, False)
        gc, qc, kc, vct = _branch_prep(pc, _mla_epilogue,
                                       (row(mla_qa_g[l]), row(mla_kva_g[l]), wuq, wuk, wuv,
                                        _pad_vec(mla_qg[l]), _pad_vec(mla_kg[l])), (mla_cos, mla_sin),
                                       N_HEADS, "mla_prep")
        yc = attend(qc, kc, vct, logit_bound(mla_qg[l], mla_kg[l], MLA_NOPE + MLA_ROPE))

        gd, qd, kd_, vdt = _branch_proj(*norm, w_d, n_lat_tiles, False, _gqa_epilogue,
                                        (_pad_vec(gqa_qg[l]), _pad_vec(gqa_kg[l])), (gqa_cos, gqa_sin),
                                        GQA_KV_HEADS, "proj_gqa")
        yd = attend(qd, kd_, vdt, logit_bound(gqa_qg[l], gqa_kg[l], HD))

        n_out = n if l + 1 < depth else seq
        x_all = _merge(x_all, h, ya, ybf, ybr, yc, yd, ga, pb, gc, gd, bonus,
                       row(rw_gn_w[l]), row(rw_gn_b[l]), ones_blk,
                       mg_w[l].astype(BF16), mg_b[l][:, None, :], w_br[l].astype(BF16), w_out[l].astype(BF16),
                       gt2, n_lat_tiles, n_out)
    return x_all
```

```python
import functools

import numpy as np
import jax
import jax.numpy as jnp
from jax import lax
from jax.experimental import pallas as pl
from jax.experimental.pallas import tpu as pltpu

F32 = jnp.float32
BF16 = jnp.bfloat16
HIGHEST = lax.Precision.HIGHEST

D_MODEL = 1024
GRID_W = 64
BR = 512
HD = 64
N_HEADS = BR // HD
EPS = 1e-6
ROPE_BASE = 10000.0
NA_WIN_R = 8
NA_WIN_C = 16
RW_LORA = 64
RW_GN_EPS = 64e-5
RW_DECAY_SCALE = float(np.exp(-0.5))
MLA_RANK = 256
MLA_NOPE = 64
MLA_ROPE = 32
GQA_KV_HEADS = 2
U_COLS = 3 * BR + 4 * RW_LORA

LANES = 128
F32_SUBLANES = 8
BF16_SUBLANES = 16
V_ROWS = HD + BF16_SUBLANES
VMEM_LIMIT = 56 * 1024 * 1024

TM = 256
CHUNK = 64
RW_TT = 256
RW_PACK = 4
FLASH_QB = 256
FLASH_KB = 256
FLASH_MAX_BOUND = 50.0
LOG2E = 1.4426950408889634
NA_ROWS = 32
NA_QROWS = 4
NA_KROWS = 12
MASKED_LOGIT = -1e30
ROUNDING_MARGIN = 1.02


def _params(*sem):
    return pltpu.CompilerParams(dimension_semantics=sem, vmem_limit_bytes=VMEM_LIMIT)


def _dot(a, b):
    return jnp.dot(a.astype(BF16), b.astype(BF16), preferred_element_type=F32)


def _dot_nt(a, b):
    return lax.dot_general(a.astype(BF16), b.astype(BF16), (((1,), (1,)), ((), ())),
                           preferred_element_type=F32)


def _split3(t):
    hi = t.astype(BF16)
    r1 = t - hi.astype(F32)
    mid = r1.astype(BF16)
    return hi, mid, (r1 - mid.astype(F32)).astype(BF16)


def _dot_exact_rhs(a, b):
    hi, mid, lo = _split3(a)
    return (jnp.dot(hi, b, preferred_element_type=F32) + jnp.dot(mid, b, preferred_element_type=F32)
            + jnp.dot(lo, b, preferred_element_type=F32))


def _silu(t):
    return t / (1.0 + jnp.exp(-t))


def _sigmoid(t):
    return 1.0 / (1.0 + jnp.exp(-t))


def _mod_kernel(c_ref, w_ref, b_ref, o_ref):
    o_ref[...] = _dot(_silu(c_ref[...]), w_ref[...]) + b_ref[...]


def _modulation(cc, w, b):
    n = cc.shape[0]
    return pl.pallas_call(
        _mod_kernel,
        out_shape=jax.ShapeDtypeStruct((n, w.shape[1]), F32),
        compiler_params=_params(),
        name="adaln_mod",
    )(cc, w, b.reshape(1, -1))


def _norm_proj_kernel(x_ref, g_ref, sc_ref, sh_ref, w_ref, *refs, n_extra, want_h, epilogue):
    x = x_ref[0]
    xn = x * lax.rsqrt(jnp.mean(x * x, -1, keepdims=True) + EPS) * g_ref[...]
    h = (xn * (1.0 + sc_ref[0, 0]) + sh_ref[0, 0]).astype(BF16)
    p = jnp.dot(h, w_ref[...], preferred_element_type=F32)
    extra, outs = refs[:n_extra], refs[n_extra:]
    if want_h:
        outs[-1][0] = h
        outs = outs[:-1]
    epilogue(p, extra, outs)


def _store_projection(p, extra, outs):
    outs[0][0] = p


def _norm_proj(x_all, g, sc, sh, w, n_lat_tiles, want_h, epilogue=_store_projection, extra=(), extra_specs=(),
               out_specs=None, out_shape=None, name="norm_proj"):
    b, n, d = x_all.shape
    ncol = w.shape[1]
    kind = lambda i: jnp.where(i < n_lat_tiles, 1, 0)
    if out_specs is None:
        out_shape = [jax.ShapeDtypeStruct((b, n, ncol), F32)]
        out_specs = [pl.BlockSpec((1, TM, ncol), lambda bi, i: (bi, i, 0))]
    out_shape, out_specs = list(out_shape), list(out_specs)
    if want_h:
        out_shape.append(jax.ShapeDtypeStruct((b, n, d), BF16))
        out_specs.append(pl.BlockSpec((1, TM, d), lambda bi, i: (bi, i, 0)))
    return pl.pallas_call(
        functools.partial(_norm_proj_kernel, n_extra=len(extra), want_h=want_h, epilogue=epilogue),
        grid=(b, n // TM),
        in_specs=[
            pl.BlockSpec((1, TM, d), lambda bi, i: (bi, i, 0)),
            pl.BlockSpec((1, d), lambda bi, i: (0, 0)),
            pl.BlockSpec((1, 1, 1, d), lambda bi, i: (bi, kind(i), 0, 0)),
            pl.BlockSpec((1, 1, 1, d), lambda bi, i: (bi, kind(i), 0, 0)),
            pl.BlockSpec((d, ncol), lambda bi, i: (0, 0)),
        ] + list(extra_specs),
        out_specs=out_specs,
        out_shape=out_shape,
        compiler_params=_params("parallel", "parallel"),
        name=name,
    )(x_all, g.reshape(1, d), sc, sh, w, *extra)


def _head_norm(xh, gain, inv_d):
    ms = jnp.sum(xh * xh, -1, keepdims=True) * inv_d
    return xh * lax.rsqrt(ms + EPS) * gain


def _rope(xh, cos, sin, lo, half):
    lane = lax.broadcasted_iota(jnp.int32, xh.shape, 1)
    swapped = jnp.where(lane < lo + half, pltpu.roll(xh, LANES - half, 1), pltpu.roll(xh, half, 1))
    return xh * cos + swapped * sin


def _store_values_t(vt_ref, vt, n_heads):
    lead = lax.broadcasted_iota(jnp.int32, (V_ROWS - HD, vt.shape[1]), 0) == 0
    tail = jnp.where(lead, 1.0, 0.0).astype(BF16)
    for h in range(n_heads):
        vt_ref[0, h, :HD] = vt[h * HD:(h + 1) * HD].astype(BF16)
        vt_ref[0, h, HD:] = tail


def _head_specs(b, n, n_kv):
    hm = lambda bi, i: (bi, 0, i, 0)
    vm = lambda bi, i: (bi, 0, 0, i)
    specs = [pl.BlockSpec((1, TM, BR), lambda bi, i: (bi, i, 0)),
             pl.BlockSpec((1, N_HEADS, TM, LANES), hm),
             pl.BlockSpec((1, n_kv, TM, LANES), hm),
             pl.BlockSpec((1, n_kv, V_ROWS, TM), vm)]
    shapes = [jax.ShapeDtypeStruct((b, n, BR), F32),
              jax.ShapeDtypeStruct((b, N_HEADS, n, LANES), BF16),
              jax.ShapeDtypeStruct((b, n_kv, n, LANES), BF16),
              jax.ShapeDtypeStruct((b, n_kv, V_ROWS, n), BF16)]
    return specs, shapes


def _na_epilogue(p, extra, outs):
    qg_ref, kg_ref = extra
    g_ref, q_ref, k_ref, vt_ref = outs
    qoff, koff, voff = BR, BR + N_HEADS * LANES, BR + 2 * N_HEADS * LANES
    g_ref[0] = p[:, :BR]
    for h in range(N_HEADS):
        qh = p[:, qoff + h * LANES:qoff + (h + 1) * LANES]
        kh = p[:, koff + h * LANES:koff + (h + 1) * LANES]
        q_ref[0, h] = (_head_norm(qh, qg_ref[...], 1.0 / HD) * (HD ** -0.5 * LOG2E)).astype(BF16)
        k_ref[0, h] = _head_norm(kh, kg_ref[...], 1.0 / HD).astype(BF16)
    _store_values_t(vt_ref, p[:, voff:voff + BR].T, N_HEADS)


def _gqa_epilogue(p, extra, outs):
    qg_ref, kg_ref, cos_ref, sin_ref = extra
    g_ref, q_ref, k_ref, vt_ref = outs
    qoff, koff = BR, BR + N_HEADS * LANES
    voff = koff + GQA_KV_HEADS * LANES
    cos, sin = cos_ref[...], sin_ref[...]
    g_ref[0] = p[:, :BR]
    for h in range(N_HEADS):
        qh = _head_norm(p[:, qoff + h * LANES:qoff + (h + 1) * LANES], qg_ref[...], 1.0 / HD)
        q_ref[0, h] = (_rope(qh, cos, sin, 0, HD // 2) * (HD ** -0.5 * LOG2E)).astype(BF16)
    for h in range(GQA_KV_HEADS):
        kh = _head_norm(p[:, koff + h * LANES:koff + (h + 1) * LANES], kg_ref[...], 1.0 / HD)
        k_ref[0, h] = _rope(kh, cos, sin, 0, HD // 2).astype(BF16)
    _store_values_t(vt_ref, p[:, voff:voff + GQA_KV_HEADS * HD].T, GQA_KV_HEADS)


def _mla_epilogue(p, extra, outs):
    qa_ref, kva_ref, wuq_ref, wuk_ref, wuv_ref, qg_ref, kg_ref, cos_ref, sin_ref = extra
    g_ref, q_ref, k_ref, vt_ref = outs
    d_qk = MLA_NOPE + MLA_ROPE
    g_ref[0] = p[:, :BR]
    groups = [slice(0, TM // 2), slice(TM // 2, TM)]
    rms = lambda t, gain: t * lax.rsqrt(jnp.mean(t * t, -1, keepdims=True) + EPS) * gain
    cqn = [rms(p[r, BR:BR + MLA_RANK], qa_ref[...]) for r in groups]
    ckvn = [rms(p[r, BR + MLA_RANK:BR + 2 * MLA_RANK], kva_ref[...]) for r in groups]
    qf = [_dot(t, wuq_ref[...]) for t in cqn]
    kf = [_dot(t, wuk_ref[...]) + p[r, BR + 2 * MLA_RANK:BR + 2 * MLA_RANK + N_HEADS * LANES]
          for t, r in zip(ckvn, groups)]
    vtf = [_dot_nt(wuv_ref[...], t) for t in ckvn]
    cos = [cos_ref[r, :] for r in groups]
    sin = [sin_ref[r, :] for r in groups]
    for h in range(N_HEADS):
        qh = [_head_norm(t[:, h * LANES:(h + 1) * LANES], qg_ref[...], 1.0 / d_qk) for t in qf]
        kh = [_head_norm(t[:, h * LANES:(h + 1) * LANES], kg_ref[...], 1.0 / d_qk) for t in kf]
        for r, x, y, c, s in zip(groups, qh, kh, cos, sin):
            q_ref[0, h, r] = (_rope(x, c, s, MLA_NOPE, MLA_ROPE // 2) * (d_qk ** -0.5 * LOG2E)).astype(BF16)
            k_ref[0, h, r] = _rope(y, c, s, MLA_NOPE, MLA_ROPE // 2).astype(BF16)
    _store_values_t(vt_ref, jnp.concatenate(vtf, axis=1), N_HEADS)


def _prep_kernel(p_ref, *refs, n_extra, epilogue):
    epilogue(p_ref[0], refs[:n_extra], refs[n_extra:])


def _branch_prep(p, epilogue, consts, tables, n_kv, name):
    b, n, ncol = p.shape
    full = lambda a: pl.BlockSpec(a.shape, lambda bi, i: (0,) * a.ndim)
    tab = pl.BlockSpec((TM, LANES), lambda bi, i: (i, 0))
    specs, shapes = _head_specs(b, n, n_kv)
    return pl.pallas_call(
        functools.partial(_prep_kernel, n_extra=len(consts) + len(tables), epilogue=epilogue),
        grid=(b, n // TM),
        in_specs=[pl.BlockSpec((1, TM, ncol), lambda bi, i: (bi, i, 0))]
        + [full(a) for a in consts] + [tab] * len(tables),
        out_specs=specs,
        out_shape=shapes,
        compiler_params=_params("parallel", "parallel"),
        name=name,
    )(p, *consts, *tables)


def _branch_proj(x_all, g, sc, sh, w, n_lat_tiles, want_h, epilogue, consts, tables, n_kv, name):
    b, n, _ = x_all.shape
    full = lambda a: pl.BlockSpec(a.shape, lambda bi, i: (0,) * a.ndim)
    tab = pl.BlockSpec((TM, LANES), lambda bi, i: (i, 0))
    specs, shapes = _head_specs(b, n, n_kv)
    return _norm_proj(x_all, g, sc, sh, w, n_lat_tiles, want_h, epilogue=epilogue,
                      extra=tuple(consts) + tuple(tables),
                      extra_specs=[full(a) for a in consts] + [tab] * len(tables),
                      out_specs=specs, out_shape=shapes, name=name)


def _flash_kernel(q_ref, k_ref, vt_ref, *rest, tk, nk, nq, kb, bounded):
    if bounded:
        bound_ref, o_ref, m_sc, acc_sc, s_sc = rest
        m_sc[...] = jnp.broadcast_to(bound_ref[...], m_sc.shape)
    else:
        o_ref, m_sc, acc_sc, s_sc = rest
        m_sc[...] = jnp.full(m_sc.shape, -jnp.inf, F32)
    acc_sc[...] = jnp.zeros(acc_sc.shape, F32)

    n_sub = tk // kb
    chains = range(nq)

    def scores(row0):
        k = k_ref[0, 0, pl.ds(row0, kb), :]
        return [lax.dot_general(k, q_ref[0, 0, c * FLASH_QB:(c + 1) * FLASH_QB, :],
                                (((1,), (1,)), ((), ())), preferred_element_type=F32) for c in chains]

    for c, s0 in zip(chains, scores(0)):
        s_sc[c] = s0

    def body(j, carry):
        off = pl.multiple_of(j * tk, tk)
        nxt = pl.multiple_of(jnp.minimum(j + 1, nk - 1) * tk, tk)
        m = [m_sc[c] for c in chains]
        acc = [acc_sc[c] for c in chains]
        s = [s_sc[c] for c in chains]
        for u in range(n_sub):
            k_next = k_ref[0, 0, pl.ds(off + (u + 1) * kb if u + 1 < n_sub else nxt, kb), :]
            vt = vt_ref[0, 0, :, pl.ds(off + u * kb, kb)]
            s_next = []
            for c in chains:
                s_next.append(lax.dot_general(k_next, q_ref[0, 0, c * FLASH_QB:(c + 1) * FLASH_QB, :],
                                              (((1,), (1,)), ((), ())), preferred_element_type=F32))
                if bounded:
                    p = jnp.exp2(s[c] - m[c])
                    acc[c] = acc[c] + jnp.dot(vt, p.astype(BF16), preferred_element_type=F32)
                    continue
                m_new = jnp.maximum(m[c], jnp.max(s[c], 0, keepdims=True))
                alpha = jnp.exp2(m[c] - m_new)
                p = jnp.exp2(s[c] - m_new)
                acc[c] = alpha * acc[c] + jnp.dot(vt, p.astype(BF16), preferred_element_type=F32)
                m[c] = m_new
            s = s_next
        for c in chains:
            m_sc[c], acc_sc[c], s_sc[c] = m[c], acc[c], s[c]
        return carry

    lax.fori_loop(0, nk, body, 0)
    for c in range(nq):
        o_ref[0, 0, :, c * FLASH_QB:(c + 1) * FLASH_QB] = acc_sc[c, :HD] / acc_sc[c, HD:HD + 1]


def _flash(q, k, vt, q0, n_q, k0, n_keys, tq, tk, bound=None):
    b, hq = q.shape[:2]
    hk = k.shape[1]
    rep = hq // hk
    assert n_q % tq == 0 and n_keys % tk == 0 and tq % FLASH_QB == 0
    kb = min(FLASH_KB, tk)
    assert tk % kb == 0
    assert q0 % tq == 0 and k0 % n_keys == 0
    hv = vt.shape[2]
    nq = tq // FLASH_QB
    qb0, kb0 = q0 // tq, k0 // n_keys

    def call(bounded):
        extra_specs = [pl.BlockSpec((1, FLASH_QB), lambda bi, h, i: (0, 0))] if bounded else []
        extra = [jnp.full((1, FLASH_QB), bound, F32)] if bounded else []
        return pl.pallas_call(
            functools.partial(_flash_kernel, tk=tk, nk=n_keys // tk, nq=nq, kb=kb, bounded=bounded),
            grid=(b, hq, n_q // tq),
            in_specs=[pl.BlockSpec((1, 1, tq, LANES), lambda bi, h, i: (bi, h, qb0 + i, 0)),
                      pl.BlockSpec((1, 1, n_keys, LANES), lambda bi, h, i: (bi, h // rep, kb0, 0)),
                      pl.BlockSpec((1, 1, hv, n_keys), lambda bi, h, i: (bi, h // rep, 0, kb0))] + extra_specs,
            out_specs=pl.BlockSpec((1, 1, HD, tq), lambda bi, h, i: (bi, h, 0, i)),
            out_shape=jax.ShapeDtypeStruct((b, hq, HD, n_q), F32),
            scratch_shapes=[pltpu.VMEM((nq, 1, FLASH_QB), F32), pltpu.VMEM((nq, hv, FLASH_QB), F32),
                            pltpu.VMEM((nq, kb, FLASH_QB), F32)],
            compiler_params=_params("parallel", "parallel", "arbitrary"),
            name="flash_attention_bounded" if bounded else "flash_attention",
        )(q, k, vt, *extra)

    if bound is None:
        return call(False)
    return lax.cond(bound <= FLASH_MAX_BOUND, lambda: call(True), lambda: call(False))


def _na_kernel(q_ref, k_ref, vt_ref, kc_ref, vtc_ref, bias_ref, o_ref, *, rows):
    kc = kc_ref[0, 0]
    vtc = vtc_ref[0, 0]
    nq, nk = NA_QROWS * GRID_W, NA_KROWS * GRID_W
    batches = range(NA_ROWS // NA_QROWS)
    r0 = [pl.program_id(2) * NA_ROWS + bi * NA_QROWS for bi in batches]
    koff = [pl.multiple_of(jnp.clip(r - NA_WIN_R // 2, 0, rows - NA_KROWS) * GRID_W, 2 * GRID_W) for r in r0]
    variant = [jnp.where(r == 0, 0, jnp.where(r == rows - NA_QROWS, 2, 1)) for r in r0]
    q = [q_ref[0, 0, bi * nq:(bi + 1) * nq, :] for bi in batches]
    nt = (((1,), (1,)), ((), ()))

    def scores(bi):
        s_w = lax.dot_general(k_ref[0, 0, pl.ds(koff[bi], nk), :], q[bi], nt, preferred_element_type=F32)
        return s_w + bias_ref[variant[bi], 0], lax.dot_general(kc, q[bi], nt, preferred_element_type=F32)

    s_next = scores(0)
    for bi in batches:
        s_w, s_c = s_next
        if bi + 1 < len(batches):
            s_next = scores(bi + 1)
        m = jnp.maximum(jnp.max(s_w, 0, keepdims=True), jnp.max(s_c, 0, keepdims=True))
        p_w = jnp.exp2(s_w - m).astype(BF16)
        p_c = jnp.exp2(s_c - m).astype(BF16)
        acc = (jnp.dot(vt_ref[0, 0, :, pl.ds(koff[bi], nk)], p_w, preferred_element_type=F32)
               + jnp.dot(vtc, p_c, preferred_element_type=F32))
        o_ref[0, 0, :, bi * nq:(bi + 1) * nq] = acc[:HD] / acc[HD:HD + 1]


def _na_bias_table(rpb):
    n_dr, n_dc = 2 * NA_WIN_R - 1, 2 * NA_WIN_C - 1
    sel_r = np.zeros((3, NA_KROWS, NA_QROWS, n_dr), np.float32)
    ok_r = np.zeros((3, NA_KROWS, NA_QROWS), bool)
    for kind in range(3):
        for i in range(NA_QROWS):
            start = (0, i, NA_KROWS - NA_WIN_R)[kind]
            shift = (0, -(NA_WIN_R // 2), NA_QROWS - NA_KROWS)[kind]
            for j in range(start, start + NA_WIN_R):
                sel_r[kind, j, i, j + shift - i + NA_WIN_R - 1] = 1.0
                ok_r[kind, j, i] = True
    cols = np.arange(GRID_W)
    cs = np.clip(cols - NA_WIN_C // 2, 0, GRID_W - NA_WIN_C)
    sel_c = np.zeros((GRID_W, GRID_W, n_dc), np.float32)
    ok_c = np.zeros((GRID_W, GRID_W), bool)
    for qc in range(GRID_W):
        for kcol in range(cs[qc], cs[qc] + NA_WIN_C):
            sel_c[kcol, qc, kcol - qc + NA_WIN_C - 1] = 1.0
            ok_c[kcol, qc] = True
    tab = jnp.einsum('hrc,vjir,kqc->vhjkiq', rpb, sel_r, sel_c, precision=HIGHEST)
    ok = ok_r[:, None, :, None, :, None] & ok_c[None, None, None, :, None, :]
    tab = jnp.where(ok, tab * LOG2E, MASKED_LOGIT)
    return tab.reshape(3, rpb.shape[0], NA_KROWS * GRID_W, NA_QROWS * GRID_W)


def _neighbourhood(q, k, vt, seq, n_ctx, bias):
    b, h = q.shape[:2]
    rows = seq // GRID_W
    assert rows >= NA_KROWS and rows % NA_ROWS == 0 and seq % n_ctx == 0
    tile = NA_ROWS * GRID_W
    hv = vt.shape[2]
    cb = seq // n_ctx
    return pl.pallas_call(
        functools.partial(_na_kernel, rows=rows),
        grid=(b, h, rows // NA_ROWS),
        in_specs=[pl.BlockSpec((1, 1, tile, LANES), lambda bi, hi, i: (bi, hi, i, 0)),
                  pl.BlockSpec((1, 1, seq, LANES), lambda bi, hi, i: (bi, hi, 0, 0)),
                  pl.BlockSpec((1, 1, hv, seq), lambda bi, hi, i: (bi, hi, 0, 0)),
                  pl.BlockSpec((1, 1, n_ctx, LANES), lambda bi, hi, i: (bi, hi, cb, 0)),
                  pl.BlockSpec((1, 1, hv, n_ctx), lambda bi, hi, i: (bi, hi, 0, cb)),
                  pl.BlockSpec((3, 1, NA_KROWS * GRID_W, NA_QROWS * GRID_W), lambda bi, hi, i: (0, hi, 0, 0))],
        out_specs=pl.BlockSpec((1, 1, HD, tile), lambda bi, hi, i: (bi, hi, 0, i)),
        out_shape=jax.ShapeDtypeStruct((b, h, HD, seq), F32),
        compiler_params=_params("parallel", "parallel", "arbitrary"),
        name="neighbourhood_attention",
    )(q, k, vt, k, vt, bias)


def _seg_sum(t, ones_blk):
    return _dot_exact_rhs(t, ones_blk)


def _rw_prep_kernel(p_ref, prev_ref, next_ref, mu_ref, kk_ref, ka_ref, rk_ref, w0_ref, w2_ref, a0_ref,
                    a2_ref, ones_ref, r_o, v_o, nkk_o, bonus_o, lw0_o, kd0_o, bd0_o, lw1_o, kd1_o, bd1_o,
                    *, n_lat_tiles, n_tiles):
    lw_o, kd_o, bd_o = (lw0_o, lw1_o), (kd0_o, kd1_o), (bd0_o, bd1_o)
    i = pl.program_id(1)
    u = p_ref[0, :, BR:]
    first = jnp.logical_or(i == 0, i == n_lat_tiles)
    last = jnp.logical_or(i == n_lat_tiles - 1, i == n_tiles - 1)
    prev_row = jnp.where(first, 0.0, prev_ref[0, F32_SUBLANES - 1:F32_SUBLANES, BR:])
    next_row = jnp.where(last, 0.0, next_ref[0, 0:1, BR:])
    row = lax.broadcasted_iota(jnp.int32, u.shape, 0)
    up = jnp.where(row == 0, prev_row, pltpu.roll(u, 1, 0))
    dn = jnp.where(row == TM - 1, next_row, pltpu.roll(u, TM - 1, 0))
    u = u + mu_ref[...] * (0.5 * (up + dn) - u)
    r, k, v = u[:, :BR], u[:, BR:2 * BR], u[:, 2 * BR:3 * BR]
    wl, al = u[:, 3 * BR:3 * BR + 2 * RW_LORA], u[:, 3 * BR + 2 * RW_LORA:]
    ones_blk = ones_ref[...]
    kk = k * kk_ref[...]
    kk = kk * lax.rsqrt(_seg_sum(kk * kk, ones_blk) + 1e-12)
    r_o[0] = r
    v_o[0] = v
    nkk_o[0] = -kk
    k_sum = jnp.zeros_like(k)
    for d in range(2):
        wl_d = wl[:, d * RW_LORA:(d + 1) * RW_LORA]
        al_d = al[:, d * RW_LORA:(d + 1) * RW_LORA]
        t = w0_ref[d] + _dot(jnp.tanh(wl_d), w2_ref[d])
        lw_o[d][0] = -RW_DECAY_SCALE * _sigmoid(t)
        a = _sigmoid(a0_ref[d] + _dot(al_d, a2_ref[d]))
        k_d = k * (1.0 + (a - 1.0) * ka_ref[...])
        kd_o[d][0] = k_d
        bd_o[d][0] = kk * a
        k_sum = k_sum + k_d
    bonus_o[0] = _seg_sum(r * k_sum * rk_ref[...], ones_blk) * v


def _rw_prep(pb, mu, kk, ka, rk, w0, w2, a0, a2, ones_blk, n_lat_tiles):
    b, n, ncol = pb.shape
    nt = n // TM
    r8 = TM // F32_SUBLANES
    full = lambda a: pl.BlockSpec(a.shape, lambda bi, i: (0,) * a.ndim)
    tok = pl.BlockSpec((1, TM, BR), lambda bi, i: (bi, i, 0))
    one = jax.ShapeDtypeStruct((b, n, BR), F32)
    return pl.pallas_call(
        functools.partial(_rw_prep_kernel, n_lat_tiles=n_lat_tiles, n_tiles=nt),
        grid=(b, nt),
        in_specs=[pl.BlockSpec((1, TM, ncol), lambda bi, i: (bi, i, 0)),
                  pl.BlockSpec((1, F32_SUBLANES, ncol), lambda bi, i: (bi, jnp.maximum(i * r8 - 1, 0), 0)),
                  pl.BlockSpec((1, F32_SUBLANES, ncol),
                               lambda bi, i: (bi, jnp.minimum((i + 1) * r8, n // F32_SUBLANES - 1), 0)),
                  full(mu), full(kk), full(ka), full(rk), full(w0), full(w2), full(a0), full(a2),
                  full(ones_blk)],
        out_specs=[tok] * 10,
        out_shape=[one] * 10,
        compiler_params=_params("parallel", "parallel"),
        name="rwkv_prep",
    )(pb, pb, pb, mu, kk, ka, rk, w0, w2, a0, a2, ones_blk)


def _rw_scan_kernel(r_ref, v_ref, a_ref, lw_ref, k_ref, b_ref, y_ref, st_ref, *, rev, n_b):
    c, w = CHUNK, RW_PACK * HD

    @pl.when(pl.program_id(0) == 0)
    def _():
        st_ref[...] = jnp.zeros(st_ref.shape, F32)

    tt = lax.broadcasted_iota(jnp.int32, (c, w), 0)
    ss = lax.broadcasted_iota(jnp.int32, (c, w), 1) & (c - 1)
    strict = (ss > tt) if rev else (ss < tt)
    incl = (ss >= tt) if rev else (ss <= tt)
    eye = (ss == tt).astype(F32)
    t2 = lax.broadcasted_iota(jnp.int32, (c, c), 0)
    s2 = lax.broadcasted_iota(jnp.int32, (c, c), 1)
    m_incl = ((s2 >= t2) if rev else (s2 <= t2)).astype(BF16)
    hd_log2 = HD.bit_length() - 1
    blk = ((lax.broadcasted_iota(jnp.int32, (w, w), 0) >> hd_log2)
           == (lax.broadcasted_iota(jnp.int32, (w, w), 1) >> hd_log2))

    def bd(t):
        tb = t.astype(BF16)
        return jnp.where(blk, jnp.concatenate([tb] * RW_PACK, axis=0), jnp.zeros((), BF16))

    def dtn(x, y):
        return lax.dot_general(x.astype(BF16), y.astype(BF16), (((0,), (0,)), ((), ())),
                               preferred_element_type=F32)

    chunks = tuple(range(RW_TT // c))
    chunks = chunks[::-1] if rev else chunks
    chains = [(bi, g) for bi in range(n_b) for g in range(BR // w)]
    units = [(ci, bi, g) for ci in chunks for (bi, g) in chains]
    sl = lambda ref, u: ref[u[1], u[0] * c:(u[0] + 1) * c, u[2] * w:(u[2] + 1) * w]

    lw = [sl(lw_ref, u) for u in units]
    cum = []
    for t in lw:
        hi, mid, lo = _split3(t)
        cum.append(jnp.dot(m_incl, hi, preferred_element_type=F32)
                   + jnp.dot(m_incl, mid, preferred_element_type=F32)
                   + jnp.dot(m_incl, lo, preferred_element_type=F32))
    tot = [jnp.sum(t, 0, keepdims=True) for t in lw]
    p_inv = [jnp.exp(-x) for x in cum]
    p_end = [jnp.exp(t - x) for t, x in zip(tot, cum)]
    a_t = [sl(a_ref, u) * jnp.exp(x - l) for u, x, l in zip(units, cum, lw)]
    r_t = [sl(r_ref, u) * jnp.exp(x) for u, x in zip(units, cum)]
    b_raw = [sl(b_ref, u) for u in units]
    k_raw = [sl(k_ref, u) for u in units]
    v = [sl(v_ref, u) for u in units]
    b_t = [x * p for x, p in zip(b_raw, p_inv)]
    k_t = [x * p for x, p in zip(k_raw, p_inv)]
    b_h = [x * p for x, p in zip(b_raw, p_end)]
    k_h = [x * p for x, p in zip(k_raw, p_end)]
    ar = [jnp.concatenate([x, y], axis=0) for x, y in zip(a_t, r_t)]
    g_b = [_dot_nt(x, bd(y)) for x, y in zip(ar, b_t)]
    g_k = [_dot_nt(x, bd(y)) for x, y in zip(ar, k_t)]
    lab = [jnp.where(strict, x[:c], 0.0) for x in g_b]
    lak = [jnp.where(strict, x[:c], 0.0) for x in g_k]
    qrb = [jnp.where(incl, x[c:], 0.0) for x in g_b]
    qrk = [jnp.where(incl, x[c:], 0.0) for x in g_k]
    tinv = [eye + x for x in lab]
    lp = lab
    stack = lambda x, y: jnp.concatenate([x, y], axis=0)
    for i in range(6):
        lp_bd = [bd(x) for x in lp]
        if i == 0:
            lp = [_dot(x, y) for x, y in zip(lp, lp_bd)]
        elif i < 5:
            both = [_dot(stack(t, x), y) for t, x, y in zip(tinv, lp, lp_bd)]
            tinv = [t + z[:c] for t, z in zip(tinv, both)]
            lp = [z[c:] for z in both]
        else:
            tinv = [t + _dot(t, y) for t, y in zip(tinv, lp_bd)]
    both = [_dot(stack(x, y), bd(z)) for x, y, z in zip(lak, qrk, v)]
    wv = [z[:c] for z in both]
    y_loc = [z[c:] for z in both]
    a_hat = [_dot(x, bd(y)) for x, y in zip(tinv, a_t)]
    u_hat = [_dot(x, bd(y)) for x, y in zip(tinv, wv)]
    ar_hat = [stack(x, y) for x, y in zip(a_hat, r_t)]
    bk_h = [stack(x, y) for x, y in zip(b_h, k_h)]
    p_c = [jnp.exp(t) for t in tot]

    n_ch = len(chains)
    for j in range(len(chunks)):
        idx = range(j * n_ch, (j + 1) * n_ch)
        st = [st_ref[q] for q in range(n_ch)]
        ur = [_dot_nt(ar_hat[i], s) for i, s in zip(idx, st)]
        u = [z[:c] + u_hat[i] for i, z in zip(idx, ur)]
        y = [z[c:] + _dot(qrb[i], bd(x)) + y_loc[i] for i, z, x in zip(idx, ur, u)]
        new = [s * p_c[i] + jnp.where(blk, dtn(stack(x, v[i]), bk_h[i]), 0.0) for i, s, x in zip(idx, st, u)]
        for q, i in enumerate(idx):
            y_ref[units[i][1], units[i][0] * c:(units[i][0] + 1) * c, units[i][2] * w:(units[i][2] + 1) * w] = y[q]
            st_ref[q] = new[q]


def _rw_scan(r, v, nkk, lw, kd, bd, n_ctx, rev):
    b, n, _ = r.shape
    nblk = n // RW_TT
    nctx = n_ctx // RW_TT
    if rev:
        blk_of = lambda i: nblk - 1 - i
    else:
        blk_of = lambda i: jnp.where(i < nctx, nblk - nctx + i, i - nctx)
    spec = pl.BlockSpec((b, RW_TT, BR), lambda i: (0, blk_of(i), 0))
    n_chains = b * (BR // (RW_PACK * HD))
    return pl.pallas_call(
        functools.partial(_rw_scan_kernel, rev=rev, n_b=b),
        grid=(nblk,),
        in_specs=[spec] * 6,
        out_specs=spec,
        out_shape=jax.ShapeDtypeStruct((b, n, BR), F32),
        scratch_shapes=[pltpu.VMEM((n_chains, RW_PACK * HD, RW_PACK * HD), F32)],
        compiler_params=_params("arbitrary"),
        name="rwkv_scan",
    )(r, v, nkk, lw, kd, bd)


def _merge_kernel(x_ref, h_ref, ya_ref, yac_ref, ybf_ref, ybr_ref, yc_ref, ycc_ref, yd_ref, ydc_ref,
                  ga_ref, gb_ref, gc_ref, gd_ref, bonus_ref, gnw_ref, gnb_ref, ones_ref, mgw_ref, mgb_ref,
                  wbr_ref, wout_ref, gt_ref, o_ref, *, n_lat_tiles):
    h = h_ref[0]
    ones_blk = ones_ref[...]
    is_ctx = pl.program_id(1) >= n_lat_tiles

    def token_major(lat_ref, ctx_ref):
        yt = jnp.where(is_ctx, ctx_ref[0], lat_ref[0])
        return yt.reshape(BR, TM).T

    yb = ybf_ref[0] + ybr_ref[0]
    mean = _seg_sum(yb, ones_blk) * (1.0 / HD)
    cen = yb - mean
    var = _seg_sum(cen * cen, ones_blk) * (1.0 / HD)
    yb = cen * lax.rsqrt(var + RW_GN_EPS) * gnw_ref[...] + gnb_ref[...] + bonus_ref[0]
    ys = (token_major(ya_ref, yac_ref), yb, token_major(yc_ref, ycc_ref), token_major(yd_ref, ydc_ref))
    gs = (ga_ref[0], gb_ref[0], gc_ref[0], gd_ref[0])
    acc = None
    for i in range(4):
        gate = _sigmoid(jnp.dot(h, mgw_ref[i], preferred_element_type=F32) + mgb_ref[i])
        term = gate * _dot(ys[i] * _silu(gs[i]), wbr_ref[i])
        acc = term if acc is None else acc + term
    o_ref[0] = x_ref[0] + gt_ref[0, 0] * _dot(acc, wout_ref[...])


def _merge(x_all, h, ya, ybf, ybr, yc, yd, pa, pb, pc, pd, bonus, gnw, gnb, ones_blk, mgw, mgb, wbr, wout, gt,
           n_lat_tiles, n_out):
    b, n, d = x_all.shape
    kind = lambda i: jnp.where(i < n_lat_tiles, 1, 0)
    tok = lambda w: pl.BlockSpec((1, TM, w), lambda bi, i: (bi, i, 0))
    full = lambda a: pl.BlockSpec(a.shape, lambda bi, i: (0,) * a.ndim)
    lat = pl.BlockSpec((1, N_HEADS, HD, TM), lambda bi, i: (bi, 0, 0, jnp.minimum(i, n_lat_tiles - 1)))
    ctx = pl.BlockSpec((1, N_HEADS, HD, TM), lambda bi, i: (bi, 0, 0, 0))
    return pl.pallas_call(
        functools.partial(_merge_kernel, n_lat_tiles=n_lat_tiles),
        grid=(b, n_out // TM),
        in_specs=[tok(d), tok(d), lat, ctx, tok(BR), tok(BR), lat, ctx, lat, ctx,
                  tok(BR), tok(BR), tok(BR), tok(BR), tok(BR),
                  full(gnw), full(gnb), full(ones_blk), full(mgw), full(mgb), full(wbr), full(wout),
                  pl.BlockSpec((1, 1, 1, d), lambda bi, i: (bi, kind(i), 0, 0))],
        out_specs=tok(d),
        out_shape=jax.ShapeDtypeStruct((b, n_out, d), F32),
        compiler_params=_params("parallel", "parallel"),
        name="merge_out",
    )(x_all, h, ya[0], ya[1], ybf, ybr, yc[0], yc[1], yd[0], yd[1], pa, pb, pc, pd, bonus,
      gnw, gnb, ones_blk, mgw, mgb, wbr, wout, gt)


def _pad_heads(w, n_heads, d):
    lead = w.shape[:-1]
    w = w.reshape(lead + (n_heads, d))
    w = jnp.pad(w, [(0, 0)] * len(lead) + [(0, 0), (0, LANES - d)])
    return w.reshape(lead + (n_heads * LANES,))


def _pad_vec(g):
    return jnp.pad(g, (0, LANES - g.shape[0])).reshape(1, LANES)


def _rope_tables(n_lat, n_ctx, d_rot, lo):
    t = np.arange(n_lat)
    row = (t // GRID_W).astype(np.float32)
    col = (t % GRID_W).astype(np.float32)
    n_freq = d_rot // 4
    inv = np.float32(ROPE_BASE) ** (-np.arange(n_freq, dtype=np.float32) / np.float32(n_freq))
    ang = np.concatenate([row[:, None] * inv, col[:, None] * inv], -1).astype(np.float32)
    cos, sin = np.cos(ang), np.sin(ang)
    half = d_rot // 2
    cos_t = np.ones((n_lat + n_ctx, LANES), np.float32)
    sin_t = np.zeros((n_lat + n_ctx, LANES), np.float32)
    cos_t[:n_lat, lo:lo + half] = cos
    cos_t[:n_lat, lo + half:lo + d_rot] = cos
    sin_t[:n_lat, lo:lo + half] = -sin
    sin_t[:n_lat, lo + half:lo + d_rot] = sin
    return jnp.asarray(cos_t), jnp.asarray(sin_t)


def _pick_tile(n, candidates):
    for c in candidates:
        if n % c == 0:
            return c
    raise ValueError(f"no tile for {n}")


def kernel(x, c, ctx, c_ctx, norm_g, mod_w, mod_b, w_in, na_qg, na_kg, na_rpb, rw_mu, rw_w0, rw_w2, rw_a0, rw_a2, rw_kk, rw_ka, rw_rk, rw_gn_w, rw_gn_b, mla_qa_g, mla_kva_g, mla_wuq, mla_wukv, mla_qg, mla_kg, gqa_qg, gqa_kg, mg_w, mg_b, w_br, w_out):
    bsz, seq, d = x.shape
    n_ctx = ctx.shape[1]
    depth = w_in.shape[0]
    assert d == D_MODEL and n_ctx == TM and seq % (NA_ROWS * GRID_W) == 0
    n = seq + n_ctx
    n_lat_tiles = seq // TM

    gqa_cos, gqa_sin = _rope_tables(seq, n_ctx, HD, 0)
    mla_cos, mla_sin = _rope_tables(seq, n_ctx, MLA_ROPE, MLA_NOPE)
    ones_blk = jnp.asarray(np.kron(np.eye(N_HEADS), np.ones((HD, HD))), BF16)
    tq = _pick_tile(seq, (2048, 1024, 512, 256))
    tk = _pick_tile(n, (3328, 1280, 256))

    def attend(q, k, vt, bound):
        return (_flash(q, k, vt, 0, seq, 0, n, tq, tk, bound),
                _flash(q, k, vt, seq, n_ctx, seq, n_ctx, n_ctx, n_ctx))

    def logit_bound(q_gain, k_gain, d_head):
        return (d_head ** 0.5 * LOG2E * ROUNDING_MARGIN) * jnp.max(jnp.abs(q_gain)) * jnp.max(jnp.abs(k_gain))

    x_all = jnp.concatenate([x, ctx], axis=1)
    cc = jnp.concatenate([c_ctx[None], c], axis=0)
    for l in range(depth):
        w = w_in[l]
        o = 0
        aq, ak, av, ag = (w[:, o + i * BR:o + (i + 1) * BR] for i in range(4))
        o += 4 * BR
        bu, bg = w[:, o:o + U_COLS], w[:, o + U_COLS:o + U_COLS + BR]
        o += U_COLS + BR
        ccq, cckv = w[:, o:o + MLA_RANK], w[:, o + MLA_RANK:o + 2 * MLA_RANK]
        ckr = w[:, o + 2 * MLA_RANK:o + 2 * MLA_RANK + MLA_ROPE]
        cg = w[:, o + 2 * MLA_RANK + MLA_ROPE:o + 2 * MLA_RANK + MLA_ROPE + BR]
        o += 2 * MLA_RANK + MLA_ROPE + BR
        dq = w[:, o:o + BR]
        dk = w[:, o + BR:o + BR + GQA_KV_HEADS * HD]
        dv = w[:, o + BR + GQA_KV_HEADS * HD:o + BR + 2 * GQA_KV_HEADS * HD]
        dg = w[:, o + BR + 2 * GQA_KV_HEADS * HD:]
        ckr_t = jnp.tile(jnp.pad(ckr, ((0, 0), (MLA_NOPE, LANES - MLA_NOPE - MLA_ROPE))), (1, N_HEADS))
        w_a = jnp.concatenate([ag, _pad_heads(aq, N_HEADS, HD), _pad_heads(ak, N_HEADS, HD), av], 1).astype(BF16)
        w_b = jnp.concatenate([bg, bu], 1).astype(BF16)
        w_c = jnp.concatenate([cg, ccq, cckv, ckr_t], 1).astype(BF16)
        w_d = jnp.concatenate([dg, _pad_heads(dq, N_HEADS, HD), _pad_heads(dk, GQA_KV_HEADS, HD), dv], 1).astype(BF16)
        wuq = _pad_heads(mla_wuq[l], N_HEADS, MLA_NOPE + MLA_ROPE).astype(BF16)
        wukv = mla_wukv[l].reshape(MLA_RANK, N_HEADS, MLA_NOPE + HD)
        wuk = _pad_heads(wukv[:, :, :MLA_NOPE].reshape(MLA_RANK, -1), N_HEADS, MLA_NOPE).astype(BF16)
        wuv = wukv[:, :, MLA_NOPE:].reshape(MLA_RANK, -1).T.astype(BF16)

        mod = _modulation(cc, mod_w[l], mod_b[l])
        sh, sc, gt = jnp.split(mod, 3, axis=-1)
        pair = lambda m: jnp.stack([jnp.broadcast_to(m[0], (bsz, d)), m[1:]], axis=1)[:, :, None, :]
        sh2, sc2, gt2 = pair(sh), pair(sc), pair(gt)

        row = lambda t: t.reshape(1, -1)
        norm = (x_all, norm_g[l], sc2, sh2)

        ga, qa, ka, vat, h = _branch_proj(*norm, w_a, n_lat_tiles, True, _na_epilogue,
                                          (_pad_vec(na_qg[l]), _pad_vec(na_kg[l])), (), N_HEADS, "proj_na")
        ya = (_neighbourhood(qa, ka, vat, seq, n_ctx, _na_bias_table(na_rpb[l])),
              _flash(qa, ka, vat, seq, n_ctx, seq, n_ctx, n_ctx, n_ctx))

        pb, = _norm_proj(*norm, w_b, n_lat_tiles, False)
        r, v, nkk, bonus, lw0, kd0, bd0, lw1, kd1, bd1 = _rw_prep(
            pb, row(rw_mu[l]), row(rw_kk[l]), row(rw_ka[l]), row(rw_rk[l]),
            rw_w0[l][:, None, :], rw_w2[l].astype(BF16), rw_a0[l][:, None, :], rw_a2[l].astype(BF16),
            ones_blk, n_lat_tiles)
        ybf = _rw_scan(r, v, nkk, lw0, kd0, bd0, n_ctx, False)
        ybr = _rw_scan(r, v, nkk, lw1, kd1, bd1, n_ctx, True)

        pc, = _norm_proj(*norm, w_c, n_lat_tiles, False)
        gc, qc, kc, vct = _branch_prep(pc, _mla_epilogue,
                                       (row(mla_qa_g[l]), row(mla_kva_g[l]), wuq, wuk, wuv,
                                        _pad_vec(mla_qg[l]), _pad_vec(mla_kg[l])), (mla_cos, mla_sin),
                                       N_HEADS, "mla_prep")
        yc = attend(qc, kc, vct, logit_bound(mla_qg[l], mla_kg[l], MLA_NOPE + MLA_ROPE))

        gd, qd, kd_, vdt = _branch_proj(*norm, w_d, n_lat_tiles, False, _gqa_epilogue,
                                        (_pad_vec(gqa_qg[l]), _pad_vec(gqa_kg[l])), (gqa_cos, gqa_sin),
                                        GQA_KV_HEADS, "proj_gqa")
        yd = attend(qd, kd_, vdt, logit_bound(gqa_qg[l], gqa_kg[l], HD))

        n_out = n if l + 1 < depth else seq
        x_all = _merge(x_all, h, ya, ybf, ybr, yc, yd, ga, pb, gc, gd, bonus,
                       row(rw_gn_w[l]), row(rw_gn_b[l]), ones_blk,
                       mg_w[l].astype(BF16), mg_b[l][:, None, :], w_br[l].astype(BF16), w_out[l].astype(BF16),
                       gt2, n_lat_tiles, n_out)
    return x_all
```

```python
import functools

import numpy as np
import jax
import jax.numpy as jnp
from jax import lax
from jax.experimental import pallas as pl
from jax.experimental.pallas import tpu as pltpu

F32 = jnp.float32
BF16 = jnp.bfloat16
HIGHEST = lax.Precision.HIGHEST

D_MODEL = 1024
GRID_W = 64
BR = 512
HD = 64
N_HEADS = BR // HD
EPS = 1e-6
ROPE_BASE = 10000.0
NA_WIN_R = 8
NA_WIN_C = 16
RW_LORA = 64
RW_GN_EPS = 64e-5
RW_DECAY_SCALE = float(np.exp(-0.5))
MLA_RANK = 256
MLA_NOPE = 64
MLA_ROPE = 32
GQA_KV_HEADS = 2
U_COLS = 3 * BR + 4 * RW_LORA

LANES = 128
F32_SUBLANES = 8
BF16_SUBLANES = 16
V_ROWS = HD + BF16_SUBLANES
VMEM_LIMIT = 56 * 1024 * 1024

TM = 256
CHUNK = 64
RW_TT = 256
RW_PACK = 4
FLASH_QB = 256
FLASH_KB = 256
FLASH_MAX_BOUND = 50.0
LOG2E = 1.4426950408889634
NA_ROWS = 32
NA_QROWS = 4
NA_KROWS = 12
MASKED_LOGIT = -1e30
ROUNDING_MARGIN = 1.02


def _params(*sem):
    return pltpu.CompilerParams(dimension_semantics=sem, vmem_limit_bytes=VMEM_LIMIT)


def _dot(a, b):
    return jnp.dot(a.astype(BF16), b.astype(BF16), preferred_element_type=F32)


def _dot_nt(a, b):
    return lax.dot_general(a.astype(BF16), b.astype(BF16), (((1,), (1,)), ((), ())),
                           preferred_element_type=F32)


def _split3(t):
    hi = t.astype(BF16)
    r1 = t - hi.astype(F32)
    mid = r1.astype(BF16)
    return hi, mid, (r1 - mid.astype(F32)).astype(BF16)


def _dot_exact_rhs(a, b):
    hi, mid, lo = _split3(a)
    return (jnp.dot(hi, b, preferred_element_type=F32) + jnp.dot(mid, b, preferred_element_type=F32)
            + jnp.dot(lo, b, preferred_element_type=F32))


def _silu(t):
    return t / (1.0 + jnp.exp(-t))


def _sigmoid(t):
    return 1.0 / (1.0 + jnp.exp(-t))


def _mod_kernel(c_ref, w_ref, b_ref, o_ref):
    o_ref[...] = _dot(_silu(c_ref[...]), w_ref[...]) + b_ref[...]


def _modulation(cc, w, b):
    n = cc.shape[0]
    return pl.pallas_call(
        _mod_kernel,
        out_shape=jax.ShapeDtypeStruct((n, w.shape[1]), F32),
        compiler_params=_params(),
        name="adaln_mod",
    )(cc, w, b.reshape(1, -1))


def _norm_proj_kernel(x_ref, g_ref, sc_ref, sh_ref, w_ref, *refs, n_extra, want_h, epilogue):
    x = x_ref[0]
    xn = x * lax.rsqrt(jnp.mean(x * x, -1, keepdims=True) + EPS) * g_ref[...]
    h = (xn * (1.0 + sc_ref[0, 0]) + sh_ref[0, 0]).astype(BF16)
    p = jnp.dot(h, w_ref[...], preferred_element_type=F32)
    extra, outs = refs[:n_extra], refs[n_extra:]
    if want_h:
        outs[-1][0] = h
        outs = outs[:-1]
    epilogue(p, extra, outs)


def _store_projection(p, extra, outs):
    outs[0][0] = p


def _norm_proj(x_all, g, sc, sh, w, n_lat_tiles, want_h, epilogue=_store_projection, extra=(), extra_specs=(),
               out_specs=None, out_shape=None, name="norm_proj"):
    b, n, d = x_all.shape
    ncol = w.shape[1]
    kind = lambda i: jnp.where(i < n_lat_tiles, 1, 0)
    if out_specs is None:
        out_shape = [jax.ShapeDtypeStruct((b, n, ncol), F32)]
        out_specs = [pl.BlockSpec((1, TM, ncol), lambda bi, i: (bi, i, 0))]
    out_shape, out_specs = list(out_shape), list(out_specs)
    if want_h:
        out_shape.append(jax.ShapeDtypeStruct((b, n, d), BF16))
        out_specs.append(pl.BlockSpec((1, TM, d), lambda bi, i: (bi, i, 0)))
    return pl.pallas_call(
        functools.partial(_norm_proj_kernel, n_extra=len(extra), want_h=want_h, epilogue=epilogue),
        grid=(b, n // TM),
        in_specs=[
            pl.BlockSpec((1, TM, d), lambda bi, i: (bi, i, 0)),
            pl.BlockSpec((1, d), lambda bi, i: (0, 0)),
            pl.BlockSpec((1, 1, 1, d), lambda bi, i: (bi, kind(i), 0, 0)),
            pl.BlockSpec((1, 1, 1, d), lambda bi, i: (bi, kind(i), 0, 0)),
            pl.BlockSpec((d, ncol), lambda bi, i: (0, 0)),
        ] + list(extra_specs),
        out_specs=out_specs,
        out_shape=out_shape,
        compiler_params=_params("parallel", "parallel"),
        name=name,
    )(x_all, g.reshape(1, d), sc, sh, w, *extra)


def _head_norm(xh, gain, inv_d):
    ms = jnp.sum(xh * xh, -1, keepdims=True) * inv_d
    return xh * lax.rsqrt(ms + EPS) * gain


def _rope(xh, cos, sin, lo, half):
    lane = lax.broadcasted_iota(jnp.int32, xh.shape, 1)
    swapped = jnp.where(lane < lo + half, pltpu.roll(xh, LANES - half, 1), pltpu.roll(xh, half, 1))
    return xh * cos + swapped * sin


def _store_values_t(vt_ref, vt, n_heads):
    lead = lax.broadcasted_iota(jnp.int32, (V_ROWS - HD, vt.shape[1]), 0) == 0
    tail = jnp.where(lead, 1.0, 0.0).astype(BF16)
    for h in range(n_heads):
        vt_ref[0, h, :HD] = vt[h * HD:(h + 1) * HD].astype(BF16)
        vt_ref[0, h, HD:] = tail


def _head_specs(b, n, n_kv):
    hm = lambda bi, i: (bi, 0, i, 0)
    vm = lambda bi, i: (bi, 0, 0, i)
    specs = [pl.BlockSpec((1, TM, BR), lambda bi, i: (bi, i, 0)),
             pl.BlockSpec((1, N_HEADS, TM, LANES), hm),
             pl.BlockSpec((1, n_kv, TM, LANES), hm),
             pl.BlockSpec((1, n_kv, V_ROWS, TM), vm)]
    shapes = [jax.ShapeDtypeStruct((b, n, BR), F32),
              jax.ShapeDtypeStruct((b, N_HEADS, n, LANES), BF16),
              jax.ShapeDtypeStruct((b, n_kv, n, LANES), BF16),
              jax.ShapeDtypeStruct((b, n_kv, V_ROWS, n), BF16)]
    return specs, shapes


def _na_epilogue(p, extra, outs):
    qg_ref, kg_ref = extra
    g_ref, q_ref, k_ref, vt_ref = outs
    qoff, koff, voff = BR, BR + N_HEADS * LANES, BR + 2 * N_HEADS * LANES
    g_ref[0] = p[:, :BR]
    for h in range(N_HEADS):
        qh = p[:, qoff + h * LANES:qoff + (h + 1) * LANES]
        kh = p[:, koff + h * LANES:koff + (h + 1) * LANES]
        q_ref[0, h] = (_head_norm(qh, qg_ref[...], 1.0 / HD) * (HD ** -0.5 * LOG2E)).astype(BF16)
        k_ref[0, h] = _head_norm(kh, kg_ref[...], 1.0 / HD).astype(BF16)
    _store_values_t(vt_ref, p[:, voff:voff + BR].T, N_HEADS)


def _gqa_epilogue(p, extra, outs):
    qg_ref, kg_ref, cos_ref, sin_ref = extra
    g_ref, q_ref, k_ref, vt_ref = outs
    qoff, koff = BR, BR + N_HEADS * LANES
    voff = koff + GQA_KV_HEADS * LANES
    cos, sin = cos_ref[...], sin_ref[...]
    g_ref[0] = p[:, :BR]
    for h in range(N_HEADS):
        qh = _head_norm(p[:, qoff + h * LANES:qoff + (h + 1) * LANES], qg_ref[...], 1.0 / HD)
        q_ref[0, h] = (_rope(qh, cos, sin, 0, HD // 2) * (HD ** -0.5 * LOG2E)).astype(BF16)
    for h in range(GQA_KV_HEADS):
        kh = _head_norm(p[:, koff + h * LANES:koff + (h + 1) * LANES], kg_ref[...], 1.0 / HD)
        k_ref[0, h] = _rope(kh, cos, sin, 0, HD // 2).astype(BF16)
    _store_values_t(vt_ref, p[:, voff:voff + GQA_KV_HEADS * HD].T, GQA_KV_HEADS)


def _mla_epilogue(p, extra, outs):
    qa_ref, kva_ref, wuq_ref, wuk_ref, wuv_ref, qg_ref, kg_ref, cos_ref, sin_ref = extra
    g_ref, q_ref, k_ref, vt_ref = outs
    d_qk = MLA_NOPE + MLA_ROPE
    g_ref[0] = p[:, :BR]
    groups = [slice(0, TM // 2), slice(TM // 2, TM)]
    rms = lambda t, gain: t * lax.rsqrt(jnp.mean(t * t, -1, keepdims=True) + EPS) * gain
    cqn = [rms(p[r, BR:BR + MLA_RANK], qa_ref[...]) for r in groups]
    ckvn = [rms(p[r, BR + MLA_RANK:BR + 2 * MLA_RANK], kva_ref[...]) for r in groups]
    qf = [_dot(t, wuq_ref[...]) for t in cqn]
    kf = [_dot(t, wuk_ref[...]) for t in ckvn]
    kr = [p[r, BR + 2 * MLA_RANK:BR + 2 * MLA_RANK + LANES] for r in groups]
    vtf = [_dot_nt(wuv_ref[...], t) for t in ckvn]
    cos = [cos_ref[r, :] for r in groups]
    sin = [sin_ref[r, :] for r in groups]
    for h in range(N_HEADS):
        qh = [_head_norm(t[:, h * LANES:(h + 1) * LANES], qg_ref[...], 1.0 / d_qk) for t in qf]
        kh = [_head_norm(t[:, h * LANES:(h + 1) * LANES] + x, kg_ref[...], 1.0 / d_qk) for t, x in zip(kf, kr)]
        for r, x, y, c, s in zip(groups, qh, kh, cos, sin):
            q_ref[0, h, r] = (_rope(x, c, s, MLA_NOPE, MLA_ROPE // 2) * (d_qk ** -0.5 * LOG2E)).astype(BF16)
            k_ref[0, h, r] = _rope(y, c, s, MLA_NOPE, MLA_ROPE // 2).astype(BF16)
    _store_values_t(vt_ref, jnp.concatenate(vtf, axis=1), N_HEADS)


def _prep_kernel(p_ref, *refs, n_extra, epilogue):
    epilogue(p_ref[0], refs[:n_extra], refs[n_extra:])


def _branch_prep(p, epilogue, consts, tables, n_kv, name):
    b, n, ncol = p.shape
    full = lambda a: pl.BlockSpec(a.shape, lambda bi, i: (0,) * a.ndim)
    tab = pl.BlockSpec((TM, LANES), lambda bi, i: (i, 0))
    specs, shapes = _head_specs(b, n, n_kv)
    return pl.pallas_call(
        functools.partial(_prep_kernel, n_extra=len(consts) + len(tables), epilogue=epilogue),
        grid=(b, n // TM),
        in_specs=[pl.BlockSpec((1, TM, ncol), lambda bi, i: (bi, i, 0))]
        + [full(a) for a in consts] + [tab] * len(tables),
        out_specs=specs,
        out_shape=shapes,
        compiler_params=_params("parallel", "parallel"),
        name=name,
    )(p, *consts, *tables)


def _branch_proj(x_all, g, sc, sh, w, n_lat_tiles, want_h, epilogue, consts, tables, n_kv, name):
    b, n, _ = x_all.shape
    full = lambda a: pl.BlockSpec(a.shape, lambda bi, i: (0,) * a.ndim)
    tab = pl.BlockSpec((TM, LANES), lambda bi, i: (i, 0))
    specs, shapes = _head_specs(b, n, n_kv)
    return _norm_proj(x_all, g, sc, sh, w, n_lat_tiles, want_h, epilogue=epilogue,
                      extra=tuple(consts) + tuple(tables),
                      extra_specs=[full(a) for a in consts] + [tab] * len(tables),
                      out_specs=specs, out_shape=shapes, name=name)


def _flash_kernel(q_ref, k_ref, vt_ref, *rest, tk, nk, nq, kb, bounded):
    if bounded:
        bound_ref, o_ref, m_sc, acc_sc, s_sc = rest
        m_sc[...] = jnp.broadcast_to(bound_ref[...], m_sc.shape)
    else:
        o_ref, m_sc, acc_sc, s_sc = rest
        m_sc[...] = jnp.full(m_sc.shape, -jnp.inf, F32)
    acc_sc[...] = jnp.zeros(acc_sc.shape, F32)

    n_sub = tk // kb
    chains = range(nq)

    def scores(row0):
        k = k_ref[0, 0, pl.ds(row0, kb), :]
        return [lax.dot_general(k, q_ref[0, 0, c * FLASH_QB:(c + 1) * FLASH_QB, :],
                                (((1,), (1,)), ((), ())), preferred_element_type=F32) for c in chains]

    for c, s0 in zip(chains, scores(0)):
        s_sc[c] = s0

    def body(j, carry):
        off = pl.multiple_of(j * tk, tk)
        nxt = pl.multiple_of(jnp.minimum(j + 1, nk - 1) * tk, tk)
        m = [m_sc[c] for c in chains]
        acc = [acc_sc[c] for c in chains]
        s = [s_sc[c] for c in chains]
        for u in range(n_sub):
            k_next = k_ref[0, 0, pl.ds(off + (u + 1) * kb if u + 1 < n_sub else nxt, kb), :]
            vt = vt_ref[0, 0, :, pl.ds(off + u * kb, kb)]
            s_next = []
            for c in chains:
                s_next.append(lax.dot_general(k_next, q_ref[0, 0, c * FLASH_QB:(c + 1) * FLASH_QB, :],
                                              (((1,), (1,)), ((), ())), preferred_element_type=F32))
                if bounded:
                    p = jnp.exp2(s[c] - m[c])
                    acc[c] = acc[c] + jnp.dot(vt, p.astype(BF16), preferred_element_type=F32)
                    continue
                m_new = jnp.maximum(m[c], jnp.max(s[c], 0, keepdims=True))
                alpha = jnp.exp2(m[c] - m_new)
                p = jnp.exp2(s[c] - m_new)
                acc[c] = alpha * acc[c] + jnp.dot(vt, p.astype(BF16), preferred_element_type=F32)
                m[c] = m_new
            s = s_next
        for c in chains:
            m_sc[c], acc_sc[c], s_sc[c] = m[c], acc[c], s[c]
        return carry

    lax.fori_loop(0, nk, body, 0)
    for c in range(nq):
        o_ref[0, 0, :, c * FLASH_QB:(c + 1) * FLASH_QB] = acc_sc[c, :HD] / acc_sc[c, HD:HD + 1]


def _flash(q, k, vt, q0, n_q, k0, n_keys, tq, tk, bound=None):
    b, hq = q.shape[:2]
    hk = k.shape[1]
    rep = hq // hk
    assert n_q % tq == 0 and n_keys % tk == 0 and tq % FLASH_QB == 0
    kb = min(FLASH_KB, tk)
    assert tk % kb == 0
    assert q0 % tq == 0 and k0 % n_keys == 0
    hv = vt.shape[2]
    nq = tq // FLASH_QB
    qb0, kb0 = q0 // tq, k0 // n_keys

    def call(bounded):
        extra_specs = [pl.BlockSpec((1, FLASH_QB), lambda bi, h, i: (0, 0))] if bounded else []
        extra = [jnp.full((1, FLASH_QB), bound, F32)] if bounded else []
        return pl.pallas_call(
            functools.partial(_flash_kernel, tk=tk, nk=n_keys // tk, nq=nq, kb=kb, bounded=bounded),
            grid=(b, hq, n_q // tq),
            in_specs=[pl.BlockSpec((1, 1, tq, LANES), lambda bi, h, i: (bi, h, qb0 + i, 0)),
                      pl.BlockSpec((1, 1, n_keys, LANES), lambda bi, h, i: (bi, h // rep, kb0, 0)),
                      pl.BlockSpec((1, 1, hv, n_keys), lambda bi, h, i: (bi, h // rep, 0, kb0))] + extra_specs,
            out_specs=pl.BlockSpec((1, 1, HD, tq), lambda bi, h, i: (bi, h, 0, i)),
            out_shape=jax.ShapeDtypeStruct((b, hq, HD, n_q), F32),
            scratch_shapes=[pltpu.VMEM((nq, 1, FLASH_QB), F32), pltpu.VMEM((nq, hv, FLASH_QB), F32),
                            pltpu.VMEM((nq, kb, FLASH_QB), F32)],
            compiler_params=_params("parallel", "parallel", "arbitrary"),
            name="flash_attention_bounded" if bounded else "flash_attention",
        )(q, k, vt, *extra)

    if bound is None:
        return call(False)
    return lax.cond(bound <= FLASH_MAX_BOUND, lambda: call(True), lambda: call(False))


def _na_kernel(q_ref, k_ref, vt_ref, kc_ref, vtc_ref, bias_ref, o_ref, *, rows):
    kc = kc_ref[0, 0]
    vtc = vtc_ref[0, 0]
    nq, nk = NA_QROWS * GRID_W, NA_KROWS * GRID_W
    batches = range(NA_ROWS // NA_QROWS)
    r0 = [pl.program_id(2) * NA_ROWS + bi * NA_QROWS for bi in batches]
    koff = [pl.multiple_of(jnp.clip(r - NA_WIN_R // 2, 0, rows - NA_KROWS) * GRID_W, 2 * GRID_W) for r in r0]
    variant = [jnp.where(r == 0, 0, jnp.where(r == rows - NA_QROWS, 2, 1)) for r in r0]
    q = [q_ref[0, 0, bi * nq:(bi + 1) * nq, :] for bi in batches]
    nt = (((1,), (1,)), ((), ()))

    def scores(bi):
        s_w = lax.dot_general(k_ref[0, 0, pl.ds(koff[bi], nk), :], q[bi], nt, preferred_element_type=F32)
        return s_w + bias_ref[variant[bi], 0], lax.dot_general(kc, q[bi], nt, preferred_element_type=F32)

    s_next = scores(0)
    for bi in batches:
        s_w, s_c = s_next
        if bi + 1 < len(batches):
            s_next = scores(bi + 1)
        m = jnp.maximum(jnp.max(s_w, 0, keepdims=True), jnp.max(s_c, 0, keepdims=True))
        p_w = jnp.exp2(s_w - m).astype(BF16)
        p_c = jnp.exp2(s_c - m).astype(BF16)
        acc = (jnp.dot(vt_ref[0, 0, :, pl.ds(koff[bi], nk)], p_w, preferred_element_type=F32)
               + jnp.dot(vtc, p_c, preferred_element_type=F32))
        o_ref[0, 0, :, bi * nq:(bi + 1) * nq] = acc[:HD] / acc[HD:HD + 1]


def _na_bias_table(rpb):
    n_dr, n_dc = 2 * NA_WIN_R - 1, 2 * NA_WIN_C - 1
    sel_r = np.zeros((3, NA_KROWS, NA_QROWS, n_dr), np.float32)
    ok_r = np.zeros((3, NA_KROWS, NA_QROWS), bool)
    for kind in range(3):
        for i in range(NA_QROWS):
            start = (0, i, NA_KROWS - NA_WIN_R)[kind]
            shift = (0, -(NA_WIN_R // 2), NA_QROWS - NA_KROWS)[kind]
            for j in range(start, start + NA_WIN_R):
                sel_r[kind, j, i, j + shift - i + NA_WIN_R - 1] = 1.0
                ok_r[kind, j, i] = True
    cols = np.arange(GRID_W)
    cs = np.clip(cols - NA_WIN_C // 2, 0, GRID_W - NA_WIN_C)
    sel_c = np.zeros((GRID_W, GRID_W, n_dc), np.float32)
    ok_c = np.zeros((GRID_W, GRID_W), bool)
    for qc in range(GRID_W):
        for kcol in range(cs[qc], cs[qc] + NA_WIN_C):
            sel_c[kcol, qc, kcol - qc + NA_WIN_C - 1] = 1.0
            ok_c[kcol, qc] = True
    tab = jnp.einsum('hrc,vjir,kqc->vhjkiq', rpb, sel_r, sel_c, precision=HIGHEST)
    ok = ok_r[:, None, :, None, :, None] & ok_c[None, None, None, :, None, :]
    tab = jnp.where(ok, tab * LOG2E, MASKED_LOGIT)
    return tab.reshape(3, rpb.shape[0], NA_KROWS * GRID_W, NA_QROWS * GRID_W)


def _neighbourhood(q, k, vt, seq, n_ctx, bias):
    b, h = q.shape[:2]
    rows = seq // GRID_W
    assert rows >= NA_KROWS and rows % NA_ROWS == 0 and seq % n_ctx == 0
    tile = NA_ROWS * GRID_W
    hv = vt.shape[2]
    cb = seq // n_ctx
    return pl.pallas_call(
        functools.partial(_na_kernel, rows=rows),
        grid=(b, h, rows // NA_ROWS),
        in_specs=[pl.BlockSpec((1, 1, tile, LANES), lambda bi, hi, i: (bi, hi, i, 0)),
                  pl.BlockSpec((1, 1, seq, LANES), lambda bi, hi, i: (bi, hi, 0, 0)),
                  pl.BlockSpec((1, 1, hv, seq), lambda bi, hi, i: (bi, hi, 0, 0)),
                  pl.BlockSpec((1, 1, n_ctx, LANES), lambda bi, hi, i: (bi, hi, cb, 0)),
                  pl.BlockSpec((1, 1, hv, n_ctx), lambda bi, hi, i: (bi, hi, 0, cb)),
                  pl.BlockSpec((3, 1, NA_KROWS * GRID_W, NA_QROWS * GRID_W), lambda bi, hi, i: (0, hi, 0, 0))],
        out_specs=pl.BlockSpec((1, 1, HD, tile), lambda bi, hi, i: (bi, hi, 0, i)),
        out_shape=jax.ShapeDtypeStruct((b, h, HD, seq), F32),
        compiler_params=_params("parallel", "parallel", "arbitrary"),
        name="neighbourhood_attention",
    )(q, k, vt, k, vt, bias)


def _seg_sum(t, ones_blk):
    return _dot_exact_rhs(t, ones_blk)


def _rw_prep_kernel(p_ref, prev_ref, next_ref, mu_ref, kk_ref, ka_ref, rk_ref, w0_ref, w2_ref, a0_ref,
                    a2_ref, ones_ref, r_o, v_o, nkk_o, bonus_o, lw0_o, kd0_o, bd0_o, lw1_o, kd1_o, bd1_o,
                    *, n_lat_tiles, n_tiles):
    lw_o, kd_o, bd_o = (lw0_o, lw1_o), (kd0_o, kd1_o), (bd0_o, bd1_o)
    i = pl.program_id(1)
    u = p_ref[0, :, BR:]
    first = jnp.logical_or(i == 0, i == n_lat_tiles)
    last = jnp.logical_or(i == n_lat_tiles - 1, i == n_tiles - 1)
    prev_row = jnp.where(first, 0.0, prev_ref[0, F32_SUBLANES - 1:F32_SUBLANES, BR:])
    next_row = jnp.where(last, 0.0, next_ref[0, 0:1, BR:])
    row = lax.broadcasted_iota(jnp.int32, u.shape, 0)
    up = jnp.where(row == 0, prev_row, pltpu.roll(u, 1, 0))
    dn = jnp.where(row == TM - 1, next_row, pltpu.roll(u, TM - 1, 0))
    u = u + mu_ref[...] * (0.5 * (up + dn) - u)
    r, k, v = u[:, :BR], u[:, BR:2 * BR], u[:, 2 * BR:3 * BR]
    wl, al = u[:, 3 * BR:3 * BR + 2 * RW_LORA], u[:, 3 * BR + 2 * RW_LORA:]
    ones_blk = ones_ref[...]
    kk = k * kk_ref[...]
    kk = kk * lax.rsqrt(_seg_sum(kk * kk, ones_blk) + 1e-12)
    r_o[0] = r
    v_o[0] = v
    nkk_o[0] = -kk
    k_sum = jnp.zeros_like(k)
    for d in range(2):
        wl_d = wl[:, d * RW_LORA:(d + 1) * RW_LORA]
        al_d = al[:, d * RW_LORA:(d + 1) * RW_LORA]
        t = w0_ref[d] + _dot(jnp.tanh(wl_d), w2_ref[d])
        lw_o[d][0] = -RW_DECAY_SCALE * _sigmoid(t)
        a = _sigmoid(a0_ref[d] + _dot(al_d, a2_ref[d]))
        k_d = k * (1.0 + (a - 1.0) * ka_ref[...])
        kd_o[d][0] = k_d
        bd_o[d][0] = kk * a
        k_sum = k_sum + k_d
    bonus_o[0] = _seg_sum(r * k_sum * rk_ref[...], ones_blk) * v


def _rw_prep(pb, mu, kk, ka, rk, w0, w2, a0, a2, ones_blk, n_lat_tiles):
    b, n, ncol = pb.shape
    nt = n // TM
    r8 = TM // F32_SUBLANES
    full = lambda a: pl.BlockSpec(a.shape, lambda bi, i: (0,) * a.ndim)
    tok = pl.BlockSpec((1, TM, BR), lambda bi, i: (bi, i, 0))
    one = jax.ShapeDtypeStruct((b, n, BR), F32)
    return pl.pallas_call(
        functools.partial(_rw_prep_kernel, n_lat_tiles=n_lat_tiles, n_tiles=nt),
        grid=(b, nt),
        in_specs=[pl.BlockSpec((1, TM, ncol), lambda bi, i: (bi, i, 0)),
                  pl.BlockSpec((1, F32_SUBLANES, ncol), lambda bi, i: (bi, jnp.maximum(i * r8 - 1, 0), 0)),
                  pl.BlockSpec((1, F32_SUBLANES, ncol),
                               lambda bi, i: (bi, jnp.minimum((i + 1) * r8, n // F32_SUBLANES - 1), 0)),
                  full(mu), full(kk), full(ka), full(rk), full(w0), full(w2), full(a0), full(a2),
                  full(ones_blk)],
        out_specs=[tok] * 10,
        out_shape=[one] * 10,
        compiler_params=_params("parallel", "parallel"),
        name="rwkv_prep",
    )(pb, pb, pb, mu, kk, ka, rk, w0, w2, a0, a2, ones_blk)


def _rw_scan_kernel(r_ref, v_ref, a_ref, lw_ref, k_ref, b_ref, y_ref, st_ref, *, rev, n_b):
    c, w = CHUNK, RW_PACK * HD

    @pl.when(pl.program_id(0) == 0)
    def _():
        st_ref[...] = jnp.zeros(st_ref.shape, F32)

    tt = lax.broadcasted_iota(jnp.int32, (c, w), 0)
    ss = lax.broadcasted_iota(jnp.int32, (c, w), 1) & (c - 1)
    strict = (ss > tt) if rev else (ss < tt)
    incl = (ss >= tt) if rev else (ss <= tt)
    eye = (ss == tt).astype(F32)
    t2 = lax.broadcasted_iota(jnp.int32, (c, c), 0)
    s2 = lax.broadcasted_iota(jnp.int32, (c, c), 1)
    m_incl = ((s2 >= t2) if rev else (s2 <= t2)).astype(BF16)
    hd_log2 = HD.bit_length() - 1
    blk = ((lax.broadcasted_iota(jnp.int32, (w, w), 0) >> hd_log2)
           == (lax.broadcasted_iota(jnp.int32, (w, w), 1) >> hd_log2))

    def bd(t):
        tb = t.astype(BF16)
        return jnp.where(blk, jnp.concatenate([tb] * RW_PACK, axis=0), jnp.zeros((), BF16))

    def dtn(x, y):
        return lax.dot_general(x.astype(BF16), y.astype(BF16), (((0,), (0,)), ((), ())),
                               preferred_element_type=F32)

    chunks = tuple(range(RW_TT // c))
    chunks = chunks[::-1] if rev else chunks
    chains = [(bi, g) for bi in range(n_b) for g in range(BR // w)]
    units = [(ci, bi, g) for ci in chunks for (bi, g) in chains]
    sl = lambda ref, u: ref[u[1], u[0] * c:(u[0] + 1) * c, u[2] * w:(u[2] + 1) * w]

    lw = [sl(lw_ref, u) for u in units]
    cum = []
    for t in lw:
        hi, mid, lo = _split3(t)
        cum.append(jnp.dot(m_incl, hi, preferred_element_type=F32)
                   + jnp.dot(m_incl, mid, preferred_element_type=F32)
                   + jnp.dot(m_incl, lo, preferred_element_type=F32))
    tot = [jnp.sum(t, 0, keepdims=True) for t in lw]
    p_inv = [jnp.exp(-x) for x in cum]
    p_end = [jnp.exp(t - x) for t, x in zip(tot, cum)]
    a_t = [sl(a_ref, u) * jnp.exp(x - l) for u, x, l in zip(units, cum, lw)]
    r_t = [sl(r_ref, u) * jnp.exp(x) for u, x in zip(units, cum)]
    b_raw = [sl(b_ref, u) for u in units]
    k_raw = [sl(k_ref, u) for u in units]
    v = [sl(v_ref, u) for u in units]
    b_t = [x * p for x, p in zip(b_raw, p_inv)]
    k_t = [x * p for x, p in zip(k_raw, p_inv)]
    b_h = [x * p for x, p in zip(b_raw, p_end)]
    k_h = [x * p for x, p in zip(k_raw, p_end)]
    ar = [jnp.concatenate([x, y], axis=0) for x, y in zip(a_t, r_t)]
    g_b = [_dot_nt(x, bd(y)) for x, y in zip(ar, b_t)]
    g_k = [_dot_nt(x, bd(y)) for x, y in zip(ar, k_t)]
    lab = [jnp.where(strict, x[:c], 0.0) for x in g_b]
    lak = [jnp.where(strict, x[:c], 0.0) for x in g_k]
    qrb = [jnp.where(incl, x[c:], 0.0) for x in g_b]
    qrk = [jnp.where(incl, x[c:], 0.0) for x in g_k]
    tinv = [eye + x for x in lab]
    lp = lab
    stack = lambda x, y: jnp.concatenate([x, y], axis=0)
    for i in range(6):
        lp_bd = [bd(x) for x in lp]
        if i == 0:
            lp = [_dot(x, y) for x, y in zip(lp, lp_bd)]
        elif i < 5:
            both = [_dot(stack(t, x), y) for t, x, y in zip(tinv, lp, lp_bd)]
            tinv = [t + z[:c] for t, z in zip(tinv, both)]
            lp = [z[c:] for z in both]
        else:
            tinv = [t + _dot(t, y) for t, y in zip(tinv, lp_bd)]
    both = [_dot(stack(x, y), bd(z)) for x, y, z in zip(lak, qrk, v)]
    wv = [z[:c] for z in both]
    y_loc = [z[c:] for z in both]
    a_hat = [_dot(x, bd(y)) for x, y in zip(tinv, a_t)]
    u_hat = [_dot(x, bd(y)) for x, y in zip(tinv, wv)]
    ar_hat = [stack(x, y) for x, y in zip(a_hat, r_t)]
    bk_h = [stack(x, y) for x, y in zip(b_h, k_h)]
    p_c = [jnp.exp(t) for t in tot]

    n_ch = len(chains)
    for j in range(len(chunks)):
        idx = range(j * n_ch, (j + 1) * n_ch)
        st = [st_ref[q] for q in range(n_ch)]
        ur = [_dot_nt(ar_hat[i], s) for i, s in zip(idx, st)]
        u = [z[:c] + u_hat[i] for i, z in zip(idx, ur)]
        y = [z[c:] + _dot(qrb[i], bd(x)) + y_loc[i] for i, z, x in zip(idx, ur, u)]
        new = [s * p_c[i] + jnp.where(blk, dtn(stack(x, v[i]), bk_h[i]), 0.0) for i, s, x in zip(idx, st, u)]
        for q, i in enumerate(idx):
            y_ref[units[i][1], units[i][0] * c:(units[i][0] + 1) * c, units[i][2] * w:(units[i][2] + 1) * w] = y[q]
            st_ref[q] = new[q]


def _rw_scan(r, v, nkk, lw, kd, bd, n_ctx, rev):
    b, n, _ = r.shape
    assert n % RW_TT == 0 and n_ctx % RW_TT == 0
    nblk = n // RW_TT
    nctx = n_ctx // RW_TT
    if rev:
        blk_of = lambda i: nblk - 1 - i
    else:
        blk_of = lambda i: jnp.where(i < nctx, nblk - nctx + i, i - nctx)
    spec = pl.BlockSpec((b, RW_TT, BR), lambda i: (0, blk_of(i), 0))
    n_chains = b * (BR // (RW_PACK * HD))
    return pl.pallas_call(
        functools.partial(_rw_scan_kernel, rev=rev, n_b=b),
        grid=(nblk,),
        in_specs=[spec] * 6,
        out_specs=spec,
        out_shape=jax.ShapeDtypeStruct((b, n, BR), F32),
        scratch_shapes=[pltpu.VMEM((n_chains, RW_PACK * HD, RW_PACK * HD), F32)],
        compiler_params=_params("arbitrary"),
        name="rwkv_scan",
    )(r, v, nkk, lw, kd, bd)


def _merge_kernel(x_ref, h_ref, ya_ref, yac_ref, ybf_ref, ybr_ref, yc_ref, ycc_ref, yd_ref, ydc_ref,
                  ga_ref, gb_ref, gc_ref, gd_ref, bonus_ref, gnw_ref, gnb_ref, ones_ref, mgw_ref, mgb_ref,
                  wbr_ref, wout_ref, gt_ref, o_ref, *, n_lat_tiles):
    h = h_ref[0]
    ones_blk = ones_ref[...]
    is_ctx = pl.program_id(1) >= n_lat_tiles

    def token_major(lat_ref, ctx_ref):
        yt = jnp.where(is_ctx, ctx_ref[0], lat_ref[0])
        return yt.reshape(BR, TM).T

    yb = ybf_ref[0] + ybr_ref[0]
    mean = _seg_sum(yb, ones_blk) * (1.0 / HD)
    cen = yb - mean
    var = _seg_sum(cen * cen, ones_blk) * (1.0 / HD)
    yb = cen * lax.rsqrt(var + RW_GN_EPS) * gnw_ref[...] + gnb_ref[...] + bonus_ref[0]
    ys = (token_major(ya_ref, yac_ref), yb, token_major(yc_ref, ycc_ref), token_major(yd_ref, ydc_ref))
    gs = (ga_ref[0], gb_ref[0], gc_ref[0], gd_ref[0])
    acc = None
    for i in range(4):
        gate = _sigmoid(jnp.dot(h, mgw_ref[i], preferred_element_type=F32) + mgb_ref[i])
        term = gate * _dot(ys[i] * _silu(gs[i]), wbr_ref[i])
        acc = term if acc is None else acc + term
    o_ref[0] = x_ref[0] + gt_ref[0, 0] * _dot(acc, wout_ref[...])


def _merge(x_all, h, ya, ybf, ybr, yc, yd, pa, pb, pc, pd, bonus, gnw, gnb, ones_blk, mgw, mgb, wbr, wout, gt,
           n_lat_tiles, n_out):
    b, n, d = x_all.shape
    kind = lambda i: jnp.where(i < n_lat_tiles, 1, 0)
    tok = lambda w: pl.BlockSpec((1, TM, w), lambda bi, i: (bi, i, 0))
    full = lambda a: pl.BlockSpec(a.shape, lambda bi, i: (0,) * a.ndim)
    lat = pl.BlockSpec((1, N_HEADS, HD, TM), lambda bi, i: (bi, 0, 0, jnp.minimum(i, n_lat_tiles - 1)))
    ctx = pl.BlockSpec((1, N_HEADS, HD, TM), lambda bi, i: (bi, 0, 0, 0))
    return pl.pallas_call(
        functools.partial(_merge_kernel, n_lat_tiles=n_lat_tiles),
        grid=(b, n_out // TM),
        in_specs=[tok(d), tok(d), lat, ctx, tok(BR), tok(BR), lat, ctx, lat, ctx,
                  tok(BR), tok(BR), tok(BR), tok(BR), tok(BR),
                  full(gnw), full(gnb), full(ones_blk), full(mgw), full(mgb), full(wbr), full(wout),
                  pl.BlockSpec((1, 1, 1, d), lambda bi, i: (bi, kind(i), 0, 0))],
        out_specs=tok(d),
        out_shape=jax.ShapeDtypeStruct((b, n_out, d), F32),
        compiler_params=_params("parallel", "parallel"),
        name="merge_out",
    )(x_all, h, ya[0], ya[1], ybf, ybr, yc[0], yc[1], yd[0], yd[1], pa, pb, pc, pd, bonus,
      gnw, gnb, ones_blk, mgw, mgb, wbr, wout, gt)


def _pad_heads(w, n_heads, d):
    lead = w.shape[:-1]
    w = w.reshape(lead + (n_heads, d))
    w = jnp.pad(w, [(0, 0)] * len(lead) + [(0, 0), (0, LANES - d)])
    return w.reshape(lead + (n_heads * LANES,))


def _pad_vec(g):
    return jnp.pad(g, (0, LANES - g.shape[0])).reshape(1, LANES)


def _rope_tables(n_lat, n_ctx, d_rot, lo):
    t = np.arange(n_lat)
    row = (t // GRID_W).astype(np.float32)
    col = (t % GRID_W).astype(np.float32)
    n_freq = d_rot // 4
    inv = np.float32(ROPE_BASE) ** (-np.arange(n_freq, dtype=np.float32) / np.float32(n_freq))
    ang = np.concatenate([row[:, None] * inv, col[:, None] * inv], -1).astype(np.float32)
    cos, sin = np.cos(ang), np.sin(ang)
    half = d_rot // 2
    cos_t = np.ones((n_lat + n_ctx, LANES), np.float32)
    sin_t = np.zeros((n_lat + n_ctx, LANES), np.float32)
    cos_t[:n_lat, lo:lo + half] = cos
    cos_t[:n_lat, lo + half:lo + d_rot] = cos
    sin_t[:n_lat, lo:lo + half] = -sin
    sin_t[:n_lat, lo + half:lo + d_rot] = sin
    return jnp.asarray(cos_t), jnp.asarray(sin_t)


def _pick_tile(n, candidates):
    for c in candidates:
        if n % c == 0:
            return c
    raise ValueError(f"no tile for {n}")


def kernel(x, c, ctx, c_ctx, norm_g, mod_w, mod_b, w_in, na_qg, na_kg, na_rpb, rw_mu, rw_w0, rw_w2, rw_a0, rw_a2, rw_kk, rw_ka, rw_rk, rw_gn_w, rw_gn_b, mla_qa_g, mla_kva_g, mla_wuq, mla_wukv, mla_qg, mla_kg, gqa_qg, gqa_kg, mg_w, mg_b, w_br, w_out):
    bsz, seq, d = x.shape
    n_ctx = ctx.shape[1]
    depth = w_in.shape[0]
    assert d == D_MODEL and n_ctx == TM and seq % (NA_ROWS * GRID_W) == 0
    n = seq + n_ctx
    n_lat_tiles = seq // TM

    gqa_cos, gqa_sin = _rope_tables(seq, n_ctx, HD, 0)
    mla_cos, mla_sin = _rope_tables(seq, n_ctx, MLA_ROPE, MLA_NOPE)
    ones_blk = jnp.asarray(np.kron(np.eye(N_HEADS), np.ones((HD, HD))), BF16)
    tq = _pick_tile(seq, (2048, 1024, 512, 256))
    tk = _pick_tile(n, (3328, 1280, 256))

    def attend(q, k, vt, bound):
        return (_flash(q, k, vt, 0, seq, 0, n, tq, tk, bound),
                _flash(q, k, vt, seq, n_ctx, seq, n_ctx, n_ctx, n_ctx))

    def logit_bound(q_gain, k_gain, d_head):
        return (d_head ** 0.5 * LOG2E * ROUNDING_MARGIN) * jnp.max(jnp.abs(q_gain)) * jnp.max(jnp.abs(k_gain))

    x_all = jnp.concatenate([x, ctx], axis=1)
    cc = jnp.concatenate([c_ctx[None], c], axis=0)
    for l in range(depth):
        w = w_in[l]
        o = 0
        aq, ak, av, ag = (w[:, o + i * BR:o + (i + 1) * BR] for i in range(4))
        o += 4 * BR
        bu, bg = w[:, o:o + U_COLS], w[:, o + U_COLS:o + U_COLS + BR]
        o += U_COLS + BR
        ccq, cckv = w[:, o:o + MLA_RANK], w[:, o + MLA_RANK:o + 2 * MLA_RANK]
        ckr = w[:, o + 2 * MLA_RANK:o + 2 * MLA_RANK + MLA_ROPE]
        cg = w[:, o + 2 * MLA_RANK + MLA_ROPE:o + 2 * MLA_RANK + MLA_ROPE + BR]
        o += 2 * MLA_RANK + MLA_ROPE + BR
        dq = w[:, o:o + BR]
        dk = w[:, o + BR:o + BR + GQA_KV_HEADS * HD]
        dv = w[:, o + BR + GQA_KV_HEADS * HD:o + BR + 2 * GQA_KV_HEADS * HD]
        dg = w[:, o + BR + 2 * GQA_KV_HEADS * HD:]
        ckr_t = jnp.pad(ckr, ((0, 0), (MLA_NOPE, LANES - MLA_NOPE - MLA_ROPE)))
        w_a = jnp.concatenate([ag, _pad_heads(aq, N_HEADS, HD), _pad_heads(ak, N_HEADS, HD), av], 1).astype(BF16)
        w_b = jnp.concatenate([bg, bu], 1).astype(BF16)
        w_c = jnp.concatenate([cg, ccq, cckv, ckr_t], 1).astype(BF16)
        w_d = jnp.concatenate([dg, _pad_heads(dq, N_HEADS, HD), _pad_heads(dk, GQA_KV_HEADS, HD), dv], 1).astype(BF16)
        wuq = _pad_heads(mla_wuq[l], N_HEADS, MLA_NOPE + MLA_ROPE).astype(BF16)
        wukv = mla_wukv[l].reshape(MLA_RANK, N_HEADS, MLA_NOPE + HD)
        wuk = _pad_heads(wukv[:, :, :MLA_NOPE].reshape(MLA_RANK, -1), N_HEADS, MLA_NOPE).astype(BF16)
        wuv = wukv[:, :, MLA_NOPE:].reshape(MLA_RANK, -1).T.astype(BF16)

        mod = _modulation(cc, mod_w[l], mod_b[l])
        sh, sc, gt = jnp.split(mod, 3, axis=-1)
        pair = lambda m: jnp.stack([jnp.broadcast_to(m[0], (bsz, d)), m[1:]], axis=1)[:, :, None, :]
        sh2, sc2, gt2 = pair(sh), pair(sc), pair(gt)

        row = lambda t: t.reshape(1, -1)
        norm = (x_all, norm_g[l], sc2, sh2)

        ga, qa, ka, vat, h = _branch_proj(*norm, w_a, n_lat_tiles, True, _na_epilogue,
                                          (_pad_vec(na_qg[l]), _pad_vec(na_kg[l])), (), N_HEADS, "proj_na")
        ya = (_neighbourhood(qa, ka, vat, seq, n_ctx, _na_bias_table(na_rpb[l])),
              _flash(qa, ka, vat, seq, n_ctx, seq, n_ctx, n_ctx, n_ctx))

        pb, = _norm_proj(*norm, w_b, n_lat_tiles, False)
        r, v, nkk, bonus, lw0, kd0, bd0, lw1, kd1, bd1 = _rw_prep(
            pb, row(rw_mu[l]), row(rw_kk[l]), row(rw_ka[l]), row(rw_rk[l]),
            rw_w0[l][:, None, :], rw_w2[l].astype(BF16), rw_a0[l][:, None, :], rw_a2[l].astype(BF16),
            ones_blk, n_lat_tiles)
        ybf = _rw_scan(r, v, nkk, lw0, kd0, bd0, n_ctx, False)
        ybr = _rw_scan(r, v, nkk, lw1, kd1, bd1, n_ctx, True)

        pc, = _norm_proj(*norm, w_c, n_lat_tiles, False)
        gc, qc, kc, vct = _branch_prep(pc, _mla_epilogue,
                                       (row(mla_qa_g[l]), row(mla_kva_g[l]), wuq, wuk, wuv,
                                        _pad_vec(mla_qg[l]), _pad_vec(mla_kg[l])), (mla_cos, mla_sin),
                                       N_HEADS, "mla_prep")
        yc = attend(qc, kc, vct, logit_bound(mla_qg[l], mla_kg[l], MLA_NOPE + MLA_ROPE))

        gd, qd, kd_, vdt = _branch_proj(*norm, w_d, n_lat_tiles, False, _gqa_epilogue,
                                        (_pad_vec(gqa_qg[l]), _pad_vec(gqa_kg[l])), (gqa_cos, gqa_sin),
                                        GQA_KV_HEADS, "proj_gqa")
        yd = attend(qd, kd_, vdt, logit_bound(gqa_qg[l], gqa_kg[l], HD))

        n_out = n if l + 1 < depth else seq
        x_all = _merge(x_all, h, ya, ybf, ybr, yc, yd, ga, pb, gc, gd, bonus,
                       row(rw_gn_w[l]), row(rw_gn_b[l]), ones_blk,
                       mg_w[l].astype(BF16), mg_b[l][:, None, :], w_br[l].astype(BF16), w_out[l].astype(BF16),
                       gt2, n_lat_tiles, n_out)
    return x_all
```

```python
import functools

import numpy as np
import jax
import jax.numpy as jnp
from jax import lax
from jax.experimental import pallas as pl
from jax.experimental.pallas import tpu as pltpu

F32 = jnp.float32
BF16 = jnp.bfloat16
HIGHEST = lax.Precision.HIGHEST

D_MODEL = 1024
GRID_W = 64
BR = 512
HD = 64
N_HEADS = BR // HD
EPS = 1e-6
ROPE_BASE = 10000.0
NA_WIN_R = 8
NA_WIN_C = 16
RW_LORA = 64
RW_GN_EPS = 64e-5
RW_DECAY_SCALE = float(np.exp(-0.5))
MLA_RANK = 256
MLA_NOPE = 64
MLA_ROPE = 32
GQA_KV_HEADS = 2
U_COLS = 3 * BR + 4 * RW_LORA

LANES = 128
F32_SUBLANES = 8
BF16_SUBLANES = 16
V_ROWS = HD + BF16_SUBLANES
VMEM_LIMIT = 56 * 1024 * 1024

TM = 256
CHUNK = 64
RW_TT = 256
RW_PACK = 4
FLASH_QB = 256
FLASH_KB = 256
FLASH_MAX_BOUND = 50.0
LOG2E = 1.4426950408889634
NA_ROWS = 32
NA_QROWS = 4
NA_KROWS = 12
MASKED_LOGIT = -1e30
ROUNDING_MARGIN = 1.02


def _params(*sem):
    return pltpu.CompilerParams(dimension_semantics=sem, vmem_limit_bytes=VMEM_LIMIT)


def _dot(a, b):
    return jnp.dot(a.astype(BF16), b.astype(BF16), preferred_element_type=F32)


def _dot_nt(a, b):
    return lax.dot_general(a.astype(BF16), b.astype(BF16), (((1,), (1,)), ((), ())),
                           preferred_element_type=F32)


def _split3(t):
    hi = t.astype(BF16)
    r1 = t - hi.astype(F32)
    mid = r1.astype(BF16)
    return hi, mid, (r1 - mid.astype(F32)).astype(BF16)


def _dot_exact_rhs(a, b):
    hi, mid, lo = _split3(a)
    return (jnp.dot(hi, b, preferred_element_type=F32) + jnp.dot(mid, b, preferred_element_type=F32)
            + jnp.dot(lo, b, preferred_element_type=F32))


def _silu(t):
    return t / (1.0 + jnp.exp(-t))


def _sigmoid(t):
    return 1.0 / (1.0 + jnp.exp(-t))


def _mod_kernel(c_ref, w_ref, b_ref, o_ref):
    o_ref[...] = _dot(_silu(c_ref[...]), w_ref[...]) + b_ref[...]


def _modulation(cc, w, b):
    n = cc.shape[0]
    return pl.pallas_call(
        _mod_kernel,
        out_shape=jax.ShapeDtypeStruct((n, w.shape[1]), F32),
        compiler_params=_params(),
        name="adaln_mod",
    )(cc, w, b.reshape(1, -1))


def _norm_proj_kernel(x_ref, g_ref, sc_ref, sh_ref, w_ref, *refs, n_extra, want_h, epilogue):
    x = x_ref[0]
    xn = x * lax.rsqrt(jnp.mean(x * x, -1, keepdims=True) + EPS) * g_ref[...]
    h = (xn * (1.0 + sc_ref[0, 0]) + sh_ref[0, 0]).astype(BF16)
    p = jnp.dot(h, w_ref[...], preferred_element_type=F32)
    extra, outs = refs[:n_extra], refs[n_extra:]
    if want_h:
        outs[-1][0] = h
        outs = outs[:-1]
    epilogue(p, extra, outs)


def _store_projection(p, extra, outs):
    outs[0][0] = p


def _norm_proj(x_all, g, sc, sh, w, n_lat_tiles, want_h, epilogue=_store_projection, extra=(), extra_specs=(),
               out_specs=None, out_shape=None, name="norm_proj"):
    b, n, d = x_all.shape
    ncol = w.shape[1]
    kind = lambda i: jnp.where(i < n_lat_tiles, 1, 0)
    if out_specs is None:
        out_shape = [jax.ShapeDtypeStruct((b, n, ncol), F32)]
        out_specs = [pl.BlockSpec((1, TM, ncol), lambda bi, i: (bi, i, 0))]
    out_shape, out_specs = list(out_shape), list(out_specs)
    if want_h:
        out_shape.append(jax.ShapeDtypeStruct((b, n, d), BF16))
        out_specs.append(pl.BlockSpec((1, TM, d), lambda bi, i: (bi, i, 0)))
    return pl.pallas_call(
        functools.partial(_norm_proj_kernel, n_extra=len(extra), want_h=want_h, epilogue=epilogue),
        grid=(b, n // TM),
        in_specs=[
            pl.BlockSpec((1, TM, d), lambda bi, i: (bi, i, 0)),
            pl.BlockSpec((1, d), lambda bi, i: (0, 0)),
            pl.BlockSpec((1, 1, 1, d), lambda bi, i: (bi, kind(i), 0, 0)),
            pl.BlockSpec((1, 1, 1, d), lambda bi, i: (bi, kind(i), 0, 0)),
            pl.BlockSpec((d, ncol), lambda bi, i: (0, 0)),
        ] + list(extra_specs),
        out_specs=out_specs,
        out_shape=out_shape,
        compiler_params=_params("parallel", "parallel"),
        name=name,
    )(x_all, g.reshape(1, d), sc, sh, w, *extra)


def _head_norm(xh, gain, inv_d):
    ms = jnp.sum(xh * xh, -1, keepdims=True) * inv_d
    return xh * lax.rsqrt(ms + EPS) * gain


def _rope(xh, cos, sin, lo, half):
    lane = lax.broadcasted_iota(jnp.int32, xh.shape, 1)
    swapped = jnp.where(lane < lo + half, pltpu.roll(xh, LANES - half, 1), pltpu.roll(xh, half, 1))
    return xh * cos + swapped * sin


def _store_values_t(vt_ref, vt, n_heads):
    lead = lax.broadcasted_iota(jnp.int32, (V_ROWS - HD, vt.shape[1]), 0) == 0
    tail = jnp.where(lead, 1.0, 0.0).astype(BF16)
    for h in range(n_heads):
        vt_ref[0, h, :HD] = vt[h * HD:(h + 1) * HD].astype(BF16)
        vt_ref[0, h, HD:] = tail


def _head_specs(b, n, n_kv):
    hm = lambda bi, i: (bi, 0, i, 0)
    vm = lambda bi, i: (bi, 0, 0, i)
    specs = [pl.BlockSpec((1, TM, BR), lambda bi, i: (bi, i, 0)),
             pl.BlockSpec((1, N_HEADS, TM, LANES), hm),
             pl.BlockSpec((1, n_kv, TM, LANES), hm),
             pl.BlockSpec((1, n_kv, V_ROWS, TM), vm)]
    shapes = [jax.ShapeDtypeStruct((b, n, BR), F32),
              jax.ShapeDtypeStruct((b, N_HEADS, n, LANES), BF16),
              jax.ShapeDtypeStruct((b, n_kv, n, LANES), BF16),
              jax.ShapeDtypeStruct((b, n_kv, V_ROWS, n), BF16)]
    return specs, shapes


def _na_epilogue(p, extra, outs):
    qg_ref, kg_ref = extra
    g_ref, q_ref, k_ref, vt_ref = outs
    qoff, koff, voff = BR, BR + N_HEADS * LANES, BR + 2 * N_HEADS * LANES
    g_ref[0] = p[:, :BR]
    for h in range(N_HEADS):
        qh = p[:, qoff + h * LANES:qoff + (h + 1) * LANES]
        kh = p[:, koff + h * LANES:koff + (h + 1) * LANES]
        q_ref[0, h] = (_head_norm(qh, qg_ref[...], 1.0 / HD) * (HD ** -0.5 * LOG2E)).astype(BF16)
        k_ref[0, h] = _head_norm(kh, kg_ref[...], 1.0 / HD).astype(BF16)
    _store_values_t(vt_ref, p[:, voff:voff + BR].T, N_HEADS)


def _gqa_epilogue(p, extra, outs):
    qg_ref, kg_ref, cos_ref, sin_ref = extra
    g_ref, q_ref, k_ref, vt_ref = outs
    qoff, koff = BR, BR + N_HEADS * LANES
    voff = koff + GQA_KV_HEADS * LANES
    cos, sin = cos_ref[...], sin_ref[...]
    g_ref[0] = p[:, :BR]
    rope = lambda t: t * cos + pltpu.roll(t, HD // 2, 1) * sin
    for h in range(N_HEADS):
        qh = _head_norm(p[:, qoff + h * LANES:qoff + (h + 1) * LANES], qg_ref[...], 1.0 / LANES)
        q_ref[0, h] = (rope(qh) * (0.5 * HD ** -0.5 * LOG2E)).astype(BF16)
    for h in range(GQA_KV_HEADS):
        kh = _head_norm(p[:, koff + h * LANES:koff + (h + 1) * LANES], kg_ref[...], 1.0 / LANES)
        k_ref[0, h] = rope(kh).astype(BF16)
    _store_values_t(vt_ref, p[:, voff:voff + GQA_KV_HEADS * HD].T, GQA_KV_HEADS)


def _mla_epilogue(p, extra, outs):
    qa_ref, kva_ref, wuq_ref, wuk_ref, wuv_ref, qg_ref, kg_ref, cos_ref, sin_ref = extra
    g_ref, q_ref, k_ref, vt_ref = outs
    d_qk = MLA_NOPE + MLA_ROPE
    g_ref[0] = p[:, :BR]
    groups = [slice(0, TM // 2), slice(TM // 2, TM)]
    rms = lambda t, gain: t * lax.rsqrt(jnp.mean(t * t, -1, keepdims=True) + EPS) * gain
    cqn = [rms(p[r, BR:BR + MLA_RANK], qa_ref[...]) for r in groups]
    ckvn = [rms(p[r, BR + MLA_RANK:BR + 2 * MLA_RANK], kva_ref[...]) for r in groups]
    qf = [_dot(t, wuq_ref[...]) for t in cqn]
    kf = [_dot(t, wuk_ref[...]) for t in ckvn]
    kr = [p[r, BR + 2 * MLA_RANK:BR + 2 * MLA_RANK + LANES] for r in groups]
    vtf = [_dot_nt(wuv_ref[...], t) for t in ckvn]
    cos = [cos_ref[r, :] for r in groups]
    sin = [sin_ref[r, :] for r in groups]
    for h in range(N_HEADS):
        qh = [_head_norm(t[:, h * LANES:(h + 1) * LANES], qg_ref[...], 1.0 / d_qk) for t in qf]
        kh = [_head_norm(t[:, h * LANES:(h + 1) * LANES] + x, kg_ref[...], 1.0 / d_qk) for t, x in zip(kf, kr)]
        for r, x, y, c, s in zip(groups, qh, kh, cos, sin):
            q_ref[0, h, r] = (_rope(x, c, s, MLA_NOPE, MLA_ROPE // 2) * (d_qk ** -0.5 * LOG2E)).astype(BF16)
            k_ref[0, h, r] = _rope(y, c, s, MLA_NOPE, MLA_ROPE // 2).astype(BF16)
    _store_values_t(vt_ref, jnp.concatenate(vtf, axis=1), N_HEADS)


def _prep_kernel(p_ref, *refs, n_extra, epilogue):
    epilogue(p_ref[0], refs[:n_extra], refs[n_extra:])


def _branch_prep(p, epilogue, consts, tables, n_kv, name):
    b, n, ncol = p.shape
    full = lambda a: pl.BlockSpec(a.shape, lambda bi, i: (0,) * a.ndim)
    tab = pl.BlockSpec((TM, LANES), lambda bi, i: (i, 0))
    specs, shapes = _head_specs(b, n, n_kv)
    return pl.pallas_call(
        functools.partial(_prep_kernel, n_extra=len(consts) + len(tables), epilogue=epilogue),
        grid=(b, n // TM),
        in_specs=[pl.BlockSpec((1, TM, ncol), lambda bi, i: (bi, i, 0))]
        + [full(a) for a in consts] + [tab] * len(tables),
        out_specs=specs,
        out_shape=shapes,
        compiler_params=_params("parallel", "parallel"),
        name=name,
    )(p, *consts, *tables)


def _branch_proj(x_all, g, sc, sh, w, n_lat_tiles, want_h, epilogue, consts, tables, n_kv, name):
    b, n, _ = x_all.shape
    full = lambda a: pl.BlockSpec(a.shape, lambda bi, i: (0,) * a.ndim)
    tab = pl.BlockSpec((TM, LANES), lambda bi, i: (i, 0))
    specs, shapes = _head_specs(b, n, n_kv)
    return _norm_proj(x_all, g, sc, sh, w, n_lat_tiles, want_h, epilogue=epilogue,
                      extra=tuple(consts) + tuple(tables),
                      extra_specs=[full(a) for a in consts] + [tab] * len(tables),
                      out_specs=specs, out_shape=shapes, name=name)


def _flash_kernel(q_ref, k_ref, vt_ref, *rest, tk, nk, nq, kb, bounded):
    if bounded:
        bound_ref, o_ref, m_sc, acc_sc, s_sc = rest
        m_sc[...] = jnp.broadcast_to(bound_ref[...], m_sc.shape)
    else:
        o_ref, m_sc, acc_sc, s_sc = rest
        m_sc[...] = jnp.full(m_sc.shape, -jnp.inf, F32)
    acc_sc[...] = jnp.zeros(acc_sc.shape, F32)

    n_sub = tk // kb
    chains = range(nq)

    def scores(row0):
        k = k_ref[0, 0, pl.ds(row0, kb), :]
        return [lax.dot_general(k, q_ref[0, 0, c * FLASH_QB:(c + 1) * FLASH_QB, :],
                                (((1,), (1,)), ((), ())), preferred_element_type=F32) for c in chains]

    for c, s0 in zip(chains, scores(0)):
        s_sc[c] = s0

    def body(j, carry):
        off = pl.multiple_of(j * tk, tk)
        nxt = pl.multiple_of(jnp.minimum(j + 1, nk - 1) * tk, tk)
        m = [m_sc[c] for c in chains]
        acc = [acc_sc[c] for c in chains]
        s = [s_sc[c] for c in chains]
        for u in range(n_sub):
            k_next = k_ref[0, 0, pl.ds(off + (u + 1) * kb if u + 1 < n_sub else nxt, kb), :]
            vt = vt_ref[0, 0, :, pl.ds(off + u * kb, kb)]
            s_next = []
            for c in chains:
                s_next.append(lax.dot_general(k_next, q_ref[0, 0, c * FLASH_QB:(c + 1) * FLASH_QB, :],
                                              (((1,), (1,)), ((), ())), preferred_element_type=F32))
                if bounded:
                    p = jnp.exp2(s[c] - m[c])
                    acc[c] = acc[c] + jnp.dot(vt, p.astype(BF16), preferred_element_type=F32)
                    continue
                m_new = jnp.maximum(m[c], jnp.max(s[c], 0, keepdims=True))
                alpha = jnp.exp2(m[c] - m_new)
                p = jnp.exp2(s[c] - m_new)
                acc[c] = alpha * acc[c] + jnp.dot(vt, p.astype(BF16), preferred_element_type=F32)
                m[c] = m_new
            s = s_next
        for c in chains:
            m_sc[c], acc_sc[c], s_sc[c] = m[c], acc[c], s[c]
        return carry

    lax.fori_loop(0, nk, body, 0)
    for c in range(nq):
        o_ref[0, 0, :, c * FLASH_QB:(c + 1) * FLASH_QB] = acc_sc[c, :HD] / acc_sc[c, HD:HD + 1]


def _flash(q, k, vt, q0, n_q, k0, n_keys, tq, tk, bound=None):
    b, hq = q.shape[:2]
    hk = k.shape[1]
    rep = hq // hk
    assert n_q % tq == 0 and n_keys % tk == 0 and tq % FLASH_QB == 0
    kb = min(FLASH_KB, tk)
    assert tk % kb == 0
    assert q0 % tq == 0 and k0 % n_keys == 0
    hv = vt.shape[2]
    nq = tq // FLASH_QB
    qb0, kb0 = q0 // tq, k0 // n_keys

    def call(bounded):
        extra_specs = [pl.BlockSpec((1, FLASH_QB), lambda bi, h, i: (0, 0))] if bounded else []
        extra = [jnp.full((1, FLASH_QB), bound, F32)] if bounded else []
        return pl.pallas_call(
            functools.partial(_flash_kernel, tk=tk, nk=n_keys // tk, nq=nq, kb=kb, bounded=bounded),
            grid=(b, hq, n_q // tq),
            in_specs=[pl.BlockSpec((1, 1, tq, LANES), lambda bi, h, i: (bi, h, qb0 + i, 0)),
                      pl.BlockSpec((1, 1, n_keys, LANES), lambda bi, h, i: (bi, h // rep, kb0, 0)),
                      pl.BlockSpec((1, 1, hv, n_keys), lambda bi, h, i: (bi, h // rep, 0, kb0))] + extra_specs,
            out_specs=pl.BlockSpec((1, 1, HD, tq), lambda bi, h, i: (bi, h, 0, i)),
            out_shape=jax.ShapeDtypeStruct((b, hq, HD, n_q), F32),
            scratch_shapes=[pltpu.VMEM((nq, 1, FLASH_QB), F32), pltpu.VMEM((nq, hv, FLASH_QB), F32),
                            pltpu.VMEM((nq, kb, FLASH_QB), F32)],
            compiler_params=_params("parallel", "parallel", "arbitrary"),
            name="flash_attention_bounded" if bounded else "flash_attention",
        )(q, k, vt, *extra)

    if bound is None:
        return call(False)
    return lax.cond(bound <= FLASH_MAX_BOUND, lambda: call(True), lambda: call(False))


def _na_kernel(q_ref, k_ref, vt_ref, kc_ref, vtc_ref, bias_ref, o_ref, *, rows):
    kc = kc_ref[0, 0]
    vtc = vtc_ref[0, 0]
    nq, nk = NA_QROWS * GRID_W, NA_KROWS * GRID_W
    batches = range(NA_ROWS // NA_QROWS)
    r0 = [pl.program_id(2) * NA_ROWS + bi * NA_QROWS for bi in batches]
    koff = [pl.multiple_of(jnp.clip(r - NA_WIN_R // 2, 0, rows - NA_KROWS) * GRID_W, 2 * GRID_W) for r in r0]
    variant = [jnp.where(r == 0, 0, jnp.where(r == rows - NA_QROWS, 2, 1)) for r in r0]
    q = [q_ref[0, 0, bi * nq:(bi + 1) * nq, :] for bi in batches]
    nt = (((1,), (1,)), ((), ()))

    def scores(bi):
        s_w = lax.dot_general(k_ref[0, 0, pl.ds(koff[bi], nk), :], q[bi], nt, preferred_element_type=F32)
        return s_w + bias_ref[variant[bi], 0], lax.dot_general(kc, q[bi], nt, preferred_element_type=F32)

    s_next = scores(0)
    for bi in batches:
        s_w, s_c = s_next
        if bi + 1 < len(batches):
            s_next = scores(bi + 1)
        m = jnp.maximum(jnp.max(s_w, 0, keepdims=True), jnp.max(s_c, 0, keepdims=True))
        p_w = jnp.exp2(s_w - m).astype(BF16)
        p_c = jnp.exp2(s_c - m).astype(BF16)
        acc = (jnp.dot(vt_ref[0, 0, :, pl.ds(koff[bi], nk)], p_w, preferred_element_type=F32)
               + jnp.dot(vtc, p_c, preferred_element_type=F32))
        o_ref[0, 0, :, bi * nq:(bi + 1) * nq] = acc[:HD] / acc[HD:HD + 1]


def _na_bias_table(rpb):
    n_dr, n_dc = 2 * NA_WIN_R - 1, 2 * NA_WIN_C - 1
    sel_r = np.zeros((3, NA_KROWS, NA_QROWS, n_dr), np.float32)
    ok_r = np.zeros((3, NA_KROWS, NA_QROWS), bool)
    for kind in range(3):
        for i in range(NA_QROWS):
            start = (0, i, NA_KROWS - NA_WIN_R)[kind]
            shift = (0, -(NA_WIN_R // 2), NA_QROWS - NA_KROWS)[kind]
            for j in range(start, start + NA_WIN_R):
                sel_r[kind, j, i, j + shift - i + NA_WIN_R - 1] = 1.0
                ok_r[kind, j, i] = True
    cols = np.arange(GRID_W)
    cs = np.clip(cols - NA_WIN_C // 2, 0, GRID_W - NA_WIN_C)
    sel_c = np.zeros((GRID_W, GRID_W, n_dc), np.float32)
    ok_c = np.zeros((GRID_W, GRID_W), bool)
    for qc in range(GRID_W):
        for kcol in range(cs[qc], cs[qc] + NA_WIN_C):
            sel_c[kcol, qc, kcol - qc + NA_WIN_C - 1] = 1.0
            ok_c[kcol, qc] = True
    tab = jnp.einsum('hrc,vjir,kqc->vhjkiq', rpb, sel_r, sel_c, precision=HIGHEST)
    ok = ok_r[:, None, :, None, :, None] & ok_c[None, None, None, :, None, :]
    tab = jnp.where(ok, tab * LOG2E, MASKED_LOGIT)
    return tab.reshape(3, rpb.shape[0], NA_KROWS * GRID_W, NA_QROWS * GRID_W)


def _neighbourhood(q, k, vt, seq, n_ctx, bias):
    b, h = q.shape[:2]
    rows = seq // GRID_W
    assert rows >= NA_KROWS and rows % NA_ROWS == 0 and seq % n_ctx == 0
    tile = NA_ROWS * GRID_W
    hv = vt.shape[2]
    cb = seq // n_ctx
    return pl.pallas_call(
        functools.partial(_na_kernel, rows=rows),
        grid=(b, h, rows // NA_ROWS),
        in_specs=[pl.BlockSpec((1, 1, tile, LANES), lambda bi, hi, i: (bi, hi, i, 0)),
                  pl.BlockSpec((1, 1, seq, LANES), lambda bi, hi, i: (bi, hi, 0, 0)),
                  pl.BlockSpec((1, 1, hv, seq), lambda bi, hi, i: (bi, hi, 0, 0)),
                  pl.BlockSpec((1, 1, n_ctx, LANES), lambda bi, hi, i: (bi, hi, cb, 0)),
                  pl.BlockSpec((1, 1, hv, n_ctx), lambda bi, hi, i: (bi, hi, 0, cb)),
                  pl.BlockSpec((3, 1, NA_KROWS * GRID_W, NA_QROWS * GRID_W), lambda bi, hi, i: (0, hi, 0, 0))],
        out_specs=pl.BlockSpec((1, 1, HD, tile), lambda bi, hi, i: (bi, hi, 0, i)),
        out_shape=jax.ShapeDtypeStruct((b, h, HD, seq), F32),
        compiler_params=_params("parallel", "parallel", "arbitrary"),
        name="neighbourhood_attention",
    )(q, k, vt, k, vt, bias)


def _seg_sum(t, ones_blk):
    return _dot_exact_rhs(t, ones_blk)


def _rw_prep_kernel(p_ref, prev_ref, next_ref, mu_ref, kk_ref, ka_ref, rk_ref, w0_ref, w2_ref, a0_ref,
                    a2_ref, ones_ref, r_o, v_o, nkk_o, bonus_o, lw0_o, kd0_o, bd0_o, lw1_o, kd1_o, bd1_o,
                    *, n_lat_tiles, n_tiles):
    lw_o, kd_o, bd_o = (lw0_o, lw1_o), (kd0_o, kd1_o), (bd0_o, bd1_o)
    i = pl.program_id(1)
    u = p_ref[0, :, BR:]
    first = jnp.logical_or(i == 0, i == n_lat_tiles)
    last = jnp.logical_or(i == n_lat_tiles - 1, i == n_tiles - 1)
    prev_row = jnp.where(first, 0.0, prev_ref[0, F32_SUBLANES - 1:F32_SUBLANES, BR:])
    next_row = jnp.where(last, 0.0, next_ref[0, 0:1, BR:])
    row = lax.broadcasted_iota(jnp.int32, u.shape, 0)
    up = jnp.where(row == 0, prev_row, pltpu.roll(u, 1, 0))
    dn = jnp.where(row == TM - 1, next_row, pltpu.roll(u, TM - 1, 0))
    u = u + mu_ref[...] * (0.5 * (up + dn) - u)
    r, k, v = u[:, :BR], u[:, BR:2 * BR], u[:, 2 * BR:3 * BR]
    wl, al = u[:, 3 * BR:3 * BR + 2 * RW_LORA], u[:, 3 * BR + 2 * RW_LORA:]
    ones_blk = ones_ref[...]
    kk = k * kk_ref[...]
    kk = kk * lax.rsqrt(_seg_sum(kk * kk, ones_blk) + 1e-12)
    r_o[0] = r
    v_o[0] = v
    nkk_o[0] = -kk
    k_sum = jnp.zeros_like(k)
    for d in range(2):
        wl_d = wl[:, d * RW_LORA:(d + 1) * RW_LORA]
        al_d = al[:, d * RW_LORA:(d + 1) * RW_LORA]
        t = w0_ref[d] + _dot(jnp.tanh(wl_d), w2_ref[d])
        lw_o[d][0] = -RW_DECAY_SCALE * _sigmoid(t)
        a = _sigmoid(a0_ref[d] + _dot(al_d, a2_ref[d]))
        k_d = k * (1.0 + (a - 1.0) * ka_ref[...])
        kd_o[d][0] = k_d
        bd_o[d][0] = kk * a
        k_sum = k_sum + k_d
    bonus_o[0] = _seg_sum(r * k_sum * rk_ref[...], ones_blk) * v


def _rw_prep(pb, mu, kk, ka, rk, w0, w2, a0, a2, ones_blk, n_lat_tiles):
    b, n, ncol = pb.shape
    nt = n // TM
    r8 = TM // F32_SUBLANES
    full = lambda a: pl.BlockSpec(a.shape, lambda bi, i: (0,) * a.ndim)
    tok = pl.BlockSpec((1, TM, BR), lambda bi, i: (bi, i, 0))
    one = jax.ShapeDtypeStruct((b, n, BR), F32)
    return pl.pallas_call(
        functools.partial(_rw_prep_kernel, n_lat_tiles=n_lat_tiles, n_tiles=nt),
        grid=(b, nt),
        in_specs=[pl.BlockSpec((1, TM, ncol), lambda bi, i: (bi, i, 0)),
                  pl.BlockSpec((1, F32_SUBLANES, ncol), lambda bi, i: (bi, jnp.maximum(i * r8 - 1, 0), 0)),
                  pl.BlockSpec((1, F32_SUBLANES, ncol),
                               lambda bi, i: (bi, jnp.minimum((i + 1) * r8, n // F32_SUBLANES - 1), 0)),
                  full(mu), full(kk), full(ka), full(rk), full(w0), full(w2), full(a0), full(a2),
                  full(ones_blk)],
        out_specs=[tok] * 10,
        out_shape=[one] * 10,
        compiler_params=_params("parallel", "parallel"),
        name="rwkv_prep",
    )(pb, pb, pb, mu, kk, ka, rk, w0, w2, a0, a2, ones_blk)


def _rw_scan_kernel(r_ref, v_ref, a_ref, lw_ref, k_ref, b_ref, y_ref, st_ref, *, rev, n_b):
    c, w = CHUNK, RW_PACK * HD

    @pl.when(pl.program_id(0) == 0)
    def _():
        st_ref[...] = jnp.zeros(st_ref.shape, F32)

    tt = lax.broadcasted_iota(jnp.int32, (c, w), 0)
    ss = lax.broadcasted_iota(jnp.int32, (c, w), 1) & (c - 1)
    strict = (ss > tt) if rev else (ss < tt)
    incl = (ss >= tt) if rev else (ss <= tt)
    eye = (ss == tt).astype(F32)
    t2 = lax.broadcasted_iota(jnp.int32, (c, c), 0)
    s2 = lax.broadcasted_iota(jnp.int32, (c, c), 1)
    m_incl = ((s2 >= t2) if rev else (s2 <= t2)).astype(BF16)
    hd_log2 = HD.bit_length() - 1
    blk = ((lax.broadcasted_iota(jnp.int32, (w, w), 0) >> hd_log2)
           == (lax.broadcasted_iota(jnp.int32, (w, w), 1) >> hd_log2))

    def bd(t):
        tb = t.astype(BF16)
        return jnp.where(blk, jnp.concatenate([tb] * RW_PACK, axis=0), jnp.zeros((), BF16))

    def dtn(x, y):
        return lax.dot_general(x.astype(BF16), y.astype(BF16), (((0,), (0,)), ((), ())),
                               preferred_element_type=F32)

    chunks = tuple(range(RW_TT // c))
    chunks = chunks[::-1] if rev else chunks
    chains = [(bi, g) for bi in range(n_b) for g in range(BR // w)]
    units = [(ci, bi, g) for ci in chunks for (bi, g) in chains]
    sl = lambda ref, u: ref[u[1], u[0] * c:(u[0] + 1) * c, u[2] * w:(u[2] + 1) * w]

    lw = [sl(lw_ref, u) for u in units]
    cum = []
    for t in lw:
        hi, mid, lo = _split3(t)
        cum.append(jnp.dot(m_incl, hi, preferred_element_type=F32)
                   + jnp.dot(m_incl, mid, preferred_element_type=F32)
                   + jnp.dot(m_incl, lo, preferred_element_type=F32))
    tot = [jnp.sum(t, 0, keepdims=True) for t in lw]
    p_inv = [jnp.exp(-x) for x in cum]
    p_end = [jnp.exp(t - x) for t, x in zip(tot, cum)]
    a_t = [sl(a_ref, u) * jnp.exp(x - l) for u, x, l in zip(units, cum, lw)]
    r_t = [sl(r_ref, u) * jnp.exp(x) for u, x in zip(units, cum)]
    b_raw = [sl(b_ref, u) for u in units]
    k_raw = [sl(k_ref, u) for u in units]
    v = [sl(v_ref, u) for u in units]
    b_t = [x * p for x, p in zip(b_raw, p_inv)]
    k_t = [x * p for x, p in zip(k_raw, p_inv)]
    b_h = [x * p for x, p in zip(b_raw, p_end)]
    k_h = [x * p for x, p in zip(k_raw, p_end)]
    ar = [jnp.concatenate([x, y], axis=0) for x, y in zip(a_t, r_t)]
    g_b = [_dot_nt(x, bd(y)) for x, y in zip(ar, b_t)]
    g_k = [_dot_nt(x, bd(y)) for x, y in zip(ar, k_t)]
    lab = [jnp.where(strict, x[:c], 0.0) for x in g_b]
    lak = [jnp.where(strict, x[:c], 0.0) for x in g_k]
    qrb = [jnp.where(incl, x[c:], 0.0) for x in g_b]
    qrk = [jnp.where(incl, x[c:], 0.0) for x in g_k]
    tinv = [eye + x for x in lab]
    lp = lab
    stack = lambda x, y: jnp.concatenate([x, y], axis=0)
    for i in range(6):
        lp_bd = [bd(x) for x in lp]
        if i == 0:
            lp = [_dot(x, y) for x, y in zip(lp, lp_bd)]
        elif i < 5:
            both = [_dot(stack(t, x), y) for t, x, y in zip(tinv, lp, lp_bd)]
            tinv = [t + z[:c] for t, z in zip(tinv, both)]
            lp = [z[c:] for z in both]
        else:
            tinv = [t + _dot(t, y) for t, y in zip(tinv, lp_bd)]
    both = [_dot(stack(x, y), bd(z)) for x, y, z in zip(lak, qrk, v)]
    wv = [z[:c] for z in both]
    y_loc = [z[c:] for z in both]
    a_hat = [_dot(x, bd(y)) for x, y in zip(tinv, a_t)]
    u_hat = [_dot(x, bd(y)) for x, y in zip(tinv, wv)]
    ar_hat = [stack(x, y) for x, y in zip(a_hat, r_t)]
    bk_h = [stack(x, y) for x, y in zip(b_h, k_h)]
    p_c = [jnp.exp(t) for t in tot]

    n_ch = len(chains)
    for j in range(len(chunks)):
        idx = range(j * n_ch, (j + 1) * n_ch)
        st = [st_ref[q] for q in range(n_ch)]
        ur = [_dot_nt(ar_hat[i], s) for i, s in zip(idx, st)]
        u = [z[:c] + u_hat[i] for i, z in zip(idx, ur)]
        y = [z[c:] + _dot(qrb[i], bd(x)) + y_loc[i] for i, z, x in zip(idx, ur, u)]
        new = [s * p_c[i] + jnp.where(blk, dtn(stack(x, v[i]), bk_h[i]), 0.0) for i, s, x in zip(idx, st, u)]
        for q, i in enumerate(idx):
            y_ref[units[i][1], units[i][0] * c:(units[i][0] + 1) * c, units[i][2] * w:(units[i][2] + 1) * w] = y[q]
            st_ref[q] = new[q]


def _rw_scan(r, v, nkk, lw, kd, bd, n_ctx, rev):
    b, n, _ = r.shape
    assert n % RW_TT == 0 and n_ctx % RW_TT == 0
    nblk = n // RW_TT
    nctx = n_ctx // RW_TT
    if rev:
        blk_of = lambda i: nblk - 1 - i
    else:
        blk_of = lambda i: jnp.where(i < nctx, nblk - nctx + i, i - nctx)
    spec = pl.BlockSpec((b, RW_TT, BR), lambda i: (0, blk_of(i), 0))
    n_chains = b * (BR // (RW_PACK * HD))
    return pl.pallas_call(
        functools.partial(_rw_scan_kernel, rev=rev, n_b=b),
        grid=(nblk,),
        in_specs=[spec] * 6,
        out_specs=spec,
        out_shape=jax.ShapeDtypeStruct((b, n, BR), F32),
        scratch_shapes=[pltpu.VMEM((n_chains, RW_PACK * HD, RW_PACK * HD), F32)],
        compiler_params=_params("arbitrary"),
        name="rwkv_scan",
    )(r, v, nkk, lw, kd, bd)


def _merge_kernel(x_ref, h_ref, ya_ref, yac_ref, ybf_ref, ybr_ref, yc_ref, ycc_ref, yd_ref, ydc_ref,
                  ga_ref, gb_ref, gc_ref, gd_ref, bonus_ref, gnw_ref, gnb_ref, ones_ref, mgw_ref, mgb_ref,
                  wbr_ref, wout_ref, gt_ref, o_ref, *, n_lat_tiles):
    h = h_ref[0]
    ones_blk = ones_ref[...]
    is_ctx = pl.program_id(1) >= n_lat_tiles

    def token_major(lat_ref, ctx_ref):
        yt = jnp.where(is_ctx, ctx_ref[0], lat_ref[0])
        return yt.reshape(BR, TM).T

    yb = ybf_ref[0] + ybr_ref[0]
    mean = _seg_sum(yb, ones_blk) * (1.0 / HD)
    cen = yb - mean
    var = _seg_sum(cen * cen, ones_blk) * (1.0 / HD)
    yb = cen * lax.rsqrt(var + RW_GN_EPS) * gnw_ref[...] + gnb_ref[...] + bonus_ref[0]
    ys = (token_major(ya_ref, yac_ref), yb, token_major(yc_ref, ycc_ref), token_major(yd_ref, ydc_ref))
    gs = (ga_ref[0], gb_ref[0], gc_ref[0], gd_ref[0])
    acc = None
    for i in range(4):
        gate = _sigmoid(jnp.dot(h, mgw_ref[i], preferred_element_type=F32) + mgb_ref[i])
        term = gate * _dot(ys[i] * _silu(gs[i]), wbr_ref[i])
        acc = term if acc is None else acc + term
    o_ref[0] = x_ref[0] + gt_ref[0, 0] * _dot(acc, wout_ref[...])


def _merge(x_all, h, ya, ybf, ybr, yc, yd, pa, pb, pc, pd, bonus, gnw, gnb, ones_blk, mgw, mgb, wbr, wout, gt,
           n_lat_tiles, n_out):
    b, n, d = x_all.shape
    kind = lambda i: jnp.where(i < n_lat_tiles, 1, 0)
    tok = lambda w: pl.BlockSpec((1, TM, w), lambda bi, i: (bi, i, 0))
    full = lambda a: pl.BlockSpec(a.shape, lambda bi, i: (0,) * a.ndim)
    lat = pl.BlockSpec((1, N_HEADS, HD, TM), lambda bi, i: (bi, 0, 0, jnp.minimum(i, n_lat_tiles - 1)))
    ctx = pl.BlockSpec((1, N_HEADS, HD, TM), lambda bi, i: (bi, 0, 0, 0))
    return pl.pallas_call(
        functools.partial(_merge_kernel, n_lat_tiles=n_lat_tiles),
        grid=(b, n_out // TM),
        in_specs=[tok(d), tok(d), lat, ctx, tok(BR), tok(BR), lat, ctx, lat, ctx,
                  tok(BR), tok(BR), tok(BR), tok(BR), tok(BR),
                  full(gnw), full(gnb), full(ones_blk), full(mgw), full(mgb), full(wbr), full(wout),
                  pl.BlockSpec((1, 1, 1, d), lambda bi, i: (bi, kind(i), 0, 0))],
        out_specs=tok(d),
        out_shape=jax.ShapeDtypeStruct((b, n_out, d), F32),
        compiler_params=_params("parallel", "parallel"),
        name="merge_out",
    )(x_all, h, ya[0], ya[1], ybf, ybr, yc[0], yc[1], yd[0], yd[1], pa, pb, pc, pd, bonus,
      gnw, gnb, ones_blk, mgw, mgb, wbr, wout, gt)


def _pad_heads(w, n_heads, d):
    lead = w.shape[:-1]
    w = w.reshape(lead + (n_heads, d))
    w = jnp.pad(w, [(0, 0)] * len(lead) + [(0, 0), (0, LANES - d)])
    return w.reshape(lead + (n_heads * LANES,))


def _dup_heads(w, n_heads, d):
    lead = w.shape[:-1]
    w = w.reshape(lead + (n_heads, d))
    return jnp.concatenate([w, w], axis=-1).reshape(lead + (n_heads * 2 * d,))


def _pad_vec(g):
    return jnp.pad(g, (0, LANES - g.shape[0])).reshape(1, LANES)


def _rope_tables(n_lat, n_ctx, d_rot, lo, dup=False):
    t = np.arange(n_lat)
    row = (t // GRID_W).astype(np.float32)
    col = (t % GRID_W).astype(np.float32)
    n_freq = d_rot // 4
    inv = np.float32(ROPE_BASE) ** (-np.arange(n_freq, dtype=np.float32) / np.float32(n_freq))
    ang = np.concatenate([row[:, None] * inv, col[:, None] * inv], -1).astype(np.float32)
    cos, sin = np.cos(ang), np.sin(ang)
    half = d_rot // 2
    cos_t = np.ones((n_lat + n_ctx, LANES), np.float32)
    sin_t = np.zeros((n_lat + n_ctx, LANES), np.float32)
    for base in range(lo, LANES if dup else lo + 1, d_rot):
        cos_t[:n_lat, base:base + half] = cos
        cos_t[:n_lat, base + half:base + d_rot] = cos
        sin_t[:n_lat, base:base + half] = -sin
        sin_t[:n_lat, base + half:base + d_rot] = sin
    return jnp.asarray(cos_t), jnp.asarray(sin_t)


def _pick_tile(n, candidates):
    for c in candidates:
        if n % c == 0:
            return c
    raise ValueError(f"no tile for {n}")


def kernel(x, c, ctx, c_ctx, norm_g, mod_w, mod_b, w_in, na_qg, na_kg, na_rpb, rw_mu, rw_w0, rw_w2, rw_a0, rw_a2, rw_kk, rw_ka, rw_rk, rw_gn_w, rw_gn_b, mla_qa_g, mla_kva_g, mla_wuq, mla_wukv, mla_qg, mla_kg, gqa_qg, gqa_kg, mg_w, mg_b, w_br, w_out):
    bsz, seq, d = x.shape
    n_ctx = ctx.shape[1]
    depth = w_in.shape[0]
    assert d == D_MODEL and n_ctx == TM and seq % (NA_ROWS * GRID_W) == 0
    n = seq + n_ctx
    n_lat_tiles = seq // TM

    gqa_cos, gqa_sin = _rope_tables(seq, n_ctx, HD, 0, dup=True)
    mla_cos, mla_sin = _rope_tables(seq, n_ctx, MLA_ROPE, MLA_NOPE)
    ones_blk = jnp.asarray(np.kron(np.eye(N_HEADS), np.ones((HD, HD))), BF16)
    tq = _pick_tile(seq, (2048, 1024, 512, 256))
    tk = _pick_tile(n, (3328, 1280, 256))

    def context_attend(q, k, vt, need_ctx):
        if not need_ctx:
            return jnp.zeros((bsz, N_HEADS, HD, n_ctx), F32)
        return _flash(q, k, vt, seq, n_ctx, seq, n_ctx, n_ctx, n_ctx)

    def attend(q, k, vt, bound, need_ctx):
        return _flash(q, k, vt, 0, seq, 0, n, tq, tk, bound), context_attend(q, k, vt, need_ctx)

    def logit_bound(q_gain, k_gain, d_head):
        return (d_head ** 0.5 * LOG2E * ROUNDING_MARGIN) * jnp.max(jnp.abs(q_gain)) * jnp.max(jnp.abs(k_gain))

    x_all = jnp.concatenate([x, ctx], axis=1)
    cc = jnp.concatenate([c_ctx[None], c], axis=0)
    for l in range(depth):
        need_ctx = l + 1 < depth
        w = w_in[l]
        o = 0
        aq, ak, av, ag = (w[:, o + i * BR:o + (i + 1) * BR] for i in range(4))
        o += 4 * BR
        bu, bg = w[:, o:o + U_COLS], w[:, o + U_COLS:o + U_COLS + BR]
        o += U_COLS + BR
        ccq, cckv = w[:, o:o + MLA_RANK], w[:, o + MLA_RANK:o + 2 * MLA_RANK]
        ckr = w[:, o + 2 * MLA_RANK:o + 2 * MLA_RANK + MLA_ROPE]
        cg = w[:, o + 2 * MLA_RANK + MLA_ROPE:o + 2 * MLA_RANK + MLA_ROPE + BR]
        o += 2 * MLA_RANK + MLA_ROPE + BR
        dq = w[:, o:o + BR]
        dk = w[:, o + BR:o + BR + GQA_KV_HEADS * HD]
        dv = w[:, o + BR + GQA_KV_HEADS * HD:o + BR + 2 * GQA_KV_HEADS * HD]
        dg = w[:, o + BR + 2 * GQA_KV_HEADS * HD:]
        ckr_t = jnp.pad(ckr, ((0, 0), (MLA_NOPE, LANES - MLA_NOPE - MLA_ROPE)))
        w_a = jnp.concatenate([ag, _pad_heads(aq, N_HEADS, HD), _pad_heads(ak, N_HEADS, HD), av], 1).astype(BF16)
        w_b = jnp.concatenate([bg, bu], 1).astype(BF16)
        w_c = jnp.concatenate([cg, ccq, cckv, ckr_t], 1).astype(BF16)
        w_d = jnp.concatenate([dg, _dup_heads(dq, N_HEADS, HD), _dup_heads(dk, GQA_KV_HEADS, HD), dv], 1).astype(BF16)
        wuq = _pad_heads(mla_wuq[l], N_HEADS, MLA_NOPE + MLA_ROPE).astype(BF16)
        wukv = mla_wukv[l].reshape(MLA_RANK, N_HEADS, MLA_NOPE + HD)
        wuk = _pad_heads(wukv[:, :, :MLA_NOPE].reshape(MLA_RANK, -1), N_HEADS, MLA_NOPE).astype(BF16)
        wuv = wukv[:, :, MLA_NOPE:].reshape(MLA_RANK, -1).T.astype(BF16)

        mod = _modulation(cc, mod_w[l], mod_b[l])
        sh, sc, gt = jnp.split(mod, 3, axis=-1)
        pair = lambda m: jnp.stack([jnp.broadcast_to(m[0], (bsz, d)), m[1:]], axis=1)[:, :, None, :]
        sh2, sc2, gt2 = pair(sh), pair(sc), pair(gt)

        row = lambda t: t.reshape(1, -1)
        norm = (x_all, norm_g[l], sc2, sh2)

        ga, qa, ka, vat, h = _branch_proj(*norm, w_a, n_lat_tiles, True, _na_epilogue,
                                          (_pad_vec(na_qg[l]), _pad_vec(na_kg[l])), (), N_HEADS, "proj_na")
        ya = (_neighbourhood(qa, ka, vat, seq, n_ctx, _na_bias_table(na_rpb[l])),
              context_attend(qa, ka, vat, need_ctx))

        pb, = _norm_proj(*norm, w_b, n_lat_tiles, False)
        r, v, nkk, bonus, lw0, kd0, bd0, lw1, kd1, bd1 = _rw_prep(
            pb, row(rw_mu[l]), row(rw_kk[l]), row(rw_ka[l]), row(rw_rk[l]),
            rw_w0[l][:, None, :], rw_w2[l].astype(BF16), rw_a0[l][:, None, :], rw_a2[l].astype(BF16),
            ones_blk, n_lat_tiles)
        ybf = _rw_scan(r, v, nkk, lw0, kd0, bd0, n_ctx, False)
        ybr = _rw_scan(r, v, nkk, lw1, kd1, bd1, n_ctx, True)

        pc, = _norm_proj(*norm, w_c, n_lat_tiles, False)
        gc, qc, kc, vct = _branch_prep(pc, _mla_epilogue,
                                       (row(mla_qa_g[l]), row(mla_kva_g[l]), wuq, wuk, wuv,
                                        _pad_vec(mla_qg[l]), _pad_vec(mla_kg[l])), (mla_cos, mla_sin),
                                       N_HEADS, "mla_prep")
        yc = attend(qc, kc, vct, logit_bound(mla_qg[l], mla_kg[l], MLA_NOPE + MLA_ROPE), need_ctx)

        gd, qd, kd_, vdt = _branch_proj(*norm, w_d, n_lat_tiles, False, _gqa_epilogue,
                                        (jnp.tile(gqa_qg[l], 2)[None], jnp.tile(gqa_kg[l], 2)[None]), (gqa_cos, gqa_sin),
                                        GQA_KV_HEADS, "proj_gqa")
        yd = attend(qd, kd_, vdt, logit_bound(gqa_qg[l], gqa_kg[l], HD), need_ctx)

        n_out = n if need_ctx else seq
        x_all = _merge(x_all, h, ya, ybf, ybr, yc, yd, ga, pb, gc, gd, bonus,
                       row(rw_gn_w[l]), row(rw_gn_b[l]), ones_blk,
                       mg_w[l].astype(BF16), mg_b[l][:, None, :], w_br[l].astype(BF16), w_out[l].astype(BF16),
                       gt2, n_lat_tiles, n_out)
    return x_all
```

```python
import functools

import numpy as np
import jax
import jax.numpy as jnp
from jax import lax
from jax.experimental import pallas as pl
from jax.experimental.pallas import tpu as pltpu

F32 = jnp.float32
BF16 = jnp.bfloat16
HIGHEST = lax.Precision.HIGHEST

D_MODEL = 1024
GRID_W = 64
BR = 512
HD = 64
N_HEADS = BR // HD
EPS = 1e-6
ROPE_BASE = 10000.0
NA_WIN_R = 8
NA_WIN_C = 16
RW_LORA = 64
RW_GN_EPS = 64e-5
RW_DECAY_SCALE = float(np.exp(-0.5))
MLA_RANK = 256
MLA_NOPE = 64
MLA_ROPE = 32
GQA_KV_HEADS = 2
U_COLS = 3 * BR + 4 * RW_LORA

LANES = 128
F32_SUBLANES = 8
BF16_SUBLANES = 16
V_ROWS = HD + BF16_SUBLANES
VMEM_LIMIT = 56 * 1024 * 1024

TM = 256
CHUNK = 64
RW_TT = 256
RW_PACK = 4
FLASH_QB = 256
FLASH_KB = 256
FLASH_MAX_BOUND = 50.0
LOG2E = 1.4426950408889634
NA_ROWS = 32
NA_QROWS = 4
NA_KROWS = 12
MASKED_LOGIT = -1e30
ROUNDING_MARGIN = 1.02


def _params(*sem):
    return pltpu.CompilerParams(dimension_semantics=sem, vmem_limit_bytes=VMEM_LIMIT)


def _dot(a, b):
    return jnp.dot(a.astype(BF16), b.astype(BF16), preferred_element_type=F32)


def _dot_nt(a, b):
    return lax.dot_general(a.astype(BF16), b.astype(BF16), (((1,), (1,)), ((), ())),
                           preferred_element_type=F32)


def _split3(t):
    hi = t.astype(BF16)
    r1 = t - hi.astype(F32)
    mid = r1.astype(BF16)
    return hi, mid, (r1 - mid.astype(F32)).astype(BF16)


def _dot_exact_rhs(a, b):
    hi, mid, lo = _split3(a)
    return (jnp.dot(hi, b, preferred_element_type=F32) + jnp.dot(mid, b, preferred_element_type=F32)
            + jnp.dot(lo, b, preferred_element_type=F32))


def _silu(t):
    return t / (1.0 + jnp.exp(-t))


def _sigmoid(t):
    return 1.0 / (1.0 + jnp.exp(-t))


def _mod_kernel(c_ref, w_ref, b_ref, o_ref):
    o_ref[...] = _dot(_silu(c_ref[...]), w_ref[...]) + b_ref[...]


def _modulation(cc, w, b):
    n = cc.shape[0]
    return pl.pallas_call(
        _mod_kernel,
        out_shape=jax.ShapeDtypeStruct((n, w.shape[1]), F32),
        compiler_params=_params(),
        name="adaln_mod",
    )(cc, w, b.reshape(1, -1))


def _norm_proj_kernel(x_ref, g_ref, sc_ref, sh_ref, w_ref, *refs, n_extra, want_h, epilogue):
    x = x_ref[0]
    xn = x * lax.rsqrt(jnp.mean(x * x, -1, keepdims=True) + EPS) * g_ref[...]
    h = (xn * (1.0 + sc_ref[0, 0]) + sh_ref[0, 0]).astype(BF16)
    p = jnp.dot(h, w_ref[...], preferred_element_type=F32)
    extra, outs = refs[:n_extra], refs[n_extra:]
    if want_h:
        outs[-1][0] = h
        outs = outs[:-1]
    epilogue(p, extra, outs)


def _store_projection(p, extra, outs):
    outs[0][0] = p


def _norm_proj_pair(x_all, g, sc, sh, w_first, w_second, n_lat_tiles):
    b, n, _ = x_all.shape
    split = w_first.shape[1]
    assert split % LANES == 0

    def store_pair(p, extra, outs):
        outs[0][0] = p[:, :split]
        outs[1][0] = p[:, split:]

    widths = (split, w_second.shape[1])
    return _norm_proj(x_all, g, sc, sh, jnp.concatenate([w_first, w_second], 1), n_lat_tiles, False,
                      epilogue=store_pair,
                      out_specs=[pl.BlockSpec((1, TM, c), lambda bi, i: (bi, i, 0)) for c in widths],
                      out_shape=[jax.ShapeDtypeStruct((b, n, c), F32) for c in widths], name="norm_proj_pair")


def _norm_proj(x_all, g, sc, sh, w, n_lat_tiles, want_h, epilogue=_store_projection, extra=(), extra_specs=(),
               out_specs=None, out_shape=None, name="norm_proj"):
    b, n, d = x_all.shape
    ncol = w.shape[1]
    kind = lambda i: jnp.where(i < n_lat_tiles, 1, 0)
    if out_specs is None:
        out_shape = [jax.ShapeDtypeStruct((b, n, ncol), F32)]
        out_specs = [pl.BlockSpec((1, TM, ncol), lambda bi, i: (bi, i, 0))]
    out_shape, out_specs = list(out_shape), list(out_specs)
    if want_h:
        out_shape.append(jax.ShapeDtypeStruct((b, n, d), BF16))
        out_specs.append(pl.BlockSpec((1, TM, d), lambda bi, i: (bi, i, 0)))
    return pl.pallas_call(
        functools.partial(_norm_proj_kernel, n_extra=len(extra), want_h=want_h, epilogue=epilogue),
        grid=(b, n // TM),
        in_specs=[
            pl.BlockSpec((1, TM, d), lambda bi, i: (bi, i, 0)),
            pl.BlockSpec((1, d), lambda bi, i: (0, 0)),
            pl.BlockSpec((1, 1, 1, d), lambda bi, i: (bi, kind(i), 0, 0)),
            pl.BlockSpec((1, 1, 1, d), lambda bi, i: (bi, kind(i), 0, 0)),
            pl.BlockSpec((d, ncol), lambda bi, i: (0, 0)),
        ] + list(extra_specs),
        out_specs=out_specs,
        out_shape=out_shape,
        compiler_params=_params("parallel", "parallel"),
        name=name,
    )(x_all, g.reshape(1, d), sc, sh, w, *extra)


def _head_norm(xh, gain, inv_d):
    ms = jnp.sum(xh * xh, -1, keepdims=True) * inv_d
    return xh * lax.rsqrt(ms + EPS) * gain


def _rope(xh, cos, sin, lo, half):
    lane = lax.broadcasted_iota(jnp.int32, xh.shape, 1)
    swapped = jnp.where(lane < lo + half, pltpu.roll(xh, LANES - half, 1), pltpu.roll(xh, half, 1))
    return xh * cos + swapped * sin


def _store_values_t(vt_ref, vt, n_heads):
    lead = lax.broadcasted_iota(jnp.int32, (V_ROWS - HD, vt.shape[1]), 0) == 0
    tail = jnp.where(lead, 1.0, 0.0).astype(BF16)
    for h in range(n_heads):
        vt_ref[0, h, :HD] = vt[h * HD:(h + 1) * HD].astype(BF16)
        vt_ref[0, h, HD:] = tail


def _head_specs(b, n, n_kv):
    hm = lambda bi, i: (bi, 0, i, 0)
    vm = lambda bi, i: (bi, 0, 0, i)
    specs = [pl.BlockSpec((1, TM, BR), lambda bi, i: (bi, i, 0)),
             pl.BlockSpec((1, N_HEADS, TM, LANES), hm),
             pl.BlockSpec((1, n_kv, TM, LANES), hm),
             pl.BlockSpec((1, n_kv, V_ROWS, TM), vm)]
    shapes = [jax.ShapeDtypeStruct((b, n, BR), F32),
              jax.ShapeDtypeStruct((b, N_HEADS, n, LANES), BF16),
              jax.ShapeDtypeStruct((b, n_kv, n, LANES), BF16),
              jax.ShapeDtypeStruct((b, n_kv, V_ROWS, n), BF16)]
    return specs, shapes


def _na_epilogue(p, extra, outs):
    qg_ref, kg_ref = extra
    g_ref, q_ref, k_ref, vt_ref = outs
    qoff, koff, voff = BR, BR + N_HEADS * LANES, BR + 2 * N_HEADS * LANES
    g_ref[0] = p[:, :BR]
    for h in range(N_HEADS):
        qh = p[:, qoff + h * LANES:qoff + (h + 1) * LANES]
        kh = p[:, koff + h * LANES:koff + (h + 1) * LANES]
        q_ref[0, h] = (_head_norm(qh, qg_ref[...], 1.0 / HD) * (HD ** -0.5 * LOG2E)).astype(BF16)
        k_ref[0, h] = _head_norm(kh, kg_ref[...], 1.0 / HD).astype(BF16)
    _store_values_t(vt_ref, p[:, voff:voff + BR].T, N_HEADS)


def _gqa_epilogue(p, extra, outs):
    qg_ref, kg_ref, cos_ref, sin_ref = extra
    g_ref, q_ref, k_ref, vt_ref = outs
    qoff, koff = BR, BR + N_HEADS * LANES
    voff = koff + GQA_KV_HEADS * LANES
    cos, sin = cos_ref[...], sin_ref[...]
    g_ref[0] = p[:, :BR]
    rope = lambda t: t * cos + pltpu.roll(t, HD // 2, 1) * sin
    for h in range(N_HEADS):
        qh = _head_norm(p[:, qoff + h * LANES:qoff + (h + 1) * LANES], qg_ref[...], 1.0 / LANES)
        q_ref[0, h] = (rope(qh) * (0.5 * HD ** -0.5 * LOG2E)).astype(BF16)
    for h in range(GQA_KV_HEADS):
        kh = _head_norm(p[:, koff + h * LANES:koff + (h + 1) * LANES], kg_ref[...], 1.0 / LANES)
        k_ref[0, h] = rope(kh).astype(BF16)
    _store_values_t(vt_ref, p[:, voff:voff + GQA_KV_HEADS * HD].T, GQA_KV_HEADS)


def _mla_epilogue(p, extra, outs):
    qa_ref, kva_ref, wuq_ref, wuk_ref, wuv_ref, qg_ref, kg_ref, cos_ref, sin_ref = extra
    g_ref, q_ref, k_ref, vt_ref = outs
    d_qk = MLA_NOPE + MLA_ROPE
    g_ref[0] = p[:, :BR]
    groups = [slice(0, TM // 2), slice(TM // 2, TM)]
    rms = lambda t, gain: t * lax.rsqrt(jnp.mean(t * t, -1, keepdims=True) + EPS) * gain
    cqn = [rms(p[r, BR:BR + MLA_RANK], qa_ref[...]) for r in groups]
    ckvn = [rms(p[r, BR + MLA_RANK:BR + 2 * MLA_RANK], kva_ref[...]) for r in groups]
    qf = [_dot(t, wuq_ref[...]) for t in cqn]
    kf = [_dot(t, wuk_ref[...]) for t in ckvn]
    kr = [p[r, BR + 2 * MLA_RANK:BR + 2 * MLA_RANK + LANES] for r in groups]
    vtf = [_dot_nt(wuv_ref[...], t) for t in ckvn]
    cos = [cos_ref[r, :] for r in groups]
    sin = [sin_ref[r, :] for r in groups]
    for h in range(N_HEADS):
        qh = [_head_norm(t[:, h * LANES:(h + 1) * LANES], qg_ref[...], 1.0 / d_qk) for t in qf]
        kh = [_head_norm(t[:, h * LANES:(h + 1) * LANES] + x, kg_ref[...], 1.0 / d_qk) for t, x in zip(kf, kr)]
        for r, x, y, c, s in zip(groups, qh, kh, cos, sin):
            q_ref[0, h, r] = (_rope(x, c, s, MLA_NOPE, MLA_ROPE // 2) * (d_qk ** -0.5 * LOG2E)).astype(BF16)
            k_ref[0, h, r] = _rope(y, c, s, MLA_NOPE, MLA_ROPE // 2).astype(BF16)
    _store_values_t(vt_ref, jnp.concatenate(vtf, axis=1), N_HEADS)


def _prep_kernel(p_ref, *refs, n_extra, epilogue):
    epilogue(p_ref[0], refs[:n_extra], refs[n_extra:])


def _branch_prep(p, epilogue, consts, tables, n_kv, name):
    b, n, ncol = p.shape
    full = lambda a: pl.BlockSpec(a.shape, lambda bi, i: (0,) * a.ndim)
    tab = pl.BlockSpec((TM, LANES), lambda bi, i: (i, 0))
    specs, shapes = _head_specs(b, n, n_kv)
    return pl.pallas_call(
        functools.partial(_prep_kernel, n_extra=len(consts) + len(tables), epilogue=epilogue),
        grid=(b, n // TM),
        in_specs=[pl.BlockSpec((1, TM, ncol), lambda bi, i: (bi, i, 0))]
        + [full(a) for a in consts] + [tab] * len(tables),
        out_specs=specs,
        out_shape=shapes,
        compiler_params=_params("parallel", "parallel"),
        name=name,
    )(p, *consts, *tables)


def _branch_proj(x_all, g, sc, sh, w, n_lat_tiles, want_h, epilogue, consts, tables, n_kv, name):
    b, n, _ = x_all.shape
    full = lambda a: pl.BlockSpec(a.shape, lambda bi, i: (0,) * a.ndim)
    tab = pl.BlockSpec((TM, LANES), lambda bi, i: (i, 0))
    specs, shapes = _head_specs(b, n, n_kv)
    return _norm_proj(x_all, g, sc, sh, w, n_lat_tiles, want_h, epilogue=epilogue,
                      extra=tuple(consts) + tuple(tables),
                      extra_specs=[full(a) for a in consts] + [tab] * len(tables),
                      out_specs=specs, out_shape=shapes, name=name)


def _flash_kernel(q_ref, k_ref, vt_ref, *rest, tk, nk, nq, kb, bounded):
    if bounded:
        bound_ref, o_ref, m_sc, acc_sc, s_sc = rest
        m_sc[...] = jnp.broadcast_to(bound_ref[...], m_sc.shape)
    else:
        o_ref, m_sc, acc_sc, s_sc = rest
        m_sc[...] = jnp.full(m_sc.shape, -jnp.inf, F32)
    acc_sc[...] = jnp.zeros(acc_sc.shape, F32)

    n_sub = tk // kb
    chains = range(nq)

    def scores(row0):
        k = k_ref[0, 0, pl.ds(row0, kb), :]
        return [lax.dot_general(k, q_ref[0, 0, c * FLASH_QB:(c + 1) * FLASH_QB, :],
                                (((1,), (1,)), ((), ())), preferred_element_type=F32) for c in chains]

    for c, s0 in zip(chains, scores(0)):
        s_sc[c] = s0

    def body(j, carry):
        off = pl.multiple_of(j * tk, tk)
        nxt = pl.multiple_of(jnp.minimum(j + 1, nk - 1) * tk, tk)
        m = [m_sc[c] for c in chains]
        acc = [acc_sc[c] for c in chains]
        s = [s_sc[c] for c in chains]
        for u in range(n_sub):
            k_next = k_ref[0, 0, pl.ds(off + (u + 1) * kb if u + 1 < n_sub else nxt, kb), :]
            vt = vt_ref[0, 0, :, pl.ds(off + u * kb, kb)]
            s_next = []
            for c in chains:
                s_next.append(lax.dot_general(k_next, q_ref[0, 0, c * FLASH_QB:(c + 1) * FLASH_QB, :],
                                              (((1,), (1,)), ((), ())), preferred_element_type=F32))
                if bounded:
                    p = jnp.exp2(s[c] - m[c])
                    acc[c] = acc[c] + jnp.dot(vt, p.astype(BF16), preferred_element_type=F32)
                    continue
                m_new = jnp.maximum(m[c], jnp.max(s[c], 0, keepdims=True))
                alpha = jnp.exp2(m[c] - m_new)
                p = jnp.exp2(s[c] - m_new)
                acc[c] = alpha * acc[c] + jnp.dot(vt, p.astype(BF16), preferred_element_type=F32)
                m[c] = m_new
            s = s_next
        for c in chains:
            m_sc[c], acc_sc[c], s_sc[c] = m[c], acc[c], s[c]
        return carry

    lax.fori_loop(0, nk, body, 0)
    for c in range(nq):
        o_ref[0, 0, :, c * FLASH_QB:(c + 1) * FLASH_QB] = acc_sc[c, :HD] / acc_sc[c, HD:HD + 1]


def _flash(q, k, vt, q0, n_q, k0, n_keys, tq, tk, bound=None):
    b, hq = q.shape[:2]
    hk = k.shape[1]
    rep = hq // hk
    assert n_q % tq == 0 and n_keys % tk == 0 and tq % FLASH_QB == 0
    kb = min(FLASH_KB, tk)
    assert tk % kb == 0
    assert q0 % tq == 0 and k0 % n_keys == 0
    hv = vt.shape[2]
    nq = tq // FLASH_QB
    qb0, kb0 = q0 // tq, k0 // n_keys

    def call(bounded):
        extra_specs = [pl.BlockSpec((1, FLASH_QB), lambda bi, h, i: (0, 0))] if bounded else []
        extra = [jnp.full((1, FLASH_QB), bound, F32)] if bounded else []
        return pl.pallas_call(
            functools.partial(_flash_kernel, tk=tk, nk=n_keys // tk, nq=nq, kb=kb, bounded=bounded),
            grid=(b, hq, n_q // tq),
            in_specs=[pl.BlockSpec((1, 1, tq, LANES), lambda bi, h, i: (bi, h, qb0 + i, 0)),
                      pl.BlockSpec((1, 1, n_keys, LANES), lambda bi, h, i: (bi, h // rep, kb0, 0)),
                      pl.BlockSpec((1, 1, hv, n_keys), lambda bi, h, i: (bi, h // rep, 0, kb0))] + extra_specs,
            out_specs=pl.BlockSpec((1, 1, HD, tq), lambda bi, h, i: (bi, h, 0, i)),
            out_shape=jax.ShapeDtypeStruct((b, hq, HD, n_q), F32),
            scratch_shapes=[pltpu.VMEM((nq, 1, FLASH_QB), F32), pltpu.VMEM((nq, hv, FLASH_QB), F32),
                            pltpu.VMEM((nq, kb, FLASH_QB), F32)],
            compiler_params=_params("parallel", "parallel", "arbitrary"),
            name="flash_attention_bounded" if bounded else "flash_attention",
        )(q, k, vt, *extra)

    if bound is None:
        return call(False)
    return lax.cond(bound <= FLASH_MAX_BOUND, lambda: call(True), lambda: call(False))


def _na_kernel(q_ref, k_ref, vt_ref, kc_ref, vtc_ref, bias_ref, o_ref, *, rows):
    kc = kc_ref[0, 0]
    vtc = vtc_ref[0, 0]
    nq, nk = NA_QROWS * GRID_W, NA_KROWS * GRID_W
    batches = range(NA_ROWS // NA_QROWS)
    r0 = [pl.program_id(2) * NA_ROWS + bi * NA_QROWS for bi in batches]
    koff = [pl.multiple_of(jnp.clip(r - NA_WIN_R // 2, 0, rows - NA_KROWS) * GRID_W, 2 * GRID_W) for r in r0]
    variant = [jnp.where(r == 0, 0, jnp.where(r == rows - NA_QROWS, 2, 1)) for r in r0]
    q = [q_ref[0, 0, bi * nq:(bi + 1) * nq, :] for bi in batches]
    nt = (((1,), (1,)), ((), ()))

    def scores(bi):
        s_w = lax.dot_general(k_ref[0, 0, pl.ds(koff[bi], nk), :], q[bi], nt, preferred_element_type=F32)
        return s_w + bias_ref[variant[bi], 0], lax.dot_general(kc, q[bi], nt, preferred_element_type=F32)

    s_next = scores(0)
    for bi in batches:
        s_w, s_c = s_next
        if bi + 1 < len(batches):
            s_next = scores(bi + 1)
        m = jnp.maximum(jnp.max(s_w, 0, keepdims=True), jnp.max(s_c, 0, keepdims=True))
        p_w = jnp.exp2(s_w - m).astype(BF16)
        p_c = jnp.exp2(s_c - m).astype(BF16)
        acc = (jnp.dot(vt_ref[0, 0, :, pl.ds(koff[bi], nk)], p_w, preferred_element_type=F32)
               + jnp.dot(vtc, p_c, preferred_element_type=F32))
        o_ref[0, 0, :, bi * nq:(bi + 1) * nq] = acc[:HD] / acc[HD:HD + 1]


def _na_bias_table(rpb):
    n_dr, n_dc = 2 * NA_WIN_R - 1, 2 * NA_WIN_C - 1
    sel_r = np.zeros((3, NA_KROWS, NA_QROWS, n_dr), np.float32)
    ok_r = np.zeros((3, NA_KROWS, NA_QROWS), bool)
    for kind in range(3):
        for i in range(NA_QROWS):
            start = (0, i, NA_KROWS - NA_WIN_R)[kind]
            shift = (0, -(NA_WIN_R // 2), NA_QROWS - NA_KROWS)[kind]
            for j in range(start, start + NA_WIN_R):
                sel_r[kind, j, i, j + shift - i + NA_WIN_R - 1] = 1.0
                ok_r[kind, j, i] = True
    cols = np.arange(GRID_W)
    cs = np.clip(cols - NA_WIN_C // 2, 0, GRID_W - NA_WIN_C)
    sel_c = np.zeros((GRID_W, GRID_W, n_dc), np.float32)
    ok_c = np.zeros((GRID_W, GRID_W), bool)
    for qc in range(GRID_W):
        for kcol in range(cs[qc], cs[qc] + NA_WIN_C):
            sel_c[kcol, qc, kcol - qc + NA_WIN_C - 1] = 1.0
            ok_c[kcol, qc] = True
    tab = jnp.einsum('hrc,vjir,kqc->vhjkiq', rpb, sel_r, sel_c, precision=HIGHEST)
    ok = ok_r[:, None, :, None, :, None] & ok_c[None, None, None, :, None, :]
    tab = jnp.where(ok, tab * LOG2E, MASKED_LOGIT)
    return tab.reshape(3, rpb.shape[0], NA_KROWS * GRID_W, NA_QROWS * GRID_W)


def _neighbourhood(q, k, vt, seq, n_ctx, bias):
    b, h = q.shape[:2]
    rows = seq // GRID_W
    assert rows >= NA_KROWS and rows % NA_ROWS == 0 and seq % n_ctx == 0
    tile = NA_ROWS * GRID_W
    hv = vt.shape[2]
    cb = seq // n_ctx
    return pl.pallas_call(
        functools.partial(_na_kernel, rows=rows),
        grid=(b, h, rows // NA_ROWS),
        in_specs=[pl.BlockSpec((1, 1, tile, LANES), lambda bi, hi, i: (bi, hi, i, 0)),
                  pl.BlockSpec((1, 1, seq, LANES), lambda bi, hi, i: (bi, hi, 0, 0)),
                  pl.BlockSpec((1, 1, hv, seq), lambda bi, hi, i: (bi, hi, 0, 0)),
                  pl.BlockSpec((1, 1, n_ctx, LANES), lambda bi, hi, i: (bi, hi, cb, 0)),
                  pl.BlockSpec((1, 1, hv, n_ctx), lambda bi, hi, i: (bi, hi, 0, cb)),
                  pl.BlockSpec((3, 1, NA_KROWS * GRID_W, NA_QROWS * GRID_W), lambda bi, hi, i: (0, hi, 0, 0))],
        out_specs=pl.BlockSpec((1, 1, HD, tile), lambda bi, hi, i: (bi, hi, 0, i)),
        out_shape=jax.ShapeDtypeStruct((b, h, HD, seq), F32),
        compiler_params=_params("parallel", "parallel", "arbitrary"),
        name="neighbourhood_attention",
    )(q, k, vt, k, vt, bias)


def _seg_sum(t, ones_blk):
    return _dot_exact_rhs(t, ones_blk)


def _rw_prep_kernel(p_ref, prev_ref, next_ref, mu_ref, kk_ref, ka_ref, rk_ref, w0_ref, w2_ref, a0_ref,
                    a2_ref, ones_ref, r_o, v_o, nkk_o, bonus_o, lw0_o, kd0_o, bd0_o, lw1_o, kd1_o, bd1_o,
                    *, n_lat_tiles, n_tiles):
    lw_o, kd_o, bd_o = (lw0_o, lw1_o), (kd0_o, kd1_o), (bd0_o, bd1_o)
    i = pl.program_id(1)
    u = p_ref[0, :, BR:]
    first = jnp.logical_or(i == 0, i == n_lat_tiles)
    last = jnp.logical_or(i == n_lat_tiles - 1, i == n_tiles - 1)
    prev_row = jnp.where(first, 0.0, prev_ref[0, F32_SUBLANES - 1:F32_SUBLANES, BR:])
    next_row = jnp.where(last, 0.0, next_ref[0, 0:1, BR:])
    row = lax.broadcasted_iota(jnp.int32, u.shape, 0)
    up = jnp.where(row == 0, prev_row, pltpu.roll(u, 1, 0))
    dn = jnp.where(row == TM - 1, next_row, pltpu.roll(u, TM - 1, 0))
    u = u + mu_ref[...] * (0.5 * (up + dn) - u)
    r, k, v = u[:, :BR], u[:, BR:2 * BR], u[:, 2 * BR:3 * BR]
    wl, al = u[:, 3 * BR:3 * BR + 2 * RW_LORA], u[:, 3 * BR + 2 * RW_LORA:]
    ones_blk = ones_ref[...]
    kk = k * kk_ref[...]
    kk = kk * lax.rsqrt(_seg_sum(kk * kk, ones_blk) + 1e-12)
    r_o[0] = r
    v_o[0] = v
    nkk_o[0] = -kk
    k_sum = jnp.zeros_like(k)
    for d in range(2):
        wl_d = wl[:, d * RW_LORA:(d + 1) * RW_LORA]
        al_d = al[:, d * RW_LORA:(d + 1) * RW_LORA]
        t = w0_ref[d] + _dot(jnp.tanh(wl_d), w2_ref[d])
        lw_o[d][0] = -RW_DECAY_SCALE * _sigmoid(t)
        a = _sigmoid(a0_ref[d] + _dot(al_d, a2_ref[d]))
        k_d = k * (1.0 + (a - 1.0) * ka_ref[...])
        kd_o[d][0] = k_d
        bd_o[d][0] = kk * a
        k_sum = k_sum + k_d
    bonus_o[0] = _seg_sum(r * k_sum * rk_ref[...], ones_blk) * v


def _rw_prep(pb, mu, kk, ka, rk, w0, w2, a0, a2, ones_blk, n_lat_tiles):
    b, n, ncol = pb.shape
    nt = n // TM
    r8 = TM // F32_SUBLANES
    full = lambda a: pl.BlockSpec(a.shape, lambda bi, i: (0,) * a.ndim)
    tok = pl.BlockSpec((1, TM, BR), lambda bi, i: (bi, i, 0))
    one = jax.ShapeDtypeStruct((b, n, BR), F32)
    return pl.pallas_call(
        functools.partial(_rw_prep_kernel, n_lat_tiles=n_lat_tiles, n_tiles=nt),
        grid=(b, nt),
        in_specs=[pl.BlockSpec((1, TM, ncol), lambda bi, i: (bi, i, 0)),
                  pl.BlockSpec((1, F32_SUBLANES, ncol), lambda bi, i: (bi, jnp.maximum(i * r8 - 1, 0), 0)),
                  pl.BlockSpec((1, F32_SUBLANES, ncol),
                               lambda bi, i: (bi, jnp.minimum((i + 1) * r8, n // F32_SUBLANES - 1), 0)),
                  full(mu), full(kk), full(ka), full(rk), full(w0), full(w2), full(a0), full(a2),
                  full(ones_blk)],
        out_specs=[tok] * 10,
        out_shape=[one] * 10,
        compiler_params=_params("parallel", "parallel"),
        name="rwkv_prep",
    )(pb, pb, pb, mu, kk, ka, rk, w0, w2, a0, a2, ones_blk)


def _rw_scan_kernel(r_ref, v_ref, a_ref, lw_ref, k_ref, b_ref, y_ref, st_ref, *, rev, n_b):
    c, w = CHUNK, RW_PACK * HD

    @pl.when(pl.program_id(0) == 0)
    def _():
        st_ref[...] = jnp.zeros(st_ref.shape, F32)

    tt = lax.broadcasted_iota(jnp.int32, (c, w), 0)
    ss = lax.broadcasted_iota(jnp.int32, (c, w), 1) & (c - 1)
    strict = (ss > tt) if rev else (ss < tt)
    incl = (ss >= tt) if rev else (ss <= tt)
    eye = (ss == tt).astype(F32)
    t2 = lax.broadcasted_iota(jnp.int32, (c, c), 0)
    s2 = lax.broadcasted_iota(jnp.int32, (c, c), 1)
    m_incl = ((s2 >= t2) if rev else (s2 <= t2)).astype(BF16)
    hd_log2 = HD.bit_length() - 1
    blk = ((lax.broadcasted_iota(jnp.int32, (w, w), 0) >> hd_log2)
           == (lax.broadcasted_iota(jnp.int32, (w, w), 1) >> hd_log2))

    def bd(t):
        tb = t.astype(BF16)
        return jnp.where(blk, jnp.concatenate([tb] * RW_PACK, axis=0), jnp.zeros((), BF16))

    def dtn(x, y):
        return lax.dot_general(x.astype(BF16), y.astype(BF16), (((0,), (0,)), ((), ())),
                               preferred_element_type=F32)

    chunks = tuple(range(RW_TT // c))
    chunks = chunks[::-1] if rev else chunks
    chains = [(bi, g) for bi in range(n_b) for g in range(BR // w)]
    units = [(ci, bi, g) for ci in chunks for (bi, g) in chains]
    sl = lambda ref, u: ref[u[1], u[0] * c:(u[0] + 1) * c, u[2] * w:(u[2] + 1) * w]

    lw = [sl(lw_ref, u) for u in units]
    cum = []
    for t in lw:
        hi, mid, lo = _split3(t)
        cum.append(jnp.dot(m_incl, hi, preferred_element_type=F32)
                   + jnp.dot(m_incl, mid, preferred_element_type=F32)
                   + jnp.dot(m_incl, lo, preferred_element_type=F32))
    tot = [jnp.sum(t, 0, keepdims=True) for t in lw]
    p_inv = [jnp.exp(-x) for x in cum]
    p_end = [jnp.exp(t - x) for t, x in zip(tot, cum)]
    a_t = [sl(a_ref, u) * jnp.exp(x - l) for u, x, l in zip(units, cum, lw)]
    r_t = [sl(r_ref, u) * jnp.exp(x) for u, x in zip(units, cum)]
    b_raw = [sl(b_ref, u) for u in units]
    k_raw = [sl(k_ref, u) for u in units]
    v = [sl(v_ref, u) for u in units]
    b_t = [x * p for x, p in zip(b_raw, p_inv)]
    k_t = [x * p for x, p in zip(k_raw, p_inv)]
    b_h = [x * p for x, p in zip(b_raw, p_end)]
    k_h = [x * p for x, p in zip(k_raw, p_end)]
    ar = [jnp.concatenate([x, y], axis=0) for x, y in zip(a_t, r_t)]
    g_b = [_dot_nt(x, bd(y)) for x, y in zip(ar, b_t)]
    g_k = [_dot_nt(x, bd(y)) for x, y in zip(ar, k_t)]
    lab = [jnp.where(strict, x[:c], 0.0) for x in g_b]
    lak = [jnp.where(strict, x[:c], 0.0) for x in g_k]
    qrb = [jnp.where(incl, x[c:], 0.0) for x in g_b]
    qrk = [jnp.where(incl, x[c:], 0.0) for x in g_k]
    tinv = [eye + x for x in lab]
    lp = lab
    stack = lambda x, y: jnp.concatenate([x, y], axis=0)
    for i in range(6):
        lp_bd = [bd(x) for x in lp]
        if i == 0:
            lp = [_dot(x, y) for x, y in zip(lp, lp_bd)]
        elif i < 5:
            both = [_dot(stack(t, x), y) for t, x, y in zip(tinv, lp, lp_bd)]
            tinv = [t + z[:c] for t, z in zip(tinv, both)]
            lp = [z[c:] for z in both]
        else:
            tinv = [t + _dot(t, y) for t, y in zip(tinv, lp_bd)]
    both = [_dot(stack(x, y), bd(z)) for x, y, z in zip(lak, qrk, v)]
    wv = [z[:c] for z in both]
    y_loc = [z[c:] for z in both]
    a_hat = [_dot(x, bd(y)) for x, y in zip(tinv, a_t)]
    u_hat = [_dot(x, bd(y)) for x, y in zip(tinv, wv)]
    ar_hat = [stack(x, y) for x, y in zip(a_hat, r_t)]
    bk_h = [stack(x, y) for x, y in zip(b_h, k_h)]
    p_c = [jnp.exp(t) for t in tot]

    n_ch = len(chains)
    for j in range(len(chunks)):
        idx = range(j * n_ch, (j + 1) * n_ch)
        st = [st_ref[q] for q in range(n_ch)]
        ur = [_dot_nt(ar_hat[i], s) for i, s in zip(idx, st)]
        u = [z[:c] + u_hat[i] for i, z in zip(idx, ur)]
        y = [z[c:] + _dot(qrb[i], bd(x)) + y_loc[i] for i, z, x in zip(idx, ur, u)]
        new = [s * p_c[i] + jnp.where(blk, dtn(stack(x, v[i]), bk_h[i]), 0.0) for i, s, x in zip(idx, st, u)]
        for q, i in enumerate(idx):
            y_ref[units[i][1], units[i][0] * c:(units[i][0] + 1) * c, units[i][2] * w:(units[i][2] + 1) * w] = y[q]
            st_ref[q] = new[q]


def _rw_scan(r, v, nkk, lw, kd, bd, n_ctx, rev):
    b, n, _ = r.shape
    assert n % RW_TT == 0 and n_ctx % RW_TT == 0
    nblk = n // RW_TT
    nctx = n_ctx // RW_TT
    if rev:
        blk_of = lambda i: nblk - 1 - i
    else:
        blk_of = lambda i: jnp.where(i < nctx, nblk - nctx + i, i - nctx)
    spec = pl.BlockSpec((b, RW_TT, BR), lambda i: (0, blk_of(i), 0))
    n_chains = b * (BR // (RW_PACK * HD))
    return pl.pallas_call(
        functools.partial(_rw_scan_kernel, rev=rev, n_b=b),
        grid=(nblk,),
        in_specs=[spec] * 6,
        out_specs=spec,
        out_shape=jax.ShapeDtypeStruct((b, n, BR), F32),
        scratch_shapes=[pltpu.VMEM((n_chains, RW_PACK * HD, RW_PACK * HD), F32)],
        compiler_params=_params("arbitrary"),
        name="rwkv_scan",
    )(r, v, nkk, lw, kd, bd)


def _merge_kernel(x_ref, h_ref, ya_ref, yac_ref, ybf_ref, ybr_ref, yc_ref, ycc_ref, yd_ref, ydc_ref,
                  ga_ref, gb_ref, gc_ref, gd_ref, bonus_ref, gnw_ref, gnb_ref, ones_ref, mgw_ref, mgb_ref,
                  wbr_ref, wout_ref, gt_ref, o_ref, *, n_lat_tiles):
    h = h_ref[0]
    ones_blk = ones_ref[...]
    is_ctx = pl.program_id(1) >= n_lat_tiles

    def token_major(lat_ref, ctx_ref):
        yt = jnp.where(is_ctx, ctx_ref[0], lat_ref[0])
        return yt.reshape(BR, TM).T

    yb = ybf_ref[0] + ybr_ref[0]
    mean = _seg_sum(yb, ones_blk) * (1.0 / HD)
    cen = yb - mean
    var = _seg_sum(cen * cen, ones_blk) * (1.0 / HD)
    yb = cen * lax.rsqrt(var + RW_GN_EPS) * gnw_ref[...] + gnb_ref[...] + bonus_ref[0]
    ys = (token_major(ya_ref, yac_ref), yb, token_major(yc_ref, ycc_ref), token_major(yd_ref, ydc_ref))
    gs = (ga_ref[0], gb_ref[0], gc_ref[0], gd_ref[0])
    acc = None
    for i in range(4):
        gate = _sigmoid(jnp.dot(h, mgw_ref[i], preferred_element_type=F32) + mgb_ref[i])
        term = gate * _dot(ys[i] * _silu(gs[i]), wbr_ref[i])
        acc = term if acc is None else acc + term
    o_ref[0] = x_ref[0] + gt_ref[0, 0] * _dot(acc, wout_ref[...])


def _merge(x_all, h, ya, ybf, ybr, yc, yd, pa, pb, pc, pd, bonus, gnw, gnb, ones_blk, mgw, mgb, wbr, wout, gt,
           n_lat_tiles, n_out):
    b, n, d = x_all.shape
    kind = lambda i: jnp.where(i < n_lat_tiles, 1, 0)
    tok = lambda w: pl.BlockSpec((1, TM, w), lambda bi, i: (bi, i, 0))
    full = lambda a: pl.BlockSpec(a.shape, lambda bi, i: (0,) * a.ndim)
    lat = pl.BlockSpec((1, N_HEADS, HD, TM), lambda bi, i: (bi, 0, 0, jnp.minimum(i, n_lat_tiles - 1)))
    ctx = pl.BlockSpec((1, N_HEADS, HD, TM), lambda bi, i: (bi, 0, 0, 0))
    return pl.pallas_call(
        functools.partial(_merge_kernel, n_lat_tiles=n_lat_tiles),
        grid=(b, n_out // TM),
        in_specs=[tok(d), tok(d), lat, ctx, tok(BR), tok(BR), lat, ctx, lat, ctx,
                  tok(BR), tok(BR), tok(BR), tok(BR), tok(BR),
                  full(gnw), full(gnb), full(ones_blk), full(mgw), full(mgb), full(wbr), full(wout),
                  pl.BlockSpec((1, 1, 1, d), lambda bi, i: (bi, kind(i), 0, 0))],
        out_specs=tok(d),
        out_shape=jax.ShapeDtypeStruct((b, n_out, d), F32),
        compiler_params=_params("parallel", "parallel"),
        name="merge_out",
    )(x_all, h, ya[0], ya[1], ybf, ybr, yc[0], yc[1], yd[0], yd[1], pa, pb, pc, pd, bonus,
      gnw, gnb, ones_blk, mgw, mgb, wbr, wout, gt)


def _pad_heads(w, n_heads, d):
    lead = w.shape[:-1]
    w = w.reshape(lead + (n_heads, d))
    w = jnp.pad(w, [(0, 0)] * len(lead) + [(0, 0), (0, LANES - d)])
    return w.reshape(lead + (n_heads * LANES,))


def _dup_heads(w, n_heads, d):
    lead = w.shape[:-1]
    w = w.reshape(lead + (n_heads, d))
    return jnp.concatenate([w, w], axis=-1).reshape(lead + (n_heads * 2 * d,))


def _pad_vec(g):
    return jnp.pad(g, (0, LANES - g.shape[0])).reshape(1, LANES)


def _rope_tables(n_lat, n_ctx, d_rot, lo, dup=False):
    t = np.arange(n_lat)
    row = (t // GRID_W).astype(np.float32)
    col = (t % GRID_W).astype(np.float32)
    n_freq = d_rot // 4
    inv = np.float32(ROPE_BASE) ** (-np.arange(n_freq, dtype=np.float32) / np.float32(n_freq))
    ang = np.concatenate([row[:, None] * inv, col[:, None] * inv], -1).astype(np.float32)
    cos, sin = np.cos(ang), np.sin(ang)
    half = d_rot // 2
    cos_t = np.ones((n_lat + n_ctx, LANES), np.float32)
    sin_t = np.zeros((n_lat + n_ctx, LANES), np.float32)
    for base in range(lo, LANES if dup else lo + 1, d_rot):
        cos_t[:n_lat, base:base + half] = cos
        cos_t[:n_lat, base + half:base + d_rot] = cos
        sin_t[:n_lat, base:base + half] = -sin
        sin_t[:n_lat, base + half:base + d_rot] = sin
    return jnp.asarray(cos_t), jnp.asarray(sin_t)


def _pick_tile(n, candidates):
    for c in candidates:
        if n % c == 0:
            return c
    raise ValueError(f"no tile for {n}")


def kernel(x, c, ctx, c_ctx, norm_g, mod_w, mod_b, w_in, na_qg, na_kg, na_rpb, rw_mu, rw_w0, rw_w2, rw_a0, rw_a2, rw_kk, rw_ka, rw_rk, rw_gn_w, rw_gn_b, mla_qa_g, mla_kva_g, mla_wuq, mla_wukv, mla_qg, mla_kg, gqa_qg, gqa_kg, mg_w, mg_b, w_br, w_out):
    bsz, seq, d = x.shape
    n_ctx = ctx.shape[1]
    depth = w_in.shape[0]
    assert d == D_MODEL and n_ctx == TM and seq % (NA_ROWS * GRID_W) == 0
    n = seq + n_ctx
    n_lat_tiles = seq // TM

    gqa_cos, gqa_sin = _rope_tables(seq, n_ctx, HD, 0, dup=True)
    mla_cos, mla_sin = _rope_tables(seq, n_ctx, MLA_ROPE, MLA_NOPE)
    ones_blk = jnp.asarray(np.kron(np.eye(N_HEADS), np.ones((HD, HD))), BF16)
    tq = _pick_tile(seq, (2048, 1024, 512, 256))
    tk = _pick_tile(n, (3328, 1280, 256))

    def context_attend(q, k, vt, need_ctx):
        if not need_ctx:
            return jnp.zeros((bsz, N_HEADS, HD, n_ctx), F32)
        return _flash(q, k, vt, seq, n_ctx, seq, n_ctx, n_ctx, n_ctx)

    def attend(q, k, vt, bound, need_ctx):
        return _flash(q, k, vt, 0, seq, 0, n, tq, tk, bound), context_attend(q, k, vt, need_ctx)

    def logit_bound(q_gain, k_gain, d_head):
        return (d_head ** 0.5 * LOG2E * ROUNDING_MARGIN) * jnp.max(jnp.abs(q_gain)) * jnp.max(jnp.abs(k_gain))

    x_all = jnp.concatenate([x, ctx], axis=1)
    cc = jnp.concatenate([c_ctx[None], c], axis=0)
    for l in range(depth):
        need_ctx = l + 1 < depth
        w = w_in[l]
        o = 0
        aq, ak, av, ag = (w[:, o + i * BR:o + (i + 1) * BR] for i in range(4))
        o += 4 * BR
        bu, bg = w[:, o:o + U_COLS], w[:, o + U_COLS:o + U_COLS + BR]
        o += U_COLS + BR
        ccq, cckv = w[:, o:o + MLA_RANK], w[:, o + MLA_RANK:o + 2 * MLA_RANK]
        ckr = w[:, o + 2 * MLA_RANK:o + 2 * MLA_RANK + MLA_ROPE]
        cg = w[:, o + 2 * MLA_RANK + MLA_ROPE:o + 2 * MLA_RANK + MLA_ROPE + BR]
        o += 2 * MLA_RANK + MLA_ROPE + BR
        dq = w[:, o:o + BR]
        dk = w[:, o + BR:o + BR + GQA_KV_HEADS * HD]
        dv = w[:, o + BR + GQA_KV_HEADS * HD:o + BR + 2 * GQA_KV_HEADS * HD]
        dg = w[:, o + BR + 2 * GQA_KV_HEADS * HD:]
        ckr_t = jnp.pad(ckr, ((0, 0), (MLA_NOPE, LANES - MLA_NOPE - MLA_ROPE)))
        w_a = jnp.concatenate([ag, _pad_heads(aq, N_HEADS, HD), _pad_heads(ak, N_HEADS, HD), av], 1).astype(BF16)
        w_b = jnp.concatenate([bg, bu], 1).astype(BF16)
        w_c = jnp.concatenate([cg, ccq, cckv, ckr_t], 1).astype(BF16)
        w_d = jnp.concatenate([dg, _dup_heads(dq, N_HEADS, HD), _dup_heads(dk, GQA_KV_HEADS, HD), dv], 1).astype(BF16)
        wuq = _pad_heads(mla_wuq[l], N_HEADS, MLA_NOPE + MLA_ROPE).astype(BF16)
        wukv = mla_wukv[l].reshape(MLA_RANK, N_HEADS, MLA_NOPE + HD)
        wuk = _pad_heads(wukv[:, :, :MLA_NOPE].reshape(MLA_RANK, -1), N_HEADS, MLA_NOPE).astype(BF16)
        wuv = wukv[:, :, MLA_NOPE:].reshape(MLA_RANK, -1).T.astype(BF16)

        mod = _modulation(cc, mod_w[l], mod_b[l])
        sh, sc, gt = jnp.split(mod, 3, axis=-1)
        pair = lambda m: jnp.stack([jnp.broadcast_to(m[0], (bsz, d)), m[1:]], axis=1)[:, :, None, :]
        sh2, sc2, gt2 = pair(sh), pair(sc), pair(gt)

        row = lambda t: t.reshape(1, -1)
        norm = (x_all, norm_g[l], sc2, sh2)

        ga, qa, ka, vat, h = _branch_proj(*norm, w_a, n_lat_tiles, True, _na_epilogue,
                                          (_pad_vec(na_qg[l]), _pad_vec(na_kg[l])), (), N_HEADS, "proj_na")
        ya = (_neighbourhood(qa, ka, vat, seq, n_ctx, _na_bias_table(na_rpb[l])),
              context_attend(qa, ka, vat, need_ctx))

        pb, pc = _norm_proj_pair(*norm, w_b, w_c, n_lat_tiles)
        r, v, nkk, bonus, lw0, kd0, bd0, lw1, kd1, bd1 = _rw_prep(
            pb, row(rw_mu[l]), row(rw_kk[l]), row(rw_ka[l]), row(rw_rk[l]),
            rw_w0[l][:, None, :], rw_w2[l].astype(BF16), rw_a0[l][:, None, :], rw_a2[l].astype(BF16),
            ones_blk, n_lat_tiles)
        ybf = _rw_scan(r, v, nkk, lw0, kd0, bd0, n_ctx, False)
        ybr = _rw_scan(r, v, nkk, lw1, kd1, bd1, n_ctx, True)

        gc, qc, kc, vct = _branch_prep(pc, _mla_epilogue,
                                       (row(mla_qa_g[l]), row(mla_kva_g[l]), wuq, wuk, wuv,
                                        _pad_vec(mla_qg[l]), _pad_vec(mla_kg[l])), (mla_cos, mla_sin),
                                       N_HEADS, "mla_prep")
        yc = attend(qc, kc, vct, logit_bound(mla_qg[l], mla_kg[l], MLA_NOPE + MLA_ROPE), need_ctx)

        gd, qd, kd_, vdt = _branch_proj(*norm, w_d, n_lat_tiles, False, _gqa_epilogue,
                                        (jnp.tile(gqa_qg[l], 2)[None], jnp.tile(gqa_kg[l], 2)[None]), (gqa_cos, gqa_sin),
                                        GQA_KV_HEADS, "proj_gqa")
        yd = attend(qd, kd_, vdt, logit_bound(gqa_qg[l], gqa_kg[l], HD), need_ctx)

        n_out = n if need_ctx else seq
        x_all = _merge(x_all, h, ya, ybf, ybr, yc, yd, ga, pb, gc, gd, bonus,
                       row(rw_gn_w[l]), row(rw_gn_b[l]), ones_blk,
                       mg_w[l].astype(BF16), mg_b[l][:, None, :], w_br[l].astype(BF16), w_out[l].astype(BF16),
                       gt2, n_lat_tiles, n_out)
    return x_all
```

```python
import functools

import numpy as np
import jax
import jax.numpy as jnp
from jax import lax
from jax.experimental import pallas as pl
from jax.experimental.pallas import tpu as pltpu

F32 = jnp.float32
BF16 = jnp.bfloat16
HIGHEST = lax.Precision.HIGHEST

D_MODEL = 1024
GRID_W = 64
BR = 512
HD = 64
N_HEADS = BR // HD
EPS = 1e-6
ROPE_BASE = 10000.0
NA_WIN_R = 8
NA_WIN_C = 16
RW_LORA = 64
RW_GN_EPS = 64e-5
RW_DECAY_SCALE = float(np.exp(-0.5))
MLA_RANK = 256
MLA_NOPE = 64
MLA_ROPE = 32
GQA_KV_HEADS = 2
U_COLS = 3 * BR + 4 * RW_LORA

LANES = 128
F32_SUBLANES = 8
BF16_SUBLANES = 16
V_ROWS = HD + BF16_SUBLANES
VMEM_LIMIT = 56 * 1024 * 1024

TM = 256
CHUNK = 64
RW_TT = 256
RW_PACK = 4
FLASH_QB = 256
FLASH_KB = 256
FLASH_MAX_BOUND = 50.0
LOG2E = 1.4426950408889634
NA_ROWS = 32
NA_QROWS = 4
NA_KROWS = 12
MASKED_LOGIT = -1e30
ROUNDING_MARGIN = 1.02


def _params(*sem):
    return pltpu.CompilerParams(dimension_semantics=sem, vmem_limit_bytes=VMEM_LIMIT)


def _dot(a, b):
    return jnp.dot(a.astype(BF16), b.astype(BF16), preferred_element_type=F32)


def _dot_nt(a, b):
    return lax.dot_general(a.astype(BF16), b.astype(BF16), (((1,), (1,)), ((), ())),
                           preferred_element_type=F32)


def _split3(t):
    hi = t.astype(BF16)
    r1 = t - hi.astype(F32)
    mid = r1.astype(BF16)
    return hi, mid, (r1 - mid.astype(F32)).astype(BF16)


def _dot_exact_rhs(a, b):
    hi, mid, lo = _split3(a)
    return (jnp.dot(hi, b, preferred_element_type=F32) + jnp.dot(mid, b, preferred_element_type=F32)
            + jnp.dot(lo, b, preferred_element_type=F32))


def _silu(t):
    return t / (1.0 + jnp.exp(-t))


def _sigmoid(t):
    return 1.0 / (1.0 + jnp.exp(-t))


def _mod_kernel(c_ref, w_ref, b_ref, o_ref):
    o_ref[...] = _dot(_silu(c_ref[...]), w_ref[...]) + b_ref[...]


def _modulation(cc, w, b):
    n = cc.shape[0]
    return pl.pallas_call(
        _mod_kernel,
        out_shape=jax.ShapeDtypeStruct((n, w.shape[1]), F32),
        compiler_params=_params(),
        name="adaln_mod",
    )(cc, w, b.reshape(1, -1))


def _norm_proj_kernel(x_ref, g_ref, sc_ref, sh_ref, w_ref, *refs, n_extra, want_h, epilogue):
    x = x_ref[0]
    xn = x * lax.rsqrt(jnp.mean(x * x, -1, keepdims=True) + EPS) * g_ref[...]
    h = (xn * (1.0 + sc_ref[0, 0]) + sh_ref[0, 0]).astype(BF16)
    p = jnp.dot(h, w_ref[...], preferred_element_type=F32)
    extra, outs = refs[:n_extra], refs[n_extra:]
    if want_h:
        outs[-1][0] = h
        outs = outs[:-1]
    epilogue(p, extra, outs)


def _store_projection(p, extra, outs):
    outs[0][0] = p


def _norm_proj_pair(x_all, g, sc, sh, w_first, w_second, n_lat_tiles):
    b, n, _ = x_all.shape
    split = w_first.shape[1]
    assert split % LANES == 0

    def store_pair(p, extra, outs):
        outs[0][0] = p[:, :split]
        outs[1][0] = p[:, split:]

    widths = (split, w_second.shape[1])
    return _norm_proj(x_all, g, sc, sh, jnp.concatenate([w_first, w_second], 1), n_lat_tiles, False,
                      epilogue=store_pair,
                      out_specs=[pl.BlockSpec((1, TM, c), lambda bi, i: (bi, i, 0)) for c in widths],
                      out_shape=[jax.ShapeDtypeStruct((b, n, c), F32) for c in widths], name="norm_proj_pair")


def _norm_proj(x_all, g, sc, sh, w, n_lat_tiles, want_h, epilogue=_store_projection, extra=(), extra_specs=(),
               out_specs=None, out_shape=None, name="norm_proj"):
    b, n, d = x_all.shape
    ncol = w.shape[1]
    kind = lambda i: jnp.where(i < n_lat_tiles, 1, 0)
    if out_specs is None:
        out_shape = [jax.ShapeDtypeStruct((b, n, ncol), F32)]
        out_specs = [pl.BlockSpec((1, TM, ncol), lambda bi, i: (bi, i, 0))]
    out_shape, out_specs = list(out_shape), list(out_specs)
    if want_h:
        out_shape.append(jax.ShapeDtypeStruct((b, n, d), BF16))
        out_specs.append(pl.BlockSpec((1, TM, d), lambda bi, i: (bi, i, 0)))
    return pl.pallas_call(
        functools.partial(_norm_proj_kernel, n_extra=len(extra), want_h=want_h, epilogue=epilogue),
        grid=(b, n // TM),
        in_specs=[
            pl.BlockSpec((1, TM, d), lambda bi, i: (bi, i, 0)),
            pl.BlockSpec((1, d), lambda bi, i: (0, 0)),
            pl.BlockSpec((1, 1, 1, d), lambda bi, i: (bi, kind(i), 0, 0)),
            pl.BlockSpec((1, 1, 1, d), lambda bi, i: (bi, kind(i), 0, 0)),
            pl.BlockSpec((d, ncol), lambda bi, i: (0, 0)),
        ] + list(extra_specs),
        out_specs=out_specs,
        out_shape=out_shape,
        compiler_params=_params("parallel", "parallel"),
        name=name,
    )(x_all, g.reshape(1, d), sc, sh, w, *extra)


def _head_norm(xh, gain, inv_d):
    ms = jnp.sum(xh * xh, -1, keepdims=True) * inv_d
    return xh * lax.rsqrt(ms + EPS) * gain


def _rope(xh, cos, sin, lo, half):
    lane = lax.broadcasted_iota(jnp.int32, xh.shape, 1)
    swapped = jnp.where(lane < lo + half, pltpu.roll(xh, LANES - half, 1), pltpu.roll(xh, half, 1))
    return xh * cos + swapped * sin


def _store_values_t(vt_ref, vt, n_heads):
    lead = lax.broadcasted_iota(jnp.int32, (V_ROWS - HD, vt.shape[1]), 0) == 0
    tail = jnp.where(lead, 1.0, 0.0).astype(BF16)
    for h in range(n_heads):
        vt_ref[0, h, :HD] = vt[h * HD:(h + 1) * HD].astype(BF16)
        vt_ref[0, h, HD:] = tail


def _head_specs(b, n, n_kv):
    hm = lambda bi, i: (bi, 0, i, 0)
    vm = lambda bi, i: (bi, 0, 0, i)
    specs = [pl.BlockSpec((1, TM, BR), lambda bi, i: (bi, i, 0)),
             pl.BlockSpec((1, N_HEADS, TM, LANES), hm),
             pl.BlockSpec((1, n_kv, TM, LANES), hm),
             pl.BlockSpec((1, n_kv, V_ROWS, TM), vm)]
    shapes = [jax.ShapeDtypeStruct((b, n, BR), F32),
              jax.ShapeDtypeStruct((b, N_HEADS, n, LANES), BF16),
              jax.ShapeDtypeStruct((b, n_kv, n, LANES), BF16),
              jax.ShapeDtypeStruct((b, n_kv, V_ROWS, n), BF16)]
    return specs, shapes


def _na_epilogue(p, extra, outs):
    qg_ref, kg_ref = extra
    g_ref, q_ref, k_ref, vt_ref = outs
    qoff, koff, voff = BR, BR + N_HEADS * LANES, BR + 2 * N_HEADS * LANES
    g_ref[0] = p[:, :BR]
    for h in range(N_HEADS):
        qh = p[:, qoff + h * LANES:qoff + (h + 1) * LANES]
        kh = p[:, koff + h * LANES:koff + (h + 1) * LANES]
        q_ref[0, h] = (_head_norm(qh, qg_ref[...], 1.0 / HD) * (HD ** -0.5 * LOG2E)).astype(BF16)
        k_ref[0, h] = _head_norm(kh, kg_ref[...], 1.0 / HD).astype(BF16)
    _store_values_t(vt_ref, p[:, voff:voff + BR].T, N_HEADS)


def _gqa_epilogue(p, extra, outs):
    qg_ref, kg_ref, cos_ref, sin_ref = extra
    g_ref, q_ref, k_ref, vt_ref = outs
    qoff, koff = BR, BR + N_HEADS * LANES
    voff = koff + GQA_KV_HEADS * LANES
    cos, sin = cos_ref[...], sin_ref[...]
    g_ref[0] = p[:, :BR]
    rope = lambda t: t * cos + pltpu.roll(t, HD // 2, 1) * sin
    for h in range(N_HEADS):
        qh = _head_norm(p[:, qoff + h * LANES:qoff + (h + 1) * LANES], qg_ref[...], 1.0 / LANES)
        q_ref[0, h] = (rope(qh) * (0.5 * HD ** -0.5 * LOG2E)).astype(BF16)
    for h in range(GQA_KV_HEADS):
        kh = _head_norm(p[:, koff + h * LANES:koff + (h + 1) * LANES], kg_ref[...], 1.0 / LANES)
        k_ref[0, h] = rope(kh).astype(BF16)
    _store_values_t(vt_ref, p[:, voff:voff + GQA_KV_HEADS * HD].T, GQA_KV_HEADS)


def _mla_epilogue(p, extra, outs):
    qa_ref, kva_ref, wuq_ref, wuk_ref, wuv_ref, qg_ref, kg_ref, cos_ref, sin_ref = extra
    q_ref, k_ref, vt_ref = outs
    d_qk = MLA_NOPE + MLA_ROPE
    groups = [slice(0, TM // 2), slice(TM // 2, TM)]
    rms = lambda t, gain: t * lax.rsqrt(jnp.mean(t * t, -1, keepdims=True) + EPS) * gain
    cqn = [rms(p[r, BR:BR + MLA_RANK], qa_ref[...]) for r in groups]
    ckvn = [rms(p[r, BR + MLA_RANK:BR + 2 * MLA_RANK], kva_ref[...]) for r in groups]
    qf = [_dot(t, wuq_ref[...]) for t in cqn]
    kf = [_dot(t, wuk_ref[...]) for t in ckvn]
    kr = [p[r, BR + 2 * MLA_RANK:BR + 2 * MLA_RANK + LANES] for r in groups]
    vtf = [_dot_nt(wuv_ref[...], t) for t in ckvn]
    cos = [cos_ref[r, :] for r in groups]
    sin = [sin_ref[r, :] for r in groups]
    for h in range(N_HEADS):
        qh = [_head_norm(t[:, h * LANES:(h + 1) * LANES], qg_ref[...], 1.0 / d_qk) for t in qf]
        kh = [_head_norm(t[:, h * LANES:(h + 1) * LANES] + x, kg_ref[...], 1.0 / d_qk) for t, x in zip(kf, kr)]
        for r, x, y, c, s in zip(groups, qh, kh, cos, sin):
            q_ref[0, h, r] = (_rope(x, c, s, MLA_NOPE, MLA_ROPE // 2) * (d_qk ** -0.5 * LOG2E)).astype(BF16)
            k_ref[0, h, r] = _rope(y, c, s, MLA_NOPE, MLA_ROPE // 2).astype(BF16)
    _store_values_t(vt_ref, jnp.concatenate(vtf, axis=1), N_HEADS)


def _prep_kernel(p_ref, *refs, n_extra, epilogue):
    epilogue(p_ref[0], refs[:n_extra], refs[n_extra:])


def _branch_prep(p, epilogue, consts, tables, n_kv, name):
    b, n, ncol = p.shape
    full = lambda a: pl.BlockSpec(a.shape, lambda bi, i: (0,) * a.ndim)
    tab = pl.BlockSpec((TM, LANES), lambda bi, i: (i, 0))
    specs, shapes = _head_specs(b, n, n_kv)
    specs, shapes = specs[1:], shapes[1:]
    return pl.pallas_call(
        functools.partial(_prep_kernel, n_extra=len(consts) + len(tables), epilogue=epilogue),
        grid=(b, n // TM),
        in_specs=[pl.BlockSpec((1, TM, ncol), lambda bi, i: (bi, i, 0))]
        + [full(a) for a in consts] + [tab] * len(tables),
        out_specs=specs,
        out_shape=shapes,
        compiler_params=_params("parallel", "parallel"),
        name=name,
    )(p, *consts, *tables)


def _branch_proj(x_all, g, sc, sh, w, n_lat_tiles, want_h, epilogue, consts, tables, n_kv, name):
    b, n, _ = x_all.shape
    full = lambda a: pl.BlockSpec(a.shape, lambda bi, i: (0,) * a.ndim)
    tab = pl.BlockSpec((TM, LANES), lambda bi, i: (i, 0))
    specs, shapes = _head_specs(b, n, n_kv)
    return _norm_proj(x_all, g, sc, sh, w, n_lat_tiles, want_h, epilogue=epilogue,
                      extra=tuple(consts) + tuple(tables),
                      extra_specs=[full(a) for a in consts] + [tab] * len(tables),
                      out_specs=specs, out_shape=shapes, name=name)


def _flash_kernel(q_ref, k_ref, vt_ref, *rest, tk, nk, nq, kb, bounded):
    if bounded:
        bound_ref, o_ref, m_sc, acc_sc, s_sc = rest
        m_sc[...] = jnp.broadcast_to(bound_ref[...], m_sc.shape)
    else:
        o_ref, m_sc, acc_sc, s_sc = rest
        m_sc[...] = jnp.full(m_sc.shape, -jnp.inf, F32)
    acc_sc[...] = jnp.zeros(acc_sc.shape, F32)

    n_sub = tk // kb
    chains = range(nq)

    def scores(row0):
        k = k_ref[0, 0, pl.ds(row0, kb), :]
        return [lax.dot_general(k, q_ref[0, 0, c * FLASH_QB:(c + 1) * FLASH_QB, :],
                                (((1,), (1,)), ((), ())), preferred_element_type=F32) for c in chains]

    for c, s0 in zip(chains, scores(0)):
        s_sc[c] = s0

    def body(j, carry):
        off = pl.multiple_of(j * tk, tk)
        nxt = pl.multiple_of(jnp.minimum(j + 1, nk - 1) * tk, tk)
        m = [m_sc[c] for c in chains]
        acc = [acc_sc[c] for c in chains]
        s = [s_sc[c] for c in chains]
        for u in range(n_sub):
            k_next = k_ref[0, 0, pl.ds(off + (u + 1) * kb if u + 1 < n_sub else nxt, kb), :]
            vt = vt_ref[0, 0, :, pl.ds(off + u * kb, kb)]
            s_next = []
            for c in chains:
                s_next.append(lax.dot_general(k_next, q_ref[0, 0, c * FLASH_QB:(c + 1) * FLASH_QB, :],
                                              (((1,), (1,)), ((), ())), preferred_element_type=F32))
                if bounded:
                    p = jnp.exp2(s[c] - m[c])
                    acc[c] = acc[c] + jnp.dot(vt, p.astype(BF16), preferred_element_type=F32)
                    continue
                m_new = jnp.maximum(m[c], jnp.max(s[c], 0, keepdims=True))
                alpha = jnp.exp2(m[c] - m_new)
                p = jnp.exp2(s[c] - m_new)
                acc[c] = alpha * acc[c] + jnp.dot(vt, p.astype(BF16), preferred_element_type=F32)
                m[c] = m_new
            s = s_next
        for c in chains:
            m_sc[c], acc_sc[c], s_sc[c] = m[c], acc[c], s[c]
        return carry

    lax.fori_loop(0, nk, body, 0)
    for c in range(nq):
        o_ref[0, 0, :, c * FLASH_QB:(c + 1) * FLASH_QB] = acc_sc[c, :HD] / acc_sc[c, HD:HD + 1]


def _flash(q, k, vt, q0, n_q, k0, n_keys, tq, tk, bound=None):
    b, hq = q.shape[:2]
    hk = k.shape[1]
    rep = hq // hk
    assert n_q % tq == 0 and n_keys % tk == 0 and tq % FLASH_QB == 0
    kb = min(FLASH_KB, tk)
    assert tk % kb == 0
    assert q0 % tq == 0 and k0 % n_keys == 0
    hv = vt.shape[2]
    nq = tq // FLASH_QB
    qb0, kb0 = q0 // tq, k0 // n_keys

    def call(bounded):
        extra_specs = [pl.BlockSpec((1, FLASH_QB), lambda bi, h, i: (0, 0))] if bounded else []
        extra = [jnp.full((1, FLASH_QB), bound, F32)] if bounded else []
        return pl.pallas_call(
            functools.partial(_flash_kernel, tk=tk, nk=n_keys // tk, nq=nq, kb=kb, bounded=bounded),
            grid=(b, hq, n_q // tq),
            in_specs=[pl.BlockSpec((1, 1, tq, LANES), lambda bi, h, i: (bi, h, qb0 + i, 0)),
                      pl.BlockSpec((1, 1, n_keys, LANES), lambda bi, h, i: (bi, h // rep, kb0, 0)),
                      pl.BlockSpec((1, 1, hv, n_keys), lambda bi, h, i: (bi, h // rep, 0, kb0))] + extra_specs,
            out_specs=pl.BlockSpec((1, 1, HD, tq), lambda bi, h, i: (bi, h, 0, i)),
            out_shape=jax.ShapeDtypeStruct((b, hq, HD, n_q), F32),
            scratch_shapes=[pltpu.VMEM((nq, 1, FLASH_QB), F32), pltpu.VMEM((nq, hv, FLASH_QB), F32),
                            pltpu.VMEM((nq, kb, FLASH_QB), F32)],
            compiler_params=_params("parallel", "parallel", "arbitrary"),
            name="flash_attention_bounded" if bounded else "flash_attention",
        )(q, k, vt, *extra)

    if bound is None:
        return call(False)
    return lax.cond(bound <= FLASH_MAX_BOUND, lambda: call(True), lambda: call(False))


def _na_kernel(q_ref, k_ref, vt_ref, kc_ref, vtc_ref, bias_ref, o_ref, *, rows):
    kc = kc_ref[0, 0]
    vtc = vtc_ref[0, 0]
    nq, nk = NA_QROWS * GRID_W, NA_KROWS * GRID_W
    batches = range(NA_ROWS // NA_QROWS)
    r0 = [pl.program_id(2) * NA_ROWS + bi * NA_QROWS for bi in batches]
    koff = [pl.multiple_of(jnp.clip(r - NA_WIN_R // 2, 0, rows - NA_KROWS) * GRID_W, 2 * GRID_W) for r in r0]
    variant = [jnp.where(r == 0, 0, jnp.where(r == rows - NA_QROWS, 2, 1)) for r in r0]
    q = [q_ref[0, 0, bi * nq:(bi + 1) * nq, :] for bi in batches]
    nt = (((1,), (1,)), ((), ()))

    def scores(bi):
        s_w = lax.dot_general(k_ref[0, 0, pl.ds(koff[bi], nk), :], q[bi], nt, preferred_element_type=F32)
        return s_w + bias_ref[variant[bi], 0], lax.dot_general(kc, q[bi], nt, preferred_element_type=F32)

    s_next = scores(0)
    for bi in batches:
        s_w, s_c = s_next
        if bi + 1 < len(batches):
            s_next = scores(bi + 1)
        m = jnp.maximum(jnp.max(s_w, 0, keepdims=True), jnp.max(s_c, 0, keepdims=True))
        p_w = jnp.exp2(s_w - m).astype(BF16)
        p_c = jnp.exp2(s_c - m).astype(BF16)
        acc = (jnp.dot(vt_ref[0, 0, :, pl.ds(koff[bi], nk)], p_w, preferred_element_type=F32)
               + jnp.dot(vtc, p_c, preferred_element_type=F32))
        o_ref[0, 0, :, bi * nq:(bi + 1) * nq] = acc[:HD] / acc[HD:HD + 1]


def _na_bias_table(rpb):
    n_dr, n_dc = 2 * NA_WIN_R - 1, 2 * NA_WIN_C - 1
    sel_r = np.zeros((3, NA_KROWS, NA_QROWS, n_dr), np.float32)
    ok_r = np.zeros((3, NA_KROWS, NA_QROWS), bool)
    for kind in range(3):
        for i in range(NA_QROWS):
            start = (0, i, NA_KROWS - NA_WIN_R)[kind]
            shift = (0, -(NA_WIN_R // 2), NA_QROWS - NA_KROWS)[kind]
            for j in range(start, start + NA_WIN_R):
                sel_r[kind, j, i, j + shift - i + NA_WIN_R - 1] = 1.0
                ok_r[kind, j, i] = True
    cols = np.arange(GRID_W)
    cs = np.clip(cols - NA_WIN_C // 2, 0, GRID_W - NA_WIN_C)
    sel_c = np.zeros((GRID_W, GRID_W, n_dc), np.float32)
    ok_c = np.zeros((GRID_W, GRID_W), bool)
    for qc in range(GRID_W):
        for kcol in range(cs[qc], cs[qc] + NA_WIN_C):
            sel_c[kcol, qc, kcol - qc + NA_WIN_C - 1] = 1.0
            ok_c[kcol, qc] = True
    tab = jnp.einsum('hrc,vjir,kqc->vhjkiq', rpb, sel_r, sel_c, precision=HIGHEST)
    ok = ok_r[:, None, :, None, :, None] & ok_c[None, None, None, :, None, :]
    tab = jnp.where(ok, tab * LOG2E, MASKED_LOGIT)
    return tab.reshape(3, rpb.shape[0], NA_KROWS * GRID_W, NA_QROWS * GRID_W)


def _neighbourhood(q, k, vt, seq, n_ctx, bias):
    b, h = q.shape[:2]
    rows = seq // GRID_W
    assert rows >= NA_KROWS and rows % NA_ROWS == 0 and seq % n_ctx == 0
    tile = NA_ROWS * GRID_W
    hv = vt.shape[2]
    cb = seq // n_ctx
    return pl.pallas_call(
        functools.partial(_na_kernel, rows=rows),
        grid=(b, h, rows // NA_ROWS),
        in_specs=[pl.BlockSpec((1, 1, tile, LANES), lambda bi, hi, i: (bi, hi, i, 0)),
                  pl.BlockSpec((1, 1, seq, LANES), lambda bi, hi, i: (bi, hi, 0, 0)),
                  pl.BlockSpec((1, 1, hv, seq), lambda bi, hi, i: (bi, hi, 0, 0)),
                  pl.BlockSpec((1, 1, n_ctx, LANES), lambda bi, hi, i: (bi, hi, cb, 0)),
                  pl.BlockSpec((1, 1, hv, n_ctx), lambda bi, hi, i: (bi, hi, 0, cb)),
                  pl.BlockSpec((3, 1, NA_KROWS * GRID_W, NA_QROWS * GRID_W), lambda bi, hi, i: (0, hi, 0, 0))],
        out_specs=pl.BlockSpec((1, 1, HD, tile), lambda bi, hi, i: (bi, hi, 0, i)),
        out_shape=jax.ShapeDtypeStruct((b, h, HD, seq), F32),
        compiler_params=_params("parallel", "parallel", "arbitrary"),
        name="neighbourhood_attention",
    )(q, k, vt, k, vt, bias)


def _seg_sum(t, ones_blk):
    return _dot_exact_rhs(t, ones_blk)


def _rw_prep_kernel(p_ref, prev_ref, next_ref, mu_ref, kk_ref, ka_ref, rk_ref, w0_ref, w2_ref, a0_ref,
                    a2_ref, ones_ref, r_o, v_o, nkk_o, bonus_o, lw0_o, kd0_o, bd0_o, lw1_o, kd1_o, bd1_o,
                    *, n_lat_tiles, n_tiles):
    lw_o, kd_o, bd_o = (lw0_o, lw1_o), (kd0_o, kd1_o), (bd0_o, bd1_o)
    i = pl.program_id(1)
    u = p_ref[0, :, BR:]
    first = jnp.logical_or(i == 0, i == n_lat_tiles)
    last = jnp.logical_or(i == n_lat_tiles - 1, i == n_tiles - 1)
    prev_row = jnp.where(first, 0.0, prev_ref[0, F32_SUBLANES - 1:F32_SUBLANES, BR:])
    next_row = jnp.where(last, 0.0, next_ref[0, 0:1, BR:])
    row = lax.broadcasted_iota(jnp.int32, u.shape, 0)
    up = jnp.where(row == 0, prev_row, pltpu.roll(u, 1, 0))
    dn = jnp.where(row == TM - 1, next_row, pltpu.roll(u, TM - 1, 0))
    u = u + mu_ref[...] * (0.5 * (up + dn) - u)
    r, k, v = u[:, :BR], u[:, BR:2 * BR], u[:, 2 * BR:3 * BR]
    wl, al = u[:, 3 * BR:3 * BR + 2 * RW_LORA], u[:, 3 * BR + 2 * RW_LORA:]
    ones_blk = ones_ref[...]
    kk = k * kk_ref[...]
    kk = kk * lax.rsqrt(_seg_sum(kk * kk, ones_blk) + 1e-12)
    r_o[0] = r
    v_o[0] = v
    nkk_o[0] = -kk
    k_sum = jnp.zeros_like(k)
    for d in range(2):
        wl_d = wl[:, d * RW_LORA:(d + 1) * RW_LORA]
        al_d = al[:, d * RW_LORA:(d + 1) * RW_LORA]
        t = w0_ref[d] + _dot(jnp.tanh(wl_d), w2_ref[d])
        lw_o[d][0] = -RW_DECAY_SCALE * _sigmoid(t)
        a = _sigmoid(a0_ref[d] + _dot(al_d, a2_ref[d]))
        k_d = k * (1.0 + (a - 1.0) * ka_ref[...])
        kd_o[d][0] = k_d
        bd_o[d][0] = kk * a
        k_sum = k_sum + k_d
    bonus_o[0] = _seg_sum(r * k_sum * rk_ref[...], ones_blk) * v


def _rw_prep(pb, mu, kk, ka, rk, w0, w2, a0, a2, ones_blk, n_lat_tiles):
    b, n, ncol = pb.shape
    nt = n // TM
    r8 = TM // F32_SUBLANES
    full = lambda a: pl.BlockSpec(a.shape, lambda bi, i: (0,) * a.ndim)
    tok = pl.BlockSpec((1, TM, BR), lambda bi, i: (bi, i, 0))
    one = jax.ShapeDtypeStruct((b, n, BR), F32)
    return pl.pallas_call(
        functools.partial(_rw_prep_kernel, n_lat_tiles=n_lat_tiles, n_tiles=nt),
        grid=(b, nt),
        in_specs=[pl.BlockSpec((1, TM, ncol), lambda bi, i: (bi, i, 0)),
                  pl.BlockSpec((1, F32_SUBLANES, ncol), lambda bi, i: (bi, jnp.maximum(i * r8 - 1, 0), 0)),
                  pl.BlockSpec((1, F32_SUBLANES, ncol),
                               lambda bi, i: (bi, jnp.minimum((i + 1) * r8, n // F32_SUBLANES - 1), 0)),
                  full(mu), full(kk), full(ka), full(rk), full(w0), full(w2), full(a0), full(a2),
                  full(ones_blk)],
        out_specs=[tok] * 10,
        out_shape=[one] * 10,
        compiler_params=_params("parallel", "parallel"),
        name="rwkv_prep",
    )(pb, pb, pb, mu, kk, ka, rk, w0, w2, a0, a2, ones_blk)


def _rw_scan_kernel(r_ref, v_ref, a_ref, lw_ref, k_ref, b_ref, y_ref, st_ref, *, rev, n_b):
    c, w = CHUNK, RW_PACK * HD

    @pl.when(pl.program_id(0) == 0)
    def _():
        st_ref[...] = jnp.zeros(st_ref.shape, F32)

    tt = lax.broadcasted_iota(jnp.int32, (c, w), 0)
    ss = lax.broadcasted_iota(jnp.int32, (c, w), 1) & (c - 1)
    strict = (ss > tt) if rev else (ss < tt)
    incl = (ss >= tt) if rev else (ss <= tt)
    eye = (ss == tt).astype(F32)
    t2 = lax.broadcasted_iota(jnp.int32, (c, c), 0)
    s2 = lax.broadcasted_iota(jnp.int32, (c, c), 1)
    m_incl = ((s2 >= t2) if rev else (s2 <= t2)).astype(BF16)
    hd_log2 = HD.bit_length() - 1
    blk = ((lax.broadcasted_iota(jnp.int32, (w, w), 0) >> hd_log2)
           == (lax.broadcasted_iota(jnp.int32, (w, w), 1) >> hd_log2))

    def bd(t):
        tb = t.astype(BF16)
        return jnp.where(blk, jnp.concatenate([tb] * RW_PACK, axis=0), jnp.zeros((), BF16))

    def dtn(x, y):
        return lax.dot_general(x.astype(BF16), y.astype(BF16), (((0,), (0,)), ((), ())),
                               preferred_element_type=F32)

    chunks = tuple(range(RW_TT // c))
    chunks = chunks[::-1] if rev else chunks
    chains = [(bi, g) for bi in range(n_b) for g in range(BR // w)]
    units = [(ci, bi, g) for ci in chunks for (bi, g) in chains]
    sl = lambda ref, u: ref[u[1], u[0] * c:(u[0] + 1) * c, u[2] * w:(u[2] + 1) * w]

    lw = [sl(lw_ref, u) for u in units]
    cum = []
    for t in lw:
        hi, mid, lo = _split3(t)
        cum.append(jnp.dot(m_incl, hi, preferred_element_type=F32)
                   + jnp.dot(m_incl, mid, preferred_element_type=F32)
                   + jnp.dot(m_incl, lo, preferred_element_type=F32))
    tot = [jnp.sum(t, 0, keepdims=True) for t in lw]
    p_inv = [jnp.exp(-x) for x in cum]
    p_end = [jnp.exp(t - x) for t, x in zip(tot, cum)]
    a_t = [sl(a_ref, u) * jnp.exp(x - l) for u, x, l in zip(units, cum, lw)]
    r_t = [sl(r_ref, u) * jnp.exp(x) for u, x in zip(units, cum)]
    b_raw = [sl(b_ref, u) for u in units]
    k_raw = [sl(k_ref, u) for u in units]
    v = [sl(v_ref, u) for u in units]
    b_t = [x * p for x, p in zip(b_raw, p_inv)]
    k_t = [x * p for x, p in zip(k_raw, p_inv)]
    b_h = [x * p for x, p in zip(b_raw, p_end)]
    k_h = [x * p for x, p in zip(k_raw, p_end)]
    ar = [jnp.concatenate([x, y], axis=0) for x, y in zip(a_t, r_t)]
    g_b = [_dot_nt(x, bd(y)) for x, y in zip(ar, b_t)]
    g_k = [_dot_nt(x, bd(y)) for x, y in zip(ar, k_t)]
    lab = [jnp.where(strict, x[:c], 0.0) for x in g_b]
    lak = [jnp.where(strict, x[:c], 0.0) for x in g_k]
    qrb = [jnp.where(incl, x[c:], 0.0) for x in g_b]
    qrk = [jnp.where(incl, x[c:], 0.0) for x in g_k]
    tinv = [eye + x for x in lab]
    lp = lab
    stack = lambda x, y: jnp.concatenate([x, y], axis=0)
    for i in range(6):
        lp_bd = [bd(x) for x in lp]
        if i == 0:
            lp = [_dot(x, y) for x, y in zip(lp, lp_bd)]
        elif i < 5:
            both = [_dot(stack(t, x), y) for t, x, y in zip(tinv, lp, lp_bd)]
            tinv = [t + z[:c] for t, z in zip(tinv, both)]
            lp = [z[c:] for z in both]
        else:
            tinv = [t + _dot(t, y) for t, y in zip(tinv, lp_bd)]
    both = [_dot(stack(x, y), bd(z)) for x, y, z in zip(lak, qrk, v)]
    wv = [z[:c] for z in both]
    y_loc = [z[c:] for z in both]
    a_hat = [_dot(x, bd(y)) for x, y in zip(tinv, a_t)]
    u_hat = [_dot(x, bd(y)) for x, y in zip(tinv, wv)]
    ar_hat = [stack(x, y) for x, y in zip(a_hat, r_t)]
    bk_h = [stack(x, y) for x, y in zip(b_h, k_h)]
    p_c = [jnp.exp(t) for t in tot]

    n_ch = len(chains)
    for j in range(len(chunks)):
        idx = range(j * n_ch, (j + 1) * n_ch)
        st = [st_ref[q] for q in range(n_ch)]
        ur = [_dot_nt(ar_hat[i], s) for i, s in zip(idx, st)]
        u = [z[:c] + u_hat[i] for i, z in zip(idx, ur)]
        y = [z[c:] + _dot(qrb[i], bd(x)) + y_loc[i] for i, z, x in zip(idx, ur, u)]
        new = [s * p_c[i] + jnp.where(blk, dtn(stack(x, v[i]), bk_h[i]), 0.0) for i, s, x in zip(idx, st, u)]
        for q, i in enumerate(idx):
            y_ref[units[i][1], units[i][0] * c:(units[i][0] + 1) * c, units[i][2] * w:(units[i][2] + 1) * w] = y[q]
            st_ref[q] = new[q]


def _rw_scan(r, v, nkk, lw, kd, bd, n_ctx, rev):
    b, n, _ = r.shape
    assert n % RW_TT == 0 and n_ctx % RW_TT == 0
    nblk = n // RW_TT
    nctx = n_ctx // RW_TT
    if rev:
        blk_of = lambda i: nblk - 1 - i
    else:
        blk_of = lambda i: jnp.where(i < nctx, nblk - nctx + i, i - nctx)
    spec = pl.BlockSpec((b, RW_TT, BR), lambda i: (0, blk_of(i), 0))
    n_chains = b * (BR // (RW_PACK * HD))
    return pl.pallas_call(
        functools.partial(_rw_scan_kernel, rev=rev, n_b=b),
        grid=(nblk,),
        in_specs=[spec] * 6,
        out_specs=spec,
        out_shape=jax.ShapeDtypeStruct((b, n, BR), F32),
        scratch_shapes=[pltpu.VMEM((n_chains, RW_PACK * HD, RW_PACK * HD), F32)],
        compiler_params=_params("arbitrary"),
        name="rwkv_scan",
    )(r, v, nkk, lw, kd, bd)


def _merge_kernel(x_ref, h_ref, ya_ref, yac_ref, ybf_ref, ybr_ref, yc_ref, ycc_ref, yd_ref, ydc_ref,
                  ga_ref, gb_ref, gc_ref, gd_ref, bonus_ref, gnw_ref, gnb_ref, ones_ref, mgw_ref, mgb_ref,
                  wbr_ref, wout_ref, gt_ref, o_ref, *, n_lat_tiles):
    h = h_ref[0]
    ones_blk = ones_ref[...]
    is_ctx = pl.program_id(1) >= n_lat_tiles

    def token_major(lat_ref, ctx_ref):
        yt = jnp.where(is_ctx, ctx_ref[0], lat_ref[0])
        return yt.reshape(BR, TM).T

    yb = ybf_ref[0] + ybr_ref[0]
    mean = _seg_sum(yb, ones_blk) * (1.0 / HD)
    cen = yb - mean
    var = _seg_sum(cen * cen, ones_blk) * (1.0 / HD)
    yb = cen * lax.rsqrt(var + RW_GN_EPS) * gnw_ref[...] + gnb_ref[...] + bonus_ref[0]
    ys = (token_major(ya_ref, yac_ref), yb, token_major(yc_ref, ycc_ref), token_major(yd_ref, ydc_ref))
    gs = (ga_ref[0], gb_ref[0], gc_ref[0], gd_ref[0])
    acc = None
    for i in range(4):
        gate = _sigmoid(jnp.dot(h, mgw_ref[i], preferred_element_type=F32) + mgb_ref[i])
        term = gate * _dot(ys[i] * _silu(gs[i]), wbr_ref[i])
        acc = term if acc is None else acc + term
    o_ref[0] = x_ref[0] + gt_ref[0, 0] * _dot(acc, wout_ref[...])


def _merge(x_all, h, ya, ybf, ybr, yc, yd, pa, pb, pc, pd, bonus, gnw, gnb, ones_blk, mgw, mgb, wbr, wout, gt,
           n_lat_tiles, n_out):
    b, n, d = x_all.shape
    kind = lambda i: jnp.where(i < n_lat_tiles, 1, 0)
    tok = lambda w: pl.BlockSpec((1, TM, w), lambda bi, i: (bi, i, 0))
    full = lambda a: pl.BlockSpec(a.shape, lambda bi, i: (0,) * a.ndim)
    lat = pl.BlockSpec((1, N_HEADS, HD, TM), lambda bi, i: (bi, 0, 0, jnp.minimum(i, n_lat_tiles - 1)))
    ctx = pl.BlockSpec((1, N_HEADS, HD, TM), lambda bi, i: (bi, 0, 0, 0))
    return pl.pallas_call(
        functools.partial(_merge_kernel, n_lat_tiles=n_lat_tiles),
        grid=(b, n_out // TM),
        in_specs=[tok(d), tok(d), lat, ctx, tok(BR), tok(BR), lat, ctx, lat, ctx,
                  tok(BR), tok(BR), tok(BR), tok(BR), tok(BR),
                  full(gnw), full(gnb), full(ones_blk), full(mgw), full(mgb), full(wbr), full(wout),
                  pl.BlockSpec((1, 1, 1, d), lambda bi, i: (bi, kind(i), 0, 0))],
        out_specs=tok(d),
        out_shape=jax.ShapeDtypeStruct((b, n_out, d), F32),
        compiler_params=_params("parallel", "parallel"),
        name="merge_out",
    )(x_all, h, ya[0], ya[1], ybf, ybr, yc[0], yc[1], yd[0], yd[1], pa, pb, pc, pd, bonus,
      gnw, gnb, ones_blk, mgw, mgb, wbr, wout, gt)


def _pad_heads(w, n_heads, d):
    lead = w.shape[:-1]
    w = w.reshape(lead + (n_heads, d))
    w = jnp.pad(w, [(0, 0)] * len(lead) + [(0, 0), (0, LANES - d)])
    return w.reshape(lead + (n_heads * LANES,))


def _dup_heads(w, n_heads, d):
    lead = w.shape[:-1]
    w = w.reshape(lead + (n_heads, d))
    return jnp.concatenate([w, w], axis=-1).reshape(lead + (n_heads * 2 * d,))


def _pad_vec(g):
    return jnp.pad(g, (0, LANES - g.shape[0])).reshape(1, LANES)


def _rope_tables(n_lat, n_ctx, d_rot, lo, dup=False):
    t = np.arange(n_lat)
    row = (t // GRID_W).astype(np.float32)
    col = (t % GRID_W).astype(np.float32)
    n_freq = d_rot // 4
    inv = np.float32(ROPE_BASE) ** (-np.arange(n_freq, dtype=np.float32) / np.float32(n_freq))
    ang = np.concatenate([row[:, None] * inv, col[:, None] * inv], -1).astype(np.float32)
    cos, sin = np.cos(ang), np.sin(ang)
    half = d_rot // 2
    cos_t = np.ones((n_lat + n_ctx, LANES), np.float32)
    sin_t = np.zeros((n_lat + n_ctx, LANES), np.float32)
    for base in range(lo, LANES if dup else lo + 1, d_rot):
        cos_t[:n_lat, base:base + half] = cos
        cos_t[:n_lat, base + half:base + d_rot] = cos
        sin_t[:n_lat, base:base + half] = -sin
        sin_t[:n_lat, base + half:base + d_rot] = sin
    return jnp.asarray(cos_t), jnp.asarray(sin_t)


def _pick_tile(n, candidates):
    for c in candidates:
        if n % c == 0:
            return c
    raise ValueError(f"no tile for {n}")


def kernel(x, c, ctx, c_ctx, norm_g, mod_w, mod_b, w_in, na_qg, na_kg, na_rpb, rw_mu, rw_w0, rw_w2, rw_a0, rw_a2, rw_kk, rw_ka, rw_rk, rw_gn_w, rw_gn_b, mla_qa_g, mla_kva_g, mla_wuq, mla_wukv, mla_qg, mla_kg, gqa_qg, gqa_kg, mg_w, mg_b, w_br, w_out):
    bsz, seq, d = x.shape
    n_ctx = ctx.shape[1]
    depth = w_in.shape[0]
    assert d == D_MODEL and n_ctx == TM and seq % (NA_ROWS * GRID_W) == 0
    n = seq + n_ctx
    n_lat_tiles = seq // TM

    gqa_cos, gqa_sin = _rope_tables(seq, n_ctx, HD, 0, dup=True)
    mla_cos, mla_sin = _rope_tables(seq, n_ctx, MLA_ROPE, MLA_NOPE)
    ones_blk = jnp.asarray(np.kron(np.eye(N_HEADS), np.ones((HD, HD))), BF16)
    tq = _pick_tile(seq, (2048, 1024, 512, 256))
    tk = _pick_tile(n, (3328, 1280, 256))

    def context_attend(q, k, vt, need_ctx):
        if not need_ctx:
            return jnp.zeros((bsz, N_HEADS, HD, n_ctx), F32)
        return _flash(q, k, vt, seq, n_ctx, seq, n_ctx, n_ctx, n_ctx)

    def attend(q, k, vt, bound, need_ctx):
        return _flash(q, k, vt, 0, seq, 0, n, tq, tk, bound), context_attend(q, k, vt, need_ctx)

    def logit_bound(q_gain, k_gain, d_head):
        return (d_head ** 0.5 * LOG2E * ROUNDING_MARGIN) * jnp.max(jnp.abs(q_gain)) * jnp.max(jnp.abs(k_gain))

    x_all = jnp.concatenate([x, ctx], axis=1)
    cc = jnp.concatenate([c_ctx[None], c], axis=0)
    for l in range(depth):
        need_ctx = l + 1 < depth
        w = w_in[l]
        o = 0
        aq, ak, av, ag = (w[:, o + i * BR:o + (i + 1) * BR] for i in range(4))
        o += 4 * BR
        bu, bg = w[:, o:o + U_COLS], w[:, o + U_COLS:o + U_COLS + BR]
        o += U_COLS + BR
        ccq, cckv = w[:, o:o + MLA_RANK], w[:, o + MLA_RANK:o + 2 * MLA_RANK]
        ckr = w[:, o + 2 * MLA_RANK:o + 2 * MLA_RANK + MLA_ROPE]
        cg = w[:, o + 2 * MLA_RANK + MLA_ROPE:o + 2 * MLA_RANK + MLA_ROPE + BR]
        o += 2 * MLA_RANK + MLA_ROPE + BR
        dq = w[:, o:o + BR]
        dk = w[:, o + BR:o + BR + GQA_KV_HEADS * HD]
        dv = w[:, o + BR + GQA_KV_HEADS * HD:o + BR + 2 * GQA_KV_HEADS * HD]
        dg = w[:, o + BR + 2 * GQA_KV_HEADS * HD:]
        ckr_t = jnp.pad(ckr, ((0, 0), (MLA_NOPE, LANES - MLA_NOPE - MLA_ROPE)))
        w_a = jnp.concatenate([ag, _pad_heads(aq, N_HEADS, HD), _pad_heads(ak, N_HEADS, HD), av], 1).astype(BF16)
        w_b = jnp.concatenate([bg, bu], 1).astype(BF16)
        w_c = jnp.concatenate([cg, ccq, cckv, ckr_t], 1).astype(BF16)
        w_d = jnp.concatenate([dg, _dup_heads(dq, N_HEADS, HD), _dup_heads(dk, GQA_KV_HEADS, HD), dv], 1).astype(BF16)
        wuq = _pad_heads(mla_wuq[l], N_HEADS, MLA_NOPE + MLA_ROPE).astype(BF16)
        wukv = mla_wukv[l].reshape(MLA_RANK, N_HEADS, MLA_NOPE + HD)
        wuk = _pad_heads(wukv[:, :, :MLA_NOPE].reshape(MLA_RANK, -1), N_HEADS, MLA_NOPE).astype(BF16)
        wuv = wukv[:, :, MLA_NOPE:].reshape(MLA_RANK, -1).T.astype(BF16)

        mod = _modulation(cc, mod_w[l], mod_b[l])
        sh, sc, gt = jnp.split(mod, 3, axis=-1)
        pair = lambda m: jnp.stack([jnp.broadcast_to(m[0], (bsz, d)), m[1:]], axis=1)[:, :, None, :]
        sh2, sc2, gt2 = pair(sh), pair(sc), pair(gt)

        row = lambda t: t.reshape(1, -1)
        norm = (x_all, norm_g[l], sc2, sh2)

        ga, qa, ka, vat, h = _branch_proj(*norm, w_a, n_lat_tiles, True, _na_epilogue,
                                          (_pad_vec(na_qg[l]), _pad_vec(na_kg[l])), (), N_HEADS, "proj_na")
        ya = (_neighbourhood(qa, ka, vat, seq, n_ctx, _na_bias_table(na_rpb[l])),
              context_attend(qa, ka, vat, need_ctx))

        pb, pc = _norm_proj_pair(*norm, w_b, w_c, n_lat_tiles)
        r, v, nkk, bonus, lw0, kd0, bd0, lw1, kd1, bd1 = _rw_prep(
            pb, row(rw_mu[l]), row(rw_kk[l]), row(rw_ka[l]), row(rw_rk[l]),
            rw_w0[l][:, None, :], rw_w2[l].astype(BF16), rw_a0[l][:, None, :], rw_a2[l].astype(BF16),
            ones_blk, n_lat_tiles)
        ybf = _rw_scan(r, v, nkk, lw0, kd0, bd0, n_ctx, False)
        ybr = _rw_scan(r, v, nkk, lw1, kd1, bd1, n_ctx, True)

        qc, kc, vct = _branch_prep(pc, _mla_epilogue,
                                       (row(mla_qa_g[l]), row(mla_kva_g[l]), wuq, wuk, wuv,
                                        _pad_vec(mla_qg[l]), _pad_vec(mla_kg[l])), (mla_cos, mla_sin),
                                       N_HEADS, "mla_prep")
        yc = attend(qc, kc, vct, logit_bound(mla_qg[l], mla_kg[l], MLA_NOPE + MLA_ROPE), need_ctx)

        gd, qd, kd_, vdt = _branch_proj(*norm, w_d, n_lat_tiles, False, _gqa_epilogue,
                                        (jnp.tile(gqa_qg[l], 2)[None], jnp.tile(gqa_kg[l], 2)[None]), (gqa_cos, gqa_sin),
                                        GQA_KV_HEADS, "proj_gqa")
        yd = attend(qd, kd_, vdt, logit_bound(gqa_qg[l], gqa_kg[l], HD), need_ctx)

        n_out = n if need_ctx else seq
        x_all = _merge(x_all, h, ya, ybf, ybr, yc, yd, ga, pb, pc, gd, bonus,
                       row(rw_gn_w[l]), row(rw_gn_b[l]), ones_blk,
                       mg_w[l].astype(BF16), mg_b[l][:, None, :], w_br[l].astype(BF16), w_out[l].astype(BF16),
                       gt2, n_lat_tiles, n_out)
    return x_all
```
